```python
import math
import jax, jax.numpy as jnp
from jax import lax
import numpy as np

D_MODEL = 1024
BATCH = 16
SEQ = 4096
DEPTH = 1

PLE_DIM = 256
EPS = 1e-6
FOX_HEADS = 8
FOX_HEAD_DIM = 64
FOX_BLOCK = 128
FOX_WIDTH = FOX_HEADS * FOX_HEAD_DIM
GDN_HEADS = 4
GDN_HEAD_DIM = 128
GDN_CONV = 4
GDN_CHUNK = 64
GDN_WIDTH = GDN_HEADS * GDN_HEAD_DIM
N_GROUPS = 4
EXPERTS_PER_GROUP = 8
N_EXPERTS = N_GROUPS * EXPERTS_PER_GROUP
TOP_K_IN_GROUP = 2
EXPERT_FF = 256
IN_SPLITS = (FOX_WIDTH, FOX_WIDTH, FOX_WIDTH, FOX_HEADS, 3 * GDN_WIDTH, GDN_HEADS, GDN_HEADS, GDN_WIDTH, D_MODEL, D_MODEL)
IN_WIDTH = 3 * FOX_WIDTH + FOX_HEADS + 4 * GDN_WIDTH + 2 * GDN_HEADS + 2 * D_MODEL

kernel_name = 'hybrid_fox_gdn_hmoe_block'


def rmsnorm(x, g):
    xf = x.astype(jnp.float32)
    y = xf * lax.rsqrt(jnp.mean(xf * xf, axis=-1, keepdims=True) + EPS)
    return (y * g.astype(jnp.float32)).astype(x.dtype)


def l2norm(x):
    xf = x.astype(jnp.float32)
    return xf * lax.rsqrt(jnp.sum(xf * xf, axis=-1, keepdims=True) + EPS)


def _split_columns(z, sizes):
    parts, start = [], 0
    for n in sizes:
        parts.append(z[..., start:start + n])
        start += n
    return parts


def _heads(t, n):
    return t.reshape(t.shape[:-1] + (n, -1))


def forgetting_attention(q, k, v, f_logit, b_forget):
    log_f = jax.nn.log_sigmoid(f_logit.astype(jnp.float32) + b_forget.astype(jnp.float32))
    cum = jnp.transpose(jnp.cumsum(log_f, axis=1), (0, 2, 1))
    scale = FOX_HEAD_DIM ** -0.5
    seq = q.shape[1]
    outs = []
    for blk in range(seq // FOX_BLOCK):
        lo, hi = blk * FOX_BLOCK, (blk + 1) * FOX_BLOCK
        s = jnp.einsum('bqhd,bkhd->bhqk', q[:, lo:hi], k[:, :hi], preferred_element_type=jnp.float32) * scale
        s = s + cum[:, :, lo:hi, None] - cum[:, :, None, :hi]
        causal = jnp.arange(hi)[None, :] <= (lo + jnp.arange(FOX_BLOCK))[:, None]
        pr = jax.nn.softmax(jnp.where(causal, s, -jnp.inf), axis=-1)
        outs.append(jnp.einsum('bhqk,bkhd->bqhd', pr.astype(v.dtype), v[:, :hi]))
    return jnp.concatenate(outs, axis=1)


def causal_depthwise_conv(x, w):
    c = x.shape[-1]
    return lax.conv_general_dilated(x, w[:, None, :].astype(x.dtype), window_strides=(1,),
                                    padding=[(GDN_CONV - 1, 0)],
                                    dimension_numbers=('NWC', 'WIO', 'NWC'),
                                    feature_group_count=c)


def gated_delta_rule(q, k, v, g, beta):
    bsz, seq, nh, dk = q.shape
    dv = v.shape[-1]
    C = GDN_CHUNK
    n = seq // C

    def chunks(t):
        t = t.reshape((bsz, n, C, nh) + t.shape[3:])
        return jnp.moveaxis(t, (1, 3), (0, 2))

    qc, kc, vc = chunks(q), chunks(k), chunks(v)
    bc = chunks(beta)
    gc = jnp.cumsum(chunks(g), axis=-1)
    idx = jnp.arange(C)
    lower = idx[:, None] >= idx[None, :]
    strict = idx[:, None] > idx[None, :]
    decay = jnp.exp(jnp.where(lower, gc[..., :, None] - gc[..., None, :], -jnp.inf))
    kb = kc * bc[..., None]
    lmat = jnp.where(strict, jnp.einsum('nbhid,nbhjd->nbhij', kb, kc) * decay, 0.0)
    amat = lmat + jnp.eye(C, dtype=jnp.float32)
    rhs = jnp.concatenate([vc * bc[..., None], kb * jnp.exp(gc)[..., None]], axis=-1)
    sol = lax.linalg.triangular_solve(amat, rhs, left_side=True, lower=True, unit_diagonal=True)
    u, w = sol[..., :dv], sol[..., dv:]
    intra = jnp.einsum('nbhid,nbhjd->nbhij', qc, kc) * decay
    q_dec = qc * jnp.exp(gc)[..., None]
    k_dec = kc * jnp.exp(gc[..., -1:] - gc)[..., None]
    g_last = jnp.exp(gc[..., -1])

    def step(state, xs):
        q_i, k_i, u_i, w_i, a_i, g_i = xs
        v_new = u_i - jnp.einsum('bhcd,bhde->bhce', w_i, state)
        o = jnp.einsum('bhcd,bhde->bhce', q_i, state) + jnp.einsum('bhij,bhje->bhie', a_i, v_new)
        state = state * g_i[..., None, None] + jnp.einsum('bhcd,bhce->bhde', k_i, v_new)
        return state, o

    state0 = jnp.zeros((bsz, nh, dk, dv), jnp.float32)
    _, o = lax.scan(step, state0, (q_dec, k_dec, u, w, intra, g_last))
    return jnp.moveaxis(o, (0, 2), (1, 3)).reshape(bsz, seq, nh, dv)


def hybrid_mixer(h, w_in, b_forget, conv_w, a_log, dt_bias, g_onorm, w_o_fox, w_o_delta, w_out):
    f32 = jnp.float32
    z = h @ w_in
    fq, fk, fv, ff, qkv, da, db, dz, gate_fox, gate_delta = _split_columns(z, IN_SPLITS)
    y_fox = forgetting_attention(_heads(fq, FOX_HEADS), _heads(fk, FOX_HEADS), _heads(fv, FOX_HEADS), ff, b_forget)
    y_fox = y_fox.reshape(h.shape[:-1] + (FOX_WIDTH,)) @ w_o_fox
    qkv = jax.nn.silu(causal_depthwise_conv(qkv, conv_w)).astype(f32)
    dq, dk, dv = _split_columns(qkv, (GDN_WIDTH, GDN_WIDTH, GDN_WIDTH))
    q = l2norm(_heads(dq, GDN_HEADS)) * GDN_HEAD_DIM ** -0.5
    k = l2norm(_heads(dk, GDN_HEADS))
    v = _heads(dv, GDN_HEADS)
    beta = jax.nn.sigmoid(db.astype(f32))
    g = -jnp.exp(a_log.astype(f32)) * jax.nn.softplus(da.astype(f32) + dt_bias.astype(f32))
    o = gated_delta_rule(q, k, v, g, beta)
    o = rmsnorm(o, g_onorm) * jax.nn.silu(_heads(dz, GDN_HEADS).astype(f32))
    y_delta = o.reshape(h.shape[:-1] + (GDN_WIDTH,)).astype(h.dtype) @ w_o_delta
    merged = jax.nn.sigmoid(gate_fox) * y_fox + jax.nn.sigmoid(gate_delta) * y_delta
    return merged @ w_out


def hierarchical_moe(h, w_group, b_group, w_router, b_router, w_gate, w_up, w_down):
    f32 = jnp.float32
    bsz, seq, d = h.shape
    t = h.reshape(-1, d)
    gl = (t @ w_group).astype(f32) + b_group.astype(f32)
    pg = jax.nn.softmax(gl, axis=-1)
    g_sel = jnp.argmax(gl, axis=-1)
    p_sel = jnp.take_along_axis(pg, g_sel[:, None], axis=-1)
    el = ((t @ w_router).astype(f32) + b_router.astype(f32)).reshape(-1, N_GROUPS, EXPERTS_PER_GROUP)
    el = jnp.take_along_axis(el, g_sel[:, None, None], axis=1)[:, 0]
    pe = jax.nn.softmax(el, axis=-1)
    top_p, top_i = lax.top_k(pe, TOP_K_IN_GROUP)
    top_p = top_p / jnp.sum(top_p, axis=-1, keepdims=True)
    w_grp = jnp.sum(jax.nn.one_hot(top_i, EXPERTS_PER_GROUP, dtype=f32) * top_p[..., None], axis=1)
    gate = (jax.nn.one_hot(g_sel, N_GROUPS, dtype=f32)[:, :, None] * (p_sel * w_grp)[:, None, :]).reshape(-1, N_EXPERTS)
    gate = gate.astype(t.dtype)
    out = jnp.zeros_like(t)
    for e in range(N_EXPERTS):
        hid = jax.nn.silu(t @ w_gate[e]) * (t @ w_up[e])
        out = out + gate[:, e:e + 1] * (hid @ w_down[e])
    return out.reshape(bsz, seq, d)


def setup_inputs(seed: int = 0) -> dict:
    key = jax.random.key(seed)
    ks = jax.random.split(key, 24)
    f32 = jnp.float32
    L = DEPTH

    def nrm(k, shape, fan_in):
        return jax.random.normal(k, shape, f32) * fan_in ** -0.5

    def gain(k, shape):
        return 1.0 + 0.02 * jax.random.normal(k, shape, f32)

    dt = jnp.exp(jax.random.uniform(ks[7], (L, GDN_HEADS), f32, minval=math.log(1e-3), maxval=math.log(1e-1)))
    return {
        'x': jax.random.normal(ks[0], (BATCH, SEQ, D_MODEL), f32),
        'p': jax.random.normal(ks[1], (L, BATCH, SEQ, PLE_DIM), f32),
        'g_mix': gain(ks[2], (L, D_MODEL)),
        'w_in': nrm(ks[3], (L, D_MODEL, IN_WIDTH), D_MODEL),
        'b_forget': jax.random.uniform(ks[4], (L, FOX_HEADS), f32, minval=1.0, maxval=6.0),
        'conv_w': nrm(ks[5], (L, GDN_CONV, 3 * GDN_WIDTH), GDN_CONV),
        'a_log': jnp.log(jax.random.uniform(ks[6], (L, GDN_HEADS), f32, minval=1.0, maxval=16.0)),
        'dt_bias': dt + jnp.log(-jnp.expm1(-dt)),
        'g_onorm': gain(ks[8], (L, GDN_HEAD_DIM)),
        'w_o_fox': nrm(ks[9], (L, FOX_WIDTH, D_MODEL), FOX_WIDTH),
        'w_o_delta': nrm(ks[10], (L, GDN_WIDTH, D_MODEL), GDN_WIDTH),
        'w_out': nrm(ks[11], (L, D_MODEL, D_MODEL), D_MODEL),
        'g_ffn': gain(ks[12], (L, D_MODEL)),
        'w_group': nrm(ks[13], (L, D_MODEL, N_GROUPS), D_MODEL),
        'b_group': 0.01 * jax.random.normal(ks[14], (L, N_GROUPS), f32),
        'w_router': nrm(ks[15], (L, D_MODEL, N_EXPERTS), D_MODEL),
        'b_router': 0.01 * jax.random.normal(ks[16], (L, N_EXPERTS), f32),
        'w_gate': nrm(ks[17], (L, N_EXPERTS, D_MODEL, EXPERT_FF), D_MODEL),
        'w_up': nrm(ks[18], (L, N_EXPERTS, D_MODEL, EXPERT_FF), D_MODEL),
        'w_down': nrm(ks[19], (L, N_EXPERTS, EXPERT_FF, D_MODEL), EXPERT_FF),
        'g_ple': gain(ks[20], (L, D_MODEL)),
        'w_ple_gate': nrm(ks[21], (L, D_MODEL, D_MODEL), D_MODEL),
        'w_ple_proj': nrm(ks[22], (L, PLE_DIM, D_MODEL), PLE_DIM),
        'g_final': gain(ks[23], (D_MODEL,)),
    }


def reference(x, p, g_mix, w_in, b_forget, conv_w, a_log, dt_bias, g_onorm, w_o_fox, w_o_delta, w_out,
              g_ffn, w_group, b_group, w_router, b_router, w_gate, w_up, w_down,
              g_ple, w_ple_gate, w_ple_proj, g_final):
    for i in range(DEPTH):
        h = rmsnorm(x, g_mix[i])
        x = x + hybrid_mixer(h, w_in[i], b_forget[i], conv_w[i], a_log[i], dt_bias[i], g_onorm[i],
                             w_o_fox[i], w_o_delta[i], w_out[i])
        x = x + hierarchical_moe(rmsnorm(x, g_ffn[i]), w_group[i], b_group[i], w_router[i], b_router[i],
                                 w_gate[i], w_up[i], w_down[i])
        ple_gate = jax.nn.sigmoid(rmsnorm(x, g_ple[i]) @ w_ple_gate[i])
        x = x + ple_gate * (p[i] @ w_ple_proj[i])
    return rmsnorm(x, g_final)
```

```python
import functools

import jax
import jax.numpy as jnp
from jax import lax
from jax.experimental import pallas as pl
from jax.experimental.pallas import tpu as pltpu

F32 = jnp.float32
BF16 = jnp.bfloat16
EPS = 1e-6
NEG = -1e30

FOX_HEADS = 8
FOX_HEAD_DIM = 64
FOX_WIDTH = FOX_HEADS * FOX_HEAD_DIM
GDN_HEADS = 4
GDN_HEAD_DIM = 128
GDN_CONV = 4
GDN_CHUNK = 64
GDN_WIDTH = GDN_HEADS * GDN_HEAD_DIM
N_GROUPS = 4
EXPERTS_PER_GROUP = 8
N_EXPERTS = N_GROUPS * EXPERTS_PER_GROUP
EXPERT_FF = 256
LANES = 128
EXPERT_LANE0 = N_GROUPS

VMEM_LIMIT_BYTES = 56 * 1024 * 1024


def _cparams(sem):
    return pltpu.CompilerParams(dimension_semantics=sem, vmem_limit_bytes=VMEM_LIMIT_BYTES)


def _rms(x, g):
    return x * lax.rsqrt(jnp.mean(x * x, axis=-1, keepdims=True) + EPS) * g


def _sigmoid(x):
    return 1.0 / (1.0 + jnp.exp(-x))


def _dot(a, b):
    return jnp.dot(a, b, preferred_element_type=F32)


def _dot_nt(a, b):
    return lax.dot_general(a, b, (((1,), (1,)), ((), ())), preferred_element_type=F32)


def _dot_tn(a, b):
    return lax.dot_general(a, b, (((0,), (0,)), ((), ())), preferred_element_type=F32)


def _split3(v):
    hi = v.astype(BF16)
    r = v - hi.astype(F32)
    mid = r.astype(BF16)
    lo = (r - mid.astype(F32)).astype(BF16)
    return hi, mid, lo


def _dot_exact_lhs01(mat01, v):
    hi, mid, lo = _split3(v)
    return _dot(mat01, hi) + _dot(mat01, mid) + _dot(mat01, lo)


def _inproj_body(x_ref, g_ref, wq_ref, wg_ref, wzs_ref, wgf_ref, wgd_ref,
                 oq_ref, og_ref, oz_ref, osm_ref, ogf_ref, ogd_ref):
    hb = _rms(x_ref[...], g_ref[...]).astype(BF16)
    oq_ref[...] = _dot(hb, wq_ref[...]).astype(BF16)
    og_ref[...] = _dot(hb, wg_ref[...]).astype(BF16)
    zs = _dot(hb, wzs_ref[...])
    oz_ref[...] = zs[:, :GDN_WIDTH].astype(BF16)
    osm_ref[...] = zs[:, GDN_WIDTH:]
    ogf_ref[...] = _dot(hb, wgf_ref[...]).astype(BF16)
    ogd_ref[...] = _dot(hb, wgd_ref[...]).astype(BF16)


def _inproj(x2d, g_mix, wq, wg, wzs, wgf, wgd, tm):
    t, d = x2d.shape
    row = lambda i: (i, 0)
    const = lambda i: (0, 0)
    widths = (3 * FOX_WIDTH, 3 * GDN_WIDTH, GDN_WIDTH, LANES, d, d)
    dtypes = (BF16, BF16, BF16, F32, BF16, BF16)
    return pl.pallas_call(
        _inproj_body,
        grid=(t // tm,),
        in_specs=[pl.BlockSpec((tm, d), row), pl.BlockSpec((1, d), const)]
        + [pl.BlockSpec(w.shape, const) for w in (wq, wg, wzs, wgf, wgd)],
        out_specs=[pl.BlockSpec((tm, n), row) for n in widths],
        out_shape=[jax.ShapeDtypeStruct((t, n), dt) for n, dt in zip(widths, dtypes)],
        compiler_params=_cparams(("parallel",)),
        name="inproj",
    )(x2d, g_mix, wq, wg, wzs, wgf, wgd)


def _gates_body(sm_ref, par_ref, o_ref, carry_ref, *, tp):
    @pl.when(pl.program_id(1) == 0)
    def _():
        carry_ref[...] = jnp.zeros_like(carry_ref)

    sm = sm_ref[0]
    lane = lax.broadcasted_iota(jnp.int32, sm.shape, 1)
    z = sm + par_ref[0:1, :]
    soft = jnp.log1p(jnp.exp(-jnp.abs(z)))
    softplus = jnp.maximum(z, 0.0) + soft
    logf = jnp.minimum(z, 0.0) - soft
    g = -jnp.exp(par_ref[1:2, :]) * softplus
    beta = _sigmoid(sm)
    is_f = lane < FOX_HEADS
    is_g = jnp.logical_and(lane >= FOX_HEADS, lane < FOX_HEADS + GDN_HEADS)
    is_b = jnp.logical_and(lane >= FOX_HEADS + GDN_HEADS, lane < FOX_HEADS + 2 * GDN_HEADS)
    val = jnp.where(is_f, logf, jnp.where(is_g, g, 0.0))
    r = lax.broadcasted_iota(jnp.int32, (tp, tp), 0)
    c = lax.broadcasted_iota(jnp.int32, (tp, tp), 1)
    lower = r >= c
    tri = jnp.where(lower, 1.0, 0.0).astype(BF16)
    same_chunk = (r // GDN_CHUNK) == (c // GDN_CHUNK)
    tri_chunk = jnp.where(jnp.logical_and(lower, same_chunk), 1.0, 0.0).astype(BF16)
    cum = _dot_exact_lhs01(tri, val) + carry_ref[...]
    gcs = _dot_exact_lhs01(tri_chunk, val)
    carry_ref[...] = cum[tp - 1:tp, :]
    o_ref[0] = jnp.where(is_f, cum, jnp.where(is_g, gcs, jnp.where(is_b, beta, 0.0)))


def _gates(small3d, par, tp):
    b, s, _ = small3d.shape
    return pl.pallas_call(
        functools.partial(_gates_body, tp=tp),
        grid=(b, s // tp),
        in_specs=[pl.BlockSpec((1, tp, LANES), lambda bi, i: (bi, i, 0)),
                  pl.BlockSpec((8, LANES), lambda bi, i: (0, 0))],
        out_specs=pl.BlockSpec((1, tp, LANES), lambda bi, i: (bi, i, 0)),
        out_shape=jax.ShapeDtypeStruct((b, s, LANES), F32),
        scratch_shapes=[pltpu.VMEM((1, LANES), F32)],
        compiler_params=_cparams(("parallel", "arbitrary")),
        name="gates",
    )(small3d, par)


def _fox_body(q_ref, k_ref, v_ref, c_ref, o_ref, acc_ref, m_ref, l_ref, *, tq):
    qi = pl.program_id(2)
    lane = lax.broadcasted_iota(jnp.int32, (tq, LANES), 1)
    q = q_ref[0] * jnp.asarray(FOX_HEAD_DIM ** -0.5, BF16)
    zero = jnp.zeros_like(q)
    qh = (jnp.where(lane < FOX_HEAD_DIM, q, zero), jnp.where(lane >= FOX_HEAD_DIM, q, zero))
    m_ref[...] = jnp.full(m_ref.shape, NEG, F32)
    l_ref[...] = jnp.zeros(l_ref.shape, F32)
    acc_ref[...] = jnp.zeros(acc_ref.shape, F32)

    def step(j, masked):
        start = pl.multiple_of(j * tq, tq)
        kb = k_ref[0, pl.ds(start, tq), :]
        vb = v_ref[0, pl.ds(start, tq), :]
        for hh in range(2):
            s = _dot_nt(qh[hh], kb) - c_ref[0, 0, hh:hh + 1, pl.ds(start, tq)]
            if masked:
                row = lax.broadcasted_iota(jnp.int32, (tq, tq), 0)
                col = lax.broadcasted_iota(jnp.int32, (tq, tq), 1)
                s = jnp.where(col <= row, s, NEG)
            m_old = m_ref[hh]
            m_new = jnp.maximum(m_old, jnp.max(s, axis=-1, keepdims=True))
            alpha = jnp.exp(m_old - m_new)
            p = jnp.exp(s - m_new)
            l_ref[hh] = alpha * l_ref[hh] + jnp.sum(p, axis=-1, keepdims=True)
            acc_ref[hh] = alpha * acc_ref[hh] + _dot(p.astype(BF16), vb)
            m_ref[hh] = m_new

    def loop_body(j, carry):
        step(j, False)
        return carry

    lax.fori_loop(0, qi, loop_body, 0)
    step(qi, True)
    out = jnp.where(lane < FOX_HEAD_DIM, acc_ref[0] / l_ref[0], acc_ref[1] / l_ref[1])
    o_ref[0] = out.astype(BF16)


def _fox(zq3d, cumrow, tq):
    b, s, _ = zq3d.shape
    npair = FOX_HEADS // 2
    kblk = FOX_WIDTH // LANES
    return pl.pallas_call(
        functools.partial(_fox_body, tq=tq),
        grid=(b, npair, s // tq),
        in_specs=[pl.BlockSpec((1, tq, LANES), lambda bi, hp, qi: (bi, qi, hp)),
                  pl.BlockSpec((1, s, LANES), lambda bi, hp, qi: (bi, 0, kblk + hp)),
                  pl.BlockSpec((1, s, LANES), lambda bi, hp, qi: (bi, 0, 2 * kblk + hp)),
                  pl.BlockSpec((1, 1, 2, s), lambda bi, hp, qi: (bi, hp, 0, 0))],
        out_specs=pl.BlockSpec((1, tq, LANES), lambda bi, hp, qi: (bi, qi, hp)),
        out_shape=jax.ShapeDtypeStruct((b, s, FOX_WIDTH), BF16),
        scratch_shapes=[pltpu.VMEM((2, tq, LANES), F32), pltpu.VMEM((2, tq, 1), F32),
                        pltpu.VMEM((2, tq, 1), F32)],
        compiler_params=_cparams(("parallel", "parallel", "parallel")),
        name="fox",
    )(zq3d, zq3d, zq3d, cumrow)


HALO = 16


def _gdnprep_body(x_ref, halo_ref, cw_ref, oq_ref, ok_ref, ov_ref, ext_ref, *, tp):
    prev = halo_ref[0].astype(F32)
    ext_ref[0:HALO, :] = jnp.where(pl.program_id(1) > 0, prev, 0.0)
    ext_ref[HALO:, :] = x_ref[0].astype(F32)
    acc = cw_ref[GDN_CONV - 1:GDN_CONV, :] * ext_ref[HALO:HALO + tp, :]
    for j in range(GDN_CONV - 1):
        off = HALO - (GDN_CONV - 1) + j
        acc = acc + cw_ref[j:j + 1, :] * ext_ref[off:off + tp, :]
    y = acc * _sigmoid(acc)

    def l2(v):
        return v * lax.rsqrt(jnp.sum(v * v, axis=-1, keepdims=True) + EPS)

    for h in range(GDN_HEADS):
        lo, hi = h * GDN_HEAD_DIM, (h + 1) * GDN_HEAD_DIM
        oq_ref[0, :, lo:hi] = (l2(y[:, lo:hi]) * GDN_HEAD_DIM ** -0.5).astype(BF16)
        ok_ref[0, :, lo:hi] = l2(y[:, GDN_WIDTH + lo:GDN_WIDTH + hi]).astype(BF16)
    ov_ref[0] = y[:, 2 * GDN_WIDTH:].astype(BF16)


def _gdnprep(zg3d, conv_w, tp):
    b, s, c = zg3d.shape
    blk = lambda bi, i: (bi, i, 0)
    return pl.pallas_call(
        functools.partial(_gdnprep_body, tp=tp),
        grid=(b, s // tp),
        in_specs=[pl.BlockSpec((1, tp, c), blk),
                  pl.BlockSpec((1, HALO, c), lambda bi, i: (bi, jnp.maximum(i * (tp // HALO) - 1, 0), 0)),
                  pl.BlockSpec(conv_w.shape, lambda bi, i: (0, 0))],
        out_specs=[pl.BlockSpec((1, tp, GDN_WIDTH), blk)] * 3,
        out_shape=[jax.ShapeDtypeStruct((b, s, GDN_WIDTH), BF16)] * 3,
        scratch_shapes=[pltpu.VMEM((tp + HALO, c), F32)],
        compiler_params=_cparams(("parallel", "parallel")),
        name="gdnprep",
    )(zg3d, zg3d, conv_w)


def _stack_heads(x):
    return jnp.concatenate([x[:, h * GDN_HEAD_DIM:(h + 1) * GDN_HEAD_DIM] for h in range(GDN_HEADS)], axis=0)


def _gdn_body(q_ref, k_ref, v_ref, col_ref, grow_ref, o_ref, state_ref, *, tg):
    C = GDN_CHUNK
    R = GDN_HEADS * C
    dh = GDN_HEAD_DIM

    @pl.when(pl.program_id(1) == 0)
    def _():
        state_ref[...] = jnp.zeros_like(state_ref)

    r = lax.broadcasted_iota(jnp.int32, (R, R), 0)
    c = lax.broadcasted_iota(jnp.int32, (R, R), 1)
    same_head = (r // C) == (c // C)
    lower = jnp.logical_and(same_head, r >= c)
    strict = jnp.logical_and(same_head, r > c)
    gc_lane0 = FOX_HEADS
    beta_lane0 = FOX_HEADS + GDN_HEADS

    def chunk(ci, carry):
        r0 = pl.multiple_of(ci * C, C)
        qs = _stack_heads(q_ref[0, pl.ds(r0, C), :]).astype(F32)
        ks = _stack_heads(k_ref[0, pl.ds(r0, C), :]).astype(F32)
        vs = _stack_heads(v_ref[0, pl.ds(r0, C), :]).astype(F32)
        col = col_ref[0, pl.ds(r0, C), :]
        gc_col = jnp.concatenate([col[:, gc_lane0 + h:gc_lane0 + h + 1] for h in range(GDN_HEADS)], axis=0)
        beta_col = jnp.concatenate([col[:, beta_lane0 + h:beta_lane0 + h + 1] for h in range(GDN_HEADS)], axis=0)
        gl_col = jnp.concatenate(
            [jnp.broadcast_to(col[C - 1:C, gc_lane0 + h:gc_lane0 + h + 1], (C, 1)) for h in range(GDN_HEADS)], axis=0)
        gc_row = grow_ref[0, pl.ds(ci, 1), 0, :]
        decay = jnp.exp(jnp.where(lower, gc_col - gc_row, NEG))
        ksb = ks.astype(BF16)
        kk = _dot_nt(ksb, ksb)
        qk = _dot_nt(qs.astype(BF16), ksb)
        lmat = jnp.where(strict, kk * decay * beta_col, 0.0)
        intra = (qk * decay).astype(BF16)
        n_mat = -lmat
        pw = lmat
        for _ in range(5):
            pwb = pw.astype(BF16)
            pw = _dot(pwb, pwb)
            n_mat = n_mat + pw + _dot(n_mat.astype(BF16), pw.astype(BF16))
        e_gc = jnp.exp(gc_col)
        rhs = jnp.concatenate([vs * beta_col, ks * (beta_col * e_gc)], axis=1)
        sol = rhs + _dot(n_mat.astype(BF16), rhs.astype(BF16))
        u = sol[:, :dh]
        w = sol[:, dh:].astype(BF16)
        q_dec = (qs * e_gc).astype(BF16)
        k_dec = (ks * jnp.exp(gl_col - gc_col)).astype(BF16)
        g_last = jnp.exp(gl_col)
        v_new = []
        o_state = []
        for h in range(GDN_HEADS):
            sl = slice(h * C, (h + 1) * C)
            st = state_ref[h].astype(BF16)
            v_new.append(u[sl] - _dot(w[sl], st))
            o_state.append(_dot(q_dec[sl], st))
        v_new = jnp.concatenate(v_new, axis=0)
        v_new_b = v_new.astype(BF16)
        o_all = jnp.concatenate(o_state, axis=0) + _dot(intra, v_new_b)
        for h in range(GDN_HEADS):
            sl = slice(h * C, (h + 1) * C)
            state_ref[h] = state_ref[h] * g_last[h * C:h * C + 1, :] + _dot_tn(k_dec[sl], v_new_b[sl])
        o_ref[0, pl.ds(r0, C), :] = jnp.concatenate(
            [o_all[h * C:(h + 1) * C] for h in range(GDN_HEADS)], axis=1).astype(BF16)
        return carry

    lax.fori_loop(0, tg // C, chunk, 0)


def _gdn(qn, kn, vv, col3d, gcrow, tg):
    b, s, _ = qn.shape
    blk = lambda bi, i: (bi, i, 0)
    nck = tg // GDN_CHUNK
    return pl.pallas_call(
        functools.partial(_gdn_body, tg=tg),
        grid=(b, s // tg),
        in_specs=[pl.BlockSpec((1, tg, GDN_WIDTH), blk)] * 3
        + [pl.BlockSpec((1, tg, LANES), blk),
           pl.BlockSpec((1, nck, 1, GDN_HEADS * GDN_CHUNK), lambda bi, i: (bi, i, 0, 0))],
        out_specs=pl.BlockSpec((1, tg, GDN_WIDTH), blk),
        out_shape=jax.ShapeDtypeStruct((b, s, GDN_WIDTH), BF16),
        scratch_shapes=[pltpu.VMEM((GDN_HEADS, GDN_HEAD_DIM, GDN_HEAD_DIM), F32)],
        compiler_params=_cparams(("parallel", "arbitrary")),
        name="gdn",
    )(qn, kn, vv, col3d, gcrow)


def _combine_body(yf_ref, og_ref, dz_ref, gf_ref, gd_ref, x_ref, wof_ref, wod_ref, wout_ref,
                  gon_ref, gffn_ref, wrh_ref, wrl_ref, br_ref, x1_ref, t_ref, gate_ref):
    on = []
    for h in range(GDN_HEADS):
        sl = slice(h * GDN_HEAD_DIM, (h + 1) * GDN_HEAD_DIM)
        dz = dz_ref[:, sl].astype(F32)
        on.append(_rms(og_ref[:, sl].astype(F32), gon_ref[...]) * (dz * _sigmoid(dz)))
    on = jnp.concatenate(on, axis=1).astype(BF16)
    y_fox = _dot(yf_ref[...], wof_ref[...])
    y_delta = _dot(on, wod_ref[...])
    merged = _sigmoid(gf_ref[...].astype(F32)) * y_fox + _sigmoid(gd_ref[...].astype(F32)) * y_delta
    x1 = x_ref[...] + _dot(merged.astype(BF16), wout_ref[...])
    x1_ref[...] = x1
    t32 = _rms(x1, gffn_ref[...])
    th = t32.astype(BF16)
    tl = (t32 - th.astype(F32)).astype(BF16)
    t_ref[...] = th
    logits = _dot(th, wrh_ref[...]) + _dot(tl, wrh_ref[...]) + _dot(th, wrl_ref[...]) + br_ref[...]
    lane = lax.broadcasted_iota(jnp.int32, logits.shape, 1)
    gl = jnp.where(lane < N_GROUPS, logits, NEG)
    gmax = jnp.max(gl, axis=-1, keepdims=True)
    g_sel = jnp.min(jnp.where(gl == gmax, lane, LANES), axis=-1, keepdims=True)
    p_sel = 1.0 / jnp.sum(jnp.exp(gl - gmax), axis=-1, keepdims=True)
    lo = EXPERT_LANE0 + EXPERTS_PER_GROUP * g_sel
    in_grp = jnp.logical_and(lane >= lo, lane < lo + EXPERTS_PER_GROUP)
    el = jnp.where(in_grp, logits, NEG)
    emax = jnp.max(el, axis=-1, keepdims=True)
    ee = jnp.where(in_grp, jnp.exp(el - emax), 0.0)
    pe = ee / jnp.sum(ee, axis=-1, keepdims=True)
    pe = jnp.where(in_grp, pe, -1.0)
    p1 = jnp.max(pe, axis=-1, keepdims=True)
    i1 = jnp.min(jnp.where(pe == p1, lane, LANES), axis=-1, keepdims=True)
    pe2 = jnp.where(lane == i1, -1.0, pe)
    p2 = jnp.max(pe2, axis=-1, keepdims=True)
    i2 = jnp.min(jnp.where(pe2 == p2, lane, LANES), axis=-1, keepdims=True)
    den = p1 + p2
    gate_ref[...] = jnp.where(lane == i1, p_sel * (p1 / den), jnp.where(lane == i2, p_sel * (p2 / den), 0.0))


def _combine(yf, og, dz, gf, gd, x2d, wof, wod, wout, g_on, g_ffn, wrh, wrl, br, tc):
    t, d = x2d.shape
    row = lambda i: (i, 0)
    const = lambda i: (0, 0)
    acts = (yf, og, dz, gf, gd, x2d)
    consts = (wof, wod, wout, g_on, g_ffn, wrh, wrl, br)
    return pl.pallas_call(
        _combine_body,
        grid=(t // tc,),
        in_specs=[pl.BlockSpec((tc, a.shape[1]), row) for a in acts]
        + [pl.BlockSpec(c.shape, const) for c in consts],
        out_specs=[pl.BlockSpec((tc, d), row), pl.BlockSpec((tc, d), row), pl.BlockSpec((tc, LANES), row)],
        out_shape=[jax.ShapeDtypeStruct((t, d), F32), jax.ShapeDtypeStruct((t, d), BF16),
                   jax.ShapeDtypeStruct((t, LANES), F32)],
        compiler_params=_cparams(("parallel",)),
        name="combine",
    )(*acts, *consts)


def _moe_body(t_ref, gate_ref, x1_ref, wgu_ref, wd_ref, o_ref):
    e = pl.program_id(1)

    @pl.when(e == 0)
    def _():
        o_ref[...] = x1_ref[...]

    hgu = _dot(t_ref[...], wgu_ref[0])
    a = hgu[:, :EXPERT_FF]
    hid = a * _sigmoid(a) * hgu[:, EXPERT_FF:]
    gate = gate_ref[...]
    lane = lax.broadcasted_iota(jnp.int32, gate.shape, 1)
    gcol = jnp.sum(jnp.where(lane == e + EXPERT_LANE0, gate, 0.0), axis=-1, keepdims=True)
    o_ref[...] += _dot((hid * gcol).astype(BF16), wd_ref[0])


def _moe(t_bf, gate, x1, wgu, wd, tm):
    t, d = x1.shape
    row = lambda i, e: (i, 0)
    return pl.pallas_call(
        _moe_body,
        grid=(t // tm, N_EXPERTS),
        in_specs=[pl.BlockSpec((tm, d), row), pl.BlockSpec((tm, LANES), row), pl.BlockSpec((tm, d), row),
                  pl.BlockSpec((1, d, 2 * EXPERT_FF), lambda i, e: (e, 0, 0)),
                  pl.BlockSpec((1, EXPERT_FF, d), lambda i, e: (e, 0, 0))],
        out_specs=pl.BlockSpec((tm, d), row),
        out_shape=jax.ShapeDtypeStruct((t, d), F32),
        compiler_params=_cparams(("parallel", "arbitrary")),
        name="moe",
    )(t_bf, gate, x1, wgu, wd)


def _final_body(x_ref, p_ref, gple_ref, wpg_ref, wpp_ref, gfin_ref, o_ref):
    x = x_ref[...]
    r = _rms(x, gple_ref[...]).astype(BF16)
    ple_gate = _sigmoid(_dot(r, wpg_ref[...]))
    proj = _dot(p_ref[...].astype(BF16), wpp_ref[...])
    o_ref[...] = _rms(x + ple_gate * proj, gfin_ref[...])


def _final(x2, p2d, g_ple, wpg, wpp, g_final, tf):
    t, d = x2.shape
    row = lambda i: (i, 0)
    const = lambda i: (0, 0)
    return pl.pallas_call(
        _final_body,
        grid=(t // tf,),
        in_specs=[pl.BlockSpec((tf, d), row), pl.BlockSpec((tf, p2d.shape[1]), row),
                  pl.BlockSpec((1, d), const), pl.BlockSpec(wpg.shape, const),
                  pl.BlockSpec(wpp.shape, const), pl.BlockSpec((1, d), const)],
        out_specs=pl.BlockSpec((tf, d), row),
        out_shape=jax.ShapeDtypeStruct((t, d), F32),
        compiler_params=_cparams(("parallel",)),
        name="final",
    )(x2, p2d, g_ple, wpg, wpp, g_final)


def _pad_lanes(v, lane0):
    return jnp.zeros((1, LANES), F32).at[0, lane0:lane0 + v.shape[0]].set(v.astype(F32))


def _tile(n, pref):
    return pref if n % pref == 0 else n


def _layer(x, p_l, g_mix, w_in, b_forget, conv_w, a_log, dt_bias, g_onorm, w_o_fox, w_o_delta, w_out,
           g_ffn, w_group, b_group, w_router, b_router, w_gate, w_up, w_down, g_ple, w_ple_gate, w_ple_proj,
           g_post, apply_post):
    b, s, d = x.shape
    t = b * s
    x2d = x.reshape(t, d)

    o_ff = 3 * FOX_WIDTH
    o_qkv = o_ff + FOX_HEADS
    o_da = o_qkv + 3 * GDN_WIDTH
    o_db = o_da + GDN_HEADS
    o_dz = o_db + GDN_HEADS
    o_gf = o_dz + GDN_WIDTH
    o_gd = o_gf + d
    w_small = jnp.concatenate(
        [w_in[:, o_ff:o_qkv], w_in[:, o_da:o_db], w_in[:, o_db:o_dz],
         jnp.zeros((d, LANES - FOX_HEADS - 2 * GDN_HEADS), w_in.dtype)], axis=1)
    wq = w_in[:, :o_ff].astype(BF16)
    wg = w_in[:, o_qkv:o_da].astype(BF16)
    wzs = jnp.concatenate([w_in[:, o_dz:o_gf], w_small], axis=1).astype(BF16)
    wgf = w_in[:, o_gf:o_gd].astype(BF16)
    wgd = w_in[:, o_gd:].astype(BF16)

    zq, zg, dz, small, gf, gd = _inproj(x2d, g_mix.reshape(1, d), wq, wg, wzs, wgf, wgd, _tile(t, 512))

    par = jnp.concatenate(
        [_pad_lanes(b_forget, 0) + _pad_lanes(dt_bias, FOX_HEADS), _pad_lanes(a_log, FOX_HEADS),
         jnp.zeros((6, LANES), F32)], axis=0)
    col = _gates(small.reshape(b, s, LANES), par, _tile(s, 256))

    cumrow = jnp.transpose(col[:, :, :FOX_HEADS], (0, 2, 1)).reshape(b, FOX_HEADS // 2, 2, s)
    nc = s // GDN_CHUNK
    gc = col[:, :, FOX_HEADS:FOX_HEADS + GDN_HEADS].reshape(b, nc, GDN_CHUNK, GDN_HEADS)
    gcrow = jnp.transpose(gc, (0, 1, 3, 2)).reshape(b, nc, 1, GDN_HEADS * GDN_CHUNK)

    y_fox = _fox(zq.reshape(b, s, 3 * FOX_WIDTH), cumrow, _tile(s, 512))

    qn, kn, vv = _gdnprep(zg.reshape(b, s, 3 * GDN_WIDTH), conv_w.astype(F32), _tile(s, 512))
    o_gdn = _gdn(qn, kn, vv, col, gcrow, _tile(s, 512))

    w_r = jnp.concatenate([w_group, w_router,
                           jnp.zeros((d, LANES - N_GROUPS - N_EXPERTS), w_group.dtype)], axis=1).astype(F32)
    wrh = w_r.astype(BF16)
    wrl = (w_r - wrh.astype(F32)).astype(BF16)
    br = _pad_lanes(b_group, 0) + _pad_lanes(b_router, EXPERT_LANE0)
    x1, t_bf, gate = _combine(
        y_fox.reshape(t, FOX_WIDTH), o_gdn.reshape(t, GDN_WIDTH), dz, gf, gd, x2d,
        w_o_fox.astype(BF16), w_o_delta.astype(BF16), w_out.astype(BF16),
        g_onorm.reshape(1, GDN_HEAD_DIM).astype(F32), g_ffn.reshape(1, d).astype(F32), wrh, wrl, br,
        _tile(t, 512))

    wgu = jnp.concatenate([w_gate, w_up], axis=2).astype(BF16)
    x2 = _moe(t_bf, gate, x1, wgu, w_down.astype(BF16), _tile(t, 1024))

    out = _final(x2, p_l.reshape(t, -1), g_ple.reshape(1, d).astype(F32), w_ple_gate.astype(BF16),
                 w_ple_proj.astype(BF16), g_post.reshape(1, d).astype(F32), _tile(t, 512))
    return out.reshape(b, s, d)


def kernel(x, p, g_mix, w_in, b_forget, conv_w, a_log, dt_bias, g_onorm, w_o_fox, w_o_delta, w_out,
           g_ffn, w_group, b_group, w_router, b_router, w_gate, w_up, w_down, g_ple, w_ple_gate, w_ple_proj,
           g_final):
    depth = p.shape[0]
    assert depth == 1, "the final rmsnorm is fused into the last layer's epilogue; depth 1 only"
    i = 0
    return _layer(x, p[i], g_mix[i], w_in[i], b_forget[i], conv_w[i], a_log[i], dt_bias[i], g_onorm[i],
                  w_o_fox[i], w_o_delta[i], w_out[i], g_ffn[i], w_group[i], b_group[i], w_router[i],
                  b_router[i], w_gate[i], w_up[i], w_down[i], g_ple[i], w_ple_gate[i], w_ple_proj[i],
                  g_final, True)
```

```python
import functools

import jax
import jax.numpy as jnp
from jax import lax
from jax.experimental import pallas as pl
from jax.experimental.pallas import tpu as pltpu

F32 = jnp.float32
BF16 = jnp.bfloat16
EPS = 1e-6
NEG = -1e30

FOX_HEADS = 8
FOX_HEAD_DIM = 64
FOX_WIDTH = FOX_HEADS * FOX_HEAD_DIM
GDN_HEADS = 4
GDN_HEAD_DIM = 128
GDN_CONV = 4
GDN_CHUNK = 64
GDN_WIDTH = GDN_HEADS * GDN_HEAD_DIM
N_GROUPS = 4
EXPERTS_PER_GROUP = 8
N_EXPERTS = N_GROUPS * EXPERTS_PER_GROUP
EXPERT_FF = 256
LANES = 128
EXPERT_LANE0 = N_GROUPS

VMEM_LIMIT_BYTES = 56 * 1024 * 1024


def _cparams(sem):
    return pltpu.CompilerParams(dimension_semantics=sem, vmem_limit_bytes=VMEM_LIMIT_BYTES)


def _rms(x, g):
    return x * lax.rsqrt(jnp.mean(x * x, axis=-1, keepdims=True) + EPS) * g


def _sigmoid(x):
    return 1.0 / (1.0 + jnp.exp(-x))


def _dot(a, b):
    return jnp.dot(a, b, preferred_element_type=F32)


def _dot_nt(a, b):
    return lax.dot_general(a, b, (((1,), (1,)), ((), ())), preferred_element_type=F32)


def _dot_tn(a, b):
    return lax.dot_general(a, b, (((0,), (0,)), ((), ())), preferred_element_type=F32)


def _split3(v):
    hi = v.astype(BF16)
    r = v - hi.astype(F32)
    mid = r.astype(BF16)
    lo = (r - mid.astype(F32)).astype(BF16)
    return hi, mid, lo


def _dot_exact_lhs01(mat01, v):
    hi, mid, lo = _split3(v)
    return _dot(mat01, hi) + _dot(mat01, mid) + _dot(mat01, lo)


def _inproj_body(x_ref, g_ref, wq_ref, wg_ref, wzs_ref, wgf_ref, wgd_ref,
                 oq_ref, og_ref, oz_ref, osm_ref, ogf_ref, ogd_ref):
    hb = _rms(x_ref[...], g_ref[...]).astype(BF16)
    oq_ref[...] = _dot(hb, wq_ref[...]).astype(BF16)
    og_ref[...] = _dot(hb, wg_ref[...]).astype(BF16)
    zs = _dot(hb, wzs_ref[...])
    oz_ref[...] = zs[:, :GDN_WIDTH].astype(BF16)
    osm_ref[...] = zs[:, GDN_WIDTH:]
    ogf_ref[...] = _dot(hb, wgf_ref[...]).astype(BF16)
    ogd_ref[...] = _dot(hb, wgd_ref[...]).astype(BF16)


def _inproj(x2d, g_mix, wq, wg, wzs, wgf, wgd, tm):
    t, d = x2d.shape
    row = lambda i: (i, 0)
    const = lambda i: (0, 0)
    widths = (3 * FOX_WIDTH, 3 * GDN_WIDTH, GDN_WIDTH, LANES, d, d)
    dtypes = (BF16, BF16, BF16, F32, BF16, BF16)
    return pl.pallas_call(
        _inproj_body,
        grid=(t // tm,),
        in_specs=[pl.BlockSpec((tm, d), row), pl.BlockSpec((1, d), const)]
        + [pl.BlockSpec(w.shape, const) for w in (wq, wg, wzs, wgf, wgd)],
        out_specs=[pl.BlockSpec((tm, n), row) for n in widths],
        out_shape=[jax.ShapeDtypeStruct((t, n), dt) for n, dt in zip(widths, dtypes)],
        compiler_params=_cparams(("parallel",)),
        name="inproj",
    )(x2d, g_mix, wq, wg, wzs, wgf, wgd)


def _gates_body(sm_ref, par_ref, o_ref, carry_ref, *, tp):
    @pl.when(pl.program_id(1) == 0)
    def _():
        carry_ref[...] = jnp.zeros_like(carry_ref)

    sm = sm_ref[0]
    lane = lax.broadcasted_iota(jnp.int32, sm.shape, 1)
    z = sm + par_ref[0:1, :]
    soft = jnp.log1p(jnp.exp(-jnp.abs(z)))
    softplus = jnp.maximum(z, 0.0) + soft
    logf = jnp.minimum(z, 0.0) - soft
    g = -jnp.exp(par_ref[1:2, :]) * softplus
    beta = _sigmoid(sm)
    is_f = lane < FOX_HEADS
    is_g = jnp.logical_and(lane >= FOX_HEADS, lane < FOX_HEADS + GDN_HEADS)
    is_b = jnp.logical_and(lane >= FOX_HEADS + GDN_HEADS, lane < FOX_HEADS + 2 * GDN_HEADS)
    val = jnp.where(is_f, logf, jnp.where(is_g, g, 0.0))
    r = lax.broadcasted_iota(jnp.int32, (tp, tp), 0)
    c = lax.broadcasted_iota(jnp.int32, (tp, tp), 1)
    lower = r >= c
    tri = jnp.where(lower, 1.0, 0.0).astype(BF16)
    same_chunk = (r // GDN_CHUNK) == (c // GDN_CHUNK)
    tri_chunk = jnp.where(jnp.logical_and(lower, same_chunk), 1.0, 0.0).astype(BF16)
    cum = _dot_exact_lhs01(tri, val) + carry_ref[...]
    gcs = _dot_exact_lhs01(tri_chunk, val)
    carry_ref[...] = cum[tp - 1:tp, :]
    o_ref[0] = jnp.where(is_f, cum, jnp.where(is_g, gcs, jnp.where(is_b, beta, 0.0)))


def _gates(small3d, par, tp):
    b, s, _ = small3d.shape
    return pl.pallas_call(
        functools.partial(_gates_body, tp=tp),
        grid=(b, s // tp),
        in_specs=[pl.BlockSpec((1, tp, LANES), lambda bi, i: (bi, i, 0)),
                  pl.BlockSpec((8, LANES), lambda bi, i: (0, 0))],
        out_specs=pl.BlockSpec((1, tp, LANES), lambda bi, i: (bi, i, 0)),
        out_shape=jax.ShapeDtypeStruct((b, s, LANES), F32),
        scratch_shapes=[pltpu.VMEM((1, LANES), F32)],
        compiler_params=_cparams(("parallel", "arbitrary")),
        name="gates",
    )(small3d, par)


LOG2E = 1.4426950408889634


def _fox_body(q_ref, k_ref, v_ref, col_ref, o_ref, vt_ref, ckb_ref, acc_ref, m_ref, l_ref, *, tq, s_len):
    hp = pl.program_id(1)
    qi = pl.program_id(2)
    hd = FOX_HEAD_DIM

    @pl.when(qi == 0)
    def _():
        def prep(ci, carry):
            r0 = pl.multiple_of(ci * tq, tq)
            vt_ref[:, pl.ds(r0, tq)] = v_ref[0, pl.ds(r0, tq), :].astype(F32).T.astype(BF16)
            col = col_ref[0, pl.ds(r0, tq), :]
            lane = lax.broadcasted_iota(jnp.int32, col.shape, 1)
            for hh in range(2):
                ck = jnp.sum(jnp.where(lane == 2 * hp + hh, col, 0.0), axis=-1, keepdims=True)
                ckb_ref[hh, pl.ds(r0, tq), :] = jnp.broadcast_to(ck * LOG2E, (tq, LANES))
            return carry

        lax.fori_loop(0, s_len // tq, prep, 0)

    lane = lax.broadcasted_iota(jnp.int32, (tq, LANES), 1)
    q = (q_ref[0].astype(F32) * (hd ** -0.5 * LOG2E)).astype(BF16)
    zero = jnp.zeros_like(q)
    qh = (jnp.where(lane < hd, q, zero), jnp.where(lane >= hd, q, zero))
    m_ref[...] = jnp.full(m_ref.shape, NEG, F32)
    l_ref[...] = jnp.zeros(l_ref.shape, F32)
    acc_ref[...] = jnp.zeros(acc_ref.shape, F32)

    def step(j, masked):
        start = pl.multiple_of(j * tq, tq)
        kb = k_ref[0, pl.ds(start, tq), :]
        for hh in range(2):
            bias = ckb_ref[hh, pl.ds(start, tq), :]
            s = _dot_nt(kb, qh[hh]) - jnp.concatenate([bias] * (tq // LANES), axis=1)
            if masked:
                key = lax.broadcasted_iota(jnp.int32, (tq, tq), 0)
                qry = lax.broadcasted_iota(jnp.int32, (tq, tq), 1)
                s = jnp.where(key <= qry, s, NEG)
            m_old = m_ref[hh:hh + 1, :]
            m_new = jnp.maximum(m_old, jnp.max(s, axis=0, keepdims=True))
            alpha = jnp.exp2(m_old - m_new)
            p = jnp.exp2(s - m_new)
            l_ref[hh:hh + 1, :] = alpha * l_ref[hh:hh + 1, :] + jnp.sum(p, axis=0, keepdims=True)
            vt = vt_ref[hh * hd:(hh + 1) * hd, pl.ds(start, tq)]
            acc_ref[hh] = alpha * acc_ref[hh] + _dot(vt, p.astype(BF16))
            m_ref[hh:hh + 1, :] = m_new

    def loop_body(j, carry):
        step(j, False)
        return carry

    lax.fori_loop(0, qi, loop_body, 0)
    step(qi, True)
    out_t = jnp.concatenate([acc_ref[hh] / l_ref[hh:hh + 1, :] for hh in range(2)], axis=0)
    o_ref[0] = out_t.T.astype(BF16)


def _fox(zq3d, col3d, tq):
    b, s, _ = zq3d.shape
    npair = FOX_HEADS // 2
    kblk = FOX_WIDTH // LANES
    return pl.pallas_call(
        functools.partial(_fox_body, tq=tq, s_len=s),
        grid=(b, npair, s // tq),
        in_specs=[pl.BlockSpec((1, tq, LANES), lambda bi, hp, qi: (bi, qi, hp)),
                  pl.BlockSpec((1, s, LANES), lambda bi, hp, qi: (bi, 0, kblk + hp)),
                  pl.BlockSpec((1, s, LANES), lambda bi, hp, qi: (bi, 0, 2 * kblk + hp)),
                  pl.BlockSpec((1, s, LANES), lambda bi, hp, qi: (bi, 0, 0))],
        out_specs=pl.BlockSpec((1, tq, LANES), lambda bi, hp, qi: (bi, qi, hp)),
        out_shape=jax.ShapeDtypeStruct((b, s, FOX_WIDTH), BF16),
        scratch_shapes=[pltpu.VMEM((LANES, s), BF16), pltpu.VMEM((2, s, LANES), F32),
                        pltpu.VMEM((2, FOX_HEAD_DIM, tq), F32), pltpu.VMEM((2, tq), F32),
                        pltpu.VMEM((2, tq), F32)],
        compiler_params=_cparams(("parallel", "parallel", "arbitrary")),
        name="fox",
    )(zq3d, zq3d, zq3d, col3d)


HALO = 16


def _gdnprep_body(x_ref, halo_ref, cw_ref, oq_ref, ok_ref, ov_ref, ext_ref, *, tp):
    prev = halo_ref[0].astype(F32)
    ext_ref[0:HALO, :] = jnp.where(pl.program_id(1) > 0, prev, 0.0)
    ext_ref[HALO:, :] = x_ref[0].astype(F32)
    acc = cw_ref[GDN_CONV - 1:GDN_CONV, :] * ext_ref[HALO:HALO + tp, :]
    for j in range(GDN_CONV - 1):
        off = HALO - (GDN_CONV - 1) + j
        acc = acc + cw_ref[j:j + 1, :] * ext_ref[off:off + tp, :]
    y = acc * _sigmoid(acc)

    def l2(v):
        return v * lax.rsqrt(jnp.sum(v * v, axis=-1, keepdims=True) + EPS)

    for h in range(GDN_HEADS):
        lo, hi = h * GDN_HEAD_DIM, (h + 1) * GDN_HEAD_DIM
        oq_ref[0, :, lo:hi] = (l2(y[:, lo:hi]) * GDN_HEAD_DIM ** -0.5).astype(BF16)
        ok_ref[0, :, lo:hi] = l2(y[:, GDN_WIDTH + lo:GDN_WIDTH + hi]).astype(BF16)
    ov_ref[0] = y[:, 2 * GDN_WIDTH:].astype(BF16)


def _gdnprep(zg3d, conv_w, tp):
    b, s, c = zg3d.shape
    blk = lambda bi, i: (bi, i, 0)
    return pl.pallas_call(
        functools.partial(_gdnprep_body, tp=tp),
        grid=(b, s // tp),
        in_specs=[pl.BlockSpec((1, tp, c), blk),
                  pl.BlockSpec((1, HALO, c), lambda bi, i: (bi, jnp.maximum(i * (tp // HALO) - 1, 0), 0)),
                  pl.BlockSpec(conv_w.shape, lambda bi, i: (0, 0))],
        out_specs=[pl.BlockSpec((1, tp, GDN_WIDTH), blk)] * 3,
        out_shape=[jax.ShapeDtypeStruct((b, s, GDN_WIDTH), BF16)] * 3,
        scratch_shapes=[pltpu.VMEM((tp + HALO, c), F32)],
        compiler_params=_cparams(("parallel", "parallel")),
        name="gdnprep",
    )(zg3d, zg3d, conv_w)


def _stack_heads(x):
    return jnp.concatenate([x[:, h * GDN_HEAD_DIM:(h + 1) * GDN_HEAD_DIM] for h in range(GDN_HEADS)], axis=0)


def _gdn_body(q_ref, k_ref, v_ref, col_ref, grow_ref, o_ref, state_ref, *, tg):
    C = GDN_CHUNK
    R = GDN_HEADS * C
    dh = GDN_HEAD_DIM

    @pl.when(pl.program_id(1) == 0)
    def _():
        state_ref[...] = jnp.zeros_like(state_ref)

    r = lax.broadcasted_iota(jnp.int32, (R, R), 0)
    c = lax.broadcasted_iota(jnp.int32, (R, R), 1)
    same_head = (r // C) == (c // C)
    lower = jnp.logical_and(same_head, r >= c)
    strict = jnp.logical_and(same_head, r > c)
    gc_lane0 = FOX_HEADS
    beta_lane0 = FOX_HEADS + GDN_HEADS

    def chunk(ci, carry):
        r0 = pl.multiple_of(ci * C, C)
        qs = _stack_heads(q_ref[0, pl.ds(r0, C), :]).astype(F32)
        ks = _stack_heads(k_ref[0, pl.ds(r0, C), :]).astype(F32)
        vs = _stack_heads(v_ref[0, pl.ds(r0, C), :]).astype(F32)
        col = col_ref[0, pl.ds(r0, C), :]
        gc_col = jnp.concatenate([col[:, gc_lane0 + h:gc_lane0 + h + 1] for h in range(GDN_HEADS)], axis=0)
        beta_col = jnp.concatenate([col[:, beta_lane0 + h:beta_lane0 + h + 1] for h in range(GDN_HEADS)], axis=0)
        gl_col = jnp.concatenate(
            [jnp.broadcast_to(col[C - 1:C, gc_lane0 + h:gc_lane0 + h + 1], (C, 1)) for h in range(GDN_HEADS)], axis=0)
        gc_row = grow_ref[0, pl.ds(ci, 1), 0, :]
        decay = jnp.exp(jnp.where(lower, gc_col - gc_row, NEG))
        ksb = ks.astype(BF16)
        kk = _dot_nt(ksb, ksb)
        qk = _dot_nt(qs.astype(BF16), ksb)
        lmat = jnp.where(strict, kk * decay * beta_col, 0.0)
        intra = (qk * decay).astype(BF16)
        n_mat = -lmat
        pw = lmat
        for _ in range(5):
            pwb = pw.astype(BF16)
            pw = _dot(pwb, pwb)
            n_mat = n_mat + pw + _dot(n_mat.astype(BF16), pw.astype(BF16))
        e_gc = jnp.exp(gc_col)
        rhs = jnp.concatenate([vs * beta_col, ks * (beta_col * e_gc)], axis=1)
        sol = rhs + _dot(n_mat.astype(BF16), rhs.astype(BF16))
        u = sol[:, :dh]
        w = sol[:, dh:].astype(BF16)
        q_dec = (qs * e_gc).astype(BF16)
        k_dec = (ks * jnp.exp(gl_col - gc_col)).astype(BF16)
        g_last = jnp.exp(gl_col)
        v_new = []
        o_state = []
        for h in range(GDN_HEADS):
            sl = slice(h * C, (h + 1) * C)
            st = state_ref[h].astype(BF16)
            v_new.append(u[sl] - _dot(w[sl], st))
            o_state.append(_dot(q_dec[sl], st))
        v_new = jnp.concatenate(v_new, axis=0)
        v_new_b = v_new.astype(BF16)
        o_all = jnp.concatenate(o_state, axis=0) + _dot(intra, v_new_b)
        for h in range(GDN_HEADS):
            sl = slice(h * C, (h + 1) * C)
            state_ref[h] = state_ref[h] * g_last[h * C:h * C + 1, :] + _dot_tn(k_dec[sl], v_new_b[sl])
        o_ref[0, pl.ds(r0, C), :] = jnp.concatenate(
            [o_all[h * C:(h + 1) * C] for h in range(GDN_HEADS)], axis=1).astype(BF16)
        return carry

    lax.fori_loop(0, tg // C, chunk, 0)


def _gdn(qn, kn, vv, col3d, gcrow, tg):
    b, s, _ = qn.shape
    blk = lambda bi, i: (bi, i, 0)
    nck = tg // GDN_CHUNK
    return pl.pallas_call(
        functools.partial(_gdn_body, tg=tg),
        grid=(b, s // tg),
        in_specs=[pl.BlockSpec((1, tg, GDN_WIDTH), blk)] * 3
        + [pl.BlockSpec((1, tg, LANES), blk),
           pl.BlockSpec((1, nck, 1, GDN_HEADS * GDN_CHUNK), lambda bi, i: (bi, i, 0, 0))],
        out_specs=pl.BlockSpec((1, tg, GDN_WIDTH), blk),
        out_shape=jax.ShapeDtypeStruct((b, s, GDN_WIDTH), BF16),
        scratch_shapes=[pltpu.VMEM((GDN_HEADS, GDN_HEAD_DIM, GDN_HEAD_DIM), F32)],
        compiler_params=_cparams(("parallel", "arbitrary")),
        name="gdn",
    )(qn, kn, vv, col3d, gcrow)


def _combine_body(yf_ref, og_ref, dz_ref, gf_ref, gd_ref, x_ref, wof_ref, wod_ref, wout_ref,
                  gon_ref, gffn_ref, wrh_ref, wrl_ref, br_ref, x1_ref, t_ref, gate_ref):
    on = []
    for h in range(GDN_HEADS):
        sl = slice(h * GDN_HEAD_DIM, (h + 1) * GDN_HEAD_DIM)
        dz = dz_ref[:, sl].astype(F32)
        on.append(_rms(og_ref[:, sl].astype(F32), gon_ref[...]) * (dz * _sigmoid(dz)))
    on = jnp.concatenate(on, axis=1).astype(BF16)
    y_fox = _dot(yf_ref[...], wof_ref[...])
    y_delta = _dot(on, wod_ref[...])
    merged = _sigmoid(gf_ref[...].astype(F32)) * y_fox + _sigmoid(gd_ref[...].astype(F32)) * y_delta
    x1 = x_ref[...] + _dot(merged.astype(BF16), wout_ref[...])
    x1_ref[...] = x1
    t32 = _rms(x1, gffn_ref[...])
    th = t32.astype(BF16)
    tl = (t32 - th.astype(F32)).astype(BF16)
    t_ref[...] = th
    logits = _dot(th, wrh_ref[...]) + _dot(tl, wrh_ref[...]) + _dot(th, wrl_ref[...]) + br_ref[...]
    lane = lax.broadcasted_iota(jnp.int32, logits.shape, 1)
    gl = jnp.where(lane < N_GROUPS, logits, NEG)
    gmax = jnp.max(gl, axis=-1, keepdims=True)
    g_sel = jnp.min(jnp.where(gl == gmax, lane, LANES), axis=-1, keepdims=True)
    p_sel = 1.0 / jnp.sum(jnp.exp(gl - gmax), axis=-1, keepdims=True)
    lo = EXPERT_LANE0 + EXPERTS_PER_GROUP * g_sel
    in_grp = jnp.logical_and(lane >= lo, lane < lo + EXPERTS_PER_GROUP)
    el = jnp.where(in_grp, logits, NEG)
    emax = jnp.max(el, axis=-1, keepdims=True)
    ee = jnp.where(in_grp, jnp.exp(el - emax), 0.0)
    pe = ee / jnp.sum(ee, axis=-1, keepdims=True)
    pe = jnp.where(in_grp, pe, -1.0)
    p1 = jnp.max(pe, axis=-1, keepdims=True)
    i1 = jnp.min(jnp.where(pe == p1, lane, LANES), axis=-1, keepdims=True)
    pe2 = jnp.where(lane == i1, -1.0, pe)
    p2 = jnp.max(pe2, axis=-1, keepdims=True)
    i2 = jnp.min(jnp.where(pe2 == p2, lane, LANES), axis=-1, keepdims=True)
    den = p1 + p2
    gate_ref[...] = jnp.where(lane == i1, p_sel * (p1 / den), jnp.where(lane == i2, p_sel * (p2 / den), 0.0))


def _combine(yf, og, dz, gf, gd, x2d, wof, wod, wout, g_on, g_ffn, wrh, wrl, br, tc):
    t, d = x2d.shape
    row = lambda i: (i, 0)
    const = lambda i: (0, 0)
    acts = (yf, og, dz, gf, gd, x2d)
    consts = (wof, wod, wout, g_on, g_ffn, wrh, wrl, br)
    return pl.pallas_call(
        _combine_body,
        grid=(t // tc,),
        in_specs=[pl.BlockSpec((tc, a.shape[1]), row) for a in acts]
        + [pl.BlockSpec(c.shape, const) for c in consts],
        out_specs=[pl.BlockSpec((tc, d), row), pl.BlockSpec((tc, d), row), pl.BlockSpec((tc, LANES), row)],
        out_shape=[jax.ShapeDtypeStruct((t, d), F32), jax.ShapeDtypeStruct((t, d), BF16),
                   jax.ShapeDtypeStruct((t, LANES), F32)],
        compiler_params=_cparams(("parallel",)),
        name="combine",
    )(*acts, *consts)


def _moe_body(t_ref, gate_ref, x1_ref, wgu_ref, wd_ref, o_ref):
    e = pl.program_id(1)

    @pl.when(e == 0)
    def _():
        o_ref[...] = x1_ref[...]

    hgu = _dot(t_ref[...], wgu_ref[0])
    a = hgu[:, :EXPERT_FF]
    hid = a * _sigmoid(a) * hgu[:, EXPERT_FF:]
    gate = gate_ref[...]
    lane = lax.broadcasted_iota(jnp.int32, gate.shape, 1)
    gcol = jnp.sum(jnp.where(lane == e + EXPERT_LANE0, gate, 0.0), axis=-1, keepdims=True)
    o_ref[...] += _dot((hid * gcol).astype(BF16), wd_ref[0])


def _moe(t_bf, gate, x1, wgu, wd, tm):
    t, d = x1.shape
    row = lambda i, e: (i, 0)
    return pl.pallas_call(
        _moe_body,
        grid=(t // tm, N_EXPERTS),
        in_specs=[pl.BlockSpec((tm, d), row), pl.BlockSpec((tm, LANES), row), pl.BlockSpec((tm, d), row),
                  pl.BlockSpec((1, d, 2 * EXPERT_FF), lambda i, e: (e, 0, 0)),
                  pl.BlockSpec((1, EXPERT_FF, d), lambda i, e: (e, 0, 0))],
        out_specs=pl.BlockSpec((tm, d), row),
        out_shape=jax.ShapeDtypeStruct((t, d), F32),
        compiler_params=_cparams(("parallel", "arbitrary")),
        name="moe",
    )(t_bf, gate, x1, wgu, wd)


def _final_body(x_ref, p_ref, gple_ref, wpg_ref, wpp_ref, gfin_ref, o_ref):
    x = x_ref[...]
    r = _rms(x, gple_ref[...]).astype(BF16)
    ple_gate = _sigmoid(_dot(r, wpg_ref[...]))
    proj = _dot(p_ref[...].astype(BF16), wpp_ref[...])
    o_ref[...] = _rms(x + ple_gate * proj, gfin_ref[...])


def _final(x2, p2d, g_ple, wpg, wpp, g_final, tf):
    t, d = x2.shape
    row = lambda i: (i, 0)
    const = lambda i: (0, 0)
    return pl.pallas_call(
        _final_body,
        grid=(t // tf,),
        in_specs=[pl.BlockSpec((tf, d), row), pl.BlockSpec((tf, p2d.shape[1]), row),
                  pl.BlockSpec((1, d), const), pl.BlockSpec(wpg.shape, const),
                  pl.BlockSpec(wpp.shape, const), pl.BlockSpec((1, d), const)],
        out_specs=pl.BlockSpec((tf, d), row),
        out_shape=jax.ShapeDtypeStruct((t, d), F32),
        compiler_params=_cparams(("parallel",)),
        name="final",
    )(x2, p2d, g_ple, wpg, wpp, g_final)


def _pad_lanes(v, lane0):
    return jnp.zeros((1, LANES), F32).at[0, lane0:lane0 + v.shape[0]].set(v.astype(F32))


def _tile(n, pref):
    return pref if n % pref == 0 else n


def _layer(x, p_l, g_mix, w_in, b_forget, conv_w, a_log, dt_bias, g_onorm, w_o_fox, w_o_delta, w_out,
           g_ffn, w_group, b_group, w_router, b_router, w_gate, w_up, w_down, g_ple, w_ple_gate, w_ple_proj,
           g_post, apply_post):
    b, s, d = x.shape
    t = b * s
    x2d = x.reshape(t, d)

    o_ff = 3 * FOX_WIDTH
    o_qkv = o_ff + FOX_HEADS
    o_da = o_qkv + 3 * GDN_WIDTH
    o_db = o_da + GDN_HEADS
    o_dz = o_db + GDN_HEADS
    o_gf = o_dz + GDN_WIDTH
    o_gd = o_gf + d
    w_small = jnp.concatenate(
        [w_in[:, o_ff:o_qkv], w_in[:, o_da:o_db], w_in[:, o_db:o_dz],
         jnp.zeros((d, LANES - FOX_HEADS - 2 * GDN_HEADS), w_in.dtype)], axis=1)
    wq = w_in[:, :o_ff].astype(BF16)
    wg = w_in[:, o_qkv:o_da].astype(BF16)
    wzs = jnp.concatenate([w_in[:, o_dz:o_gf], w_small], axis=1).astype(BF16)
    wgf = w_in[:, o_gf:o_gd].astype(BF16)
    wgd = w_in[:, o_gd:].astype(BF16)

    zq, zg, dz, small, gf, gd = _inproj(x2d, g_mix.reshape(1, d), wq, wg, wzs, wgf, wgd, _tile(t, 512))

    par = jnp.concatenate(
        [_pad_lanes(b_forget, 0) + _pad_lanes(dt_bias, FOX_HEADS), _pad_lanes(a_log, FOX_HEADS),
         jnp.zeros((6, LANES), F32)], axis=0)
    col = _gates(small.reshape(b, s, LANES), par, _tile(s, 256))

    nc = s // GDN_CHUNK
    gc = col[:, :, FOX_HEADS:FOX_HEADS + GDN_HEADS].reshape(b, nc, GDN_CHUNK, GDN_HEADS)
    gcrow = jnp.transpose(gc, (0, 1, 3, 2)).reshape(b, nc, 1, GDN_HEADS * GDN_CHUNK)

    y_fox = _fox(zq.reshape(b, s, 3 * FOX_WIDTH), col, _tile(s, 512))

    qn, kn, vv = _gdnprep(zg.reshape(b, s, 3 * GDN_WIDTH), conv_w.astype(F32), _tile(s, 512))
    o_gdn = _gdn(qn, kn, vv, col, gcrow, _tile(s, 512))

    w_r = jnp.concatenate([w_group, w_router,
                           jnp.zeros((d, LANES - N_GROUPS - N_EXPERTS), w_group.dtype)], axis=1).astype(F32)
    wrh = w_r.astype(BF16)
    wrl = (w_r - wrh.astype(F32)).astype(BF16)
    br = _pad_lanes(b_group, 0) + _pad_lanes(b_router, EXPERT_LANE0)
    x1, t_bf, gate = _combine(
        y_fox.reshape(t, FOX_WIDTH), o_gdn.reshape(t, GDN_WIDTH), dz, gf, gd, x2d,
        w_o_fox.astype(BF16), w_o_delta.astype(BF16), w_out.astype(BF16),
        g_onorm.reshape(1, GDN_HEAD_DIM).astype(F32), g_ffn.reshape(1, d).astype(F32), wrh, wrl, br,
        _tile(t, 512))

    wgu = jnp.concatenate([w_gate, w_up], axis=2).astype(BF16)
    x2 = _moe(t_bf, gate, x1, wgu, w_down.astype(BF16), _tile(t, 1024))

    out = _final(x2, p_l.reshape(t, -1), g_ple.reshape(1, d).astype(F32), w_ple_gate.astype(BF16),
                 w_ple_proj.astype(BF16), g_post.reshape(1, d).astype(F32), _tile(t, 512))
    return out.reshape(b, s, d)


def kernel(x, p, g_mix, w_in, b_forget, conv_w, a_log, dt_bias, g_onorm, w_o_fox, w_o_delta, w_out,
           g_ffn, w_group, b_group, w_router, b_router, w_gate, w_up, w_down, g_ple, w_ple_gate, w_ple_proj,
           g_final):
    depth = p.shape[0]
    assert depth == 1, "the final rmsnorm is fused into the last layer's epilogue; depth 1 only"
    i = 0
    return _layer(x, p[i], g_mix[i], w_in[i], b_forget[i], conv_w[i], a_log[i], dt_bias[i], g_onorm[i],
                  w_o_fox[i], w_o_delta[i], w_out[i], g_ffn[i], w_group[i], b_group[i], w_router[i],
                  b_router[i], w_gate[i], w_up[i], w_down[i], g_ple[i], w_ple_gate[i], w_ple_proj[i],
                  g_final, True)
```

```python
import functools

import jax
import jax.numpy as jnp
from jax import lax
from jax.experimental import pallas as pl
from jax.experimental.pallas import tpu as pltpu

F32 = jnp.float32
BF16 = jnp.bfloat16
EPS = 1e-6
NEG = -1e30

FOX_HEADS = 8
FOX_HEAD_DIM = 64
FOX_WIDTH = FOX_HEADS * FOX_HEAD_DIM
GDN_HEADS = 4
GDN_HEAD_DIM = 128
GDN_CONV = 4
GDN_CHUNK = 64
GDN_WIDTH = GDN_HEADS * GDN_HEAD_DIM
N_GROUPS = 4
EXPERTS_PER_GROUP = 8
N_EXPERTS = N_GROUPS * EXPERTS_PER_GROUP
EXPERT_FF = 256
LANES = 128
EXPERT_LANE0 = N_GROUPS

VMEM_LIMIT_BYTES = 56 * 1024 * 1024


def _cparams(sem):
    return pltpu.CompilerParams(dimension_semantics=sem, vmem_limit_bytes=VMEM_LIMIT_BYTES)


def _rms(x, g):
    return x * lax.rsqrt(jnp.mean(x * x, axis=-1, keepdims=True) + EPS) * g


def _sigmoid(x):
    return 1.0 / (1.0 + jnp.exp(-x))


def _dot(a, b):
    return jnp.dot(a, b, preferred_element_type=F32)


def _dot_nt(a, b):
    return lax.dot_general(a, b, (((1,), (1,)), ((), ())), preferred_element_type=F32)


def _dot_tn(a, b):
    return lax.dot_general(a, b, (((0,), (0,)), ((), ())), preferred_element_type=F32)


def _split3(v):
    hi = v.astype(BF16)
    r = v - hi.astype(F32)
    mid = r.astype(BF16)
    lo = (r - mid.astype(F32)).astype(BF16)
    return hi, mid, lo


def _dot_exact_lhs01(mat01, v):
    hi, mid, lo = _split3(v)
    return _dot(mat01, hi) + _dot(mat01, mid) + _dot(mat01, lo)


def _inproj_body(x_ref, g_ref, wq_ref, wg_ref, wzs_ref, wgf_ref, wgd_ref,
                 oq_ref, og_ref, oz_ref, osm_ref, ogf_ref, ogd_ref):
    hb = _rms(x_ref[...], g_ref[...]).astype(BF16)
    oq_ref[...] = _dot(hb, wq_ref[...]).astype(BF16)
    og_ref[...] = _dot(hb, wg_ref[...]).astype(BF16)
    zs = _dot(hb, wzs_ref[...])
    oz_ref[...] = zs[:, :GDN_WIDTH].astype(BF16)
    osm_ref[...] = zs[:, GDN_WIDTH:]
    ogf_ref[...] = _dot(hb, wgf_ref[...]).astype(BF16)
    ogd_ref[...] = _dot(hb, wgd_ref[...]).astype(BF16)


def _inproj(x2d, g_mix, wq, wg, wzs, wgf, wgd, tm):
    t, d = x2d.shape
    row = lambda i: (i, 0)
    const = lambda i: (0, 0)
    widths = (3 * FOX_WIDTH, 3 * GDN_WIDTH, GDN_WIDTH, LANES, d, d)
    dtypes = (BF16, BF16, BF16, F32, BF16, BF16)
    return pl.pallas_call(
        _inproj_body,
        grid=(t // tm,),
        in_specs=[pl.BlockSpec((tm, d), row), pl.BlockSpec((1, d), const)]
        + [pl.BlockSpec(w.shape, const) for w in (wq, wg, wzs, wgf, wgd)],
        out_specs=[pl.BlockSpec((tm, n), row) for n in widths],
        out_shape=[jax.ShapeDtypeStruct((t, n), dt) for n, dt in zip(widths, dtypes)],
        compiler_params=_cparams(("parallel",)),
        name="inproj",
    )(x2d, g_mix, wq, wg, wzs, wgf, wgd)


def _gates_body(sm_ref, par_ref, o_ref, carry_ref, *, tp):
    @pl.when(pl.program_id(1) == 0)
    def _():
        carry_ref[...] = jnp.zeros_like(carry_ref)

    sm = sm_ref[0]
    lane = lax.broadcasted_iota(jnp.int32, sm.shape, 1)
    z = sm + par_ref[0:1, :]
    soft = jnp.log1p(jnp.exp(-jnp.abs(z)))
    softplus = jnp.maximum(z, 0.0) + soft
    logf = jnp.minimum(z, 0.0) - soft
    g = -jnp.exp(par_ref[1:2, :]) * softplus
    beta = _sigmoid(sm)
    is_f = lane < FOX_HEADS
    is_g = jnp.logical_and(lane >= FOX_HEADS, lane < FOX_HEADS + GDN_HEADS)
    is_b = jnp.logical_and(lane >= FOX_HEADS + GDN_HEADS, lane < FOX_HEADS + 2 * GDN_HEADS)
    val = jnp.where(is_f, logf, jnp.where(is_g, g, 0.0))
    r = lax.broadcasted_iota(jnp.int32, (tp, tp), 0)
    c = lax.broadcasted_iota(jnp.int32, (tp, tp), 1)
    lower = r >= c
    tri = jnp.where(lower, 1.0, 0.0).astype(BF16)
    same_chunk = (r // GDN_CHUNK) == (c // GDN_CHUNK)
    tri_chunk = jnp.where(jnp.logical_and(lower, same_chunk), 1.0, 0.0).astype(BF16)
    cum = _dot_exact_lhs01(tri, val) + carry_ref[...]
    gcs = _dot_exact_lhs01(tri_chunk, val)
    carry_ref[...] = cum[tp - 1:tp, :]
    o_ref[0] = jnp.where(is_f, cum, jnp.where(is_g, gcs, jnp.where(is_b, beta, 0.0)))


def _gates(small3d, par, tp):
    b, s, _ = small3d.shape
    return pl.pallas_call(
        functools.partial(_gates_body, tp=tp),
        grid=(b, s // tp),
        in_specs=[pl.BlockSpec((1, tp, LANES), lambda bi, i: (bi, i, 0)),
                  pl.BlockSpec((8, LANES), lambda bi, i: (0, 0))],
        out_specs=pl.BlockSpec((1, tp, LANES), lambda bi, i: (bi, i, 0)),
        out_shape=jax.ShapeDtypeStruct((b, s, LANES), F32),
        scratch_shapes=[pltpu.VMEM((1, LANES), F32)],
        compiler_params=_cparams(("parallel", "arbitrary")),
        name="gates",
    )(small3d, par)


LOG2E = 1.4426950408889634


FOX_VROWS = FOX_HEAD_DIM + 16


def _split3_f32(v):
    hi = v.astype(BF16).astype(F32)
    r = v - hi
    mid = r.astype(BF16).astype(F32)
    lo = (r - mid).astype(BF16).astype(F32)
    return hi, mid, lo


def _fox_body(q_ref, k_ref, v_ref, col_ref, o_ref, kx_ref, vt_ref, sa_ref, sb_ref, acc_ref, m_ref,
              *, tq, s_len):
    hp = pl.program_id(1)
    qi = pl.program_id(2)
    hd = FOX_HEAD_DIM

    @pl.when(qi == 0)
    def _():
        def prep(ci, carry):
            r0 = pl.multiple_of(ci * tq, tq)
            kx_ref[pl.ds(r0, tq), :LANES] = k_ref[0, pl.ds(r0, tq), :]
            col = col_ref[0, pl.ds(r0, tq), :]
            lane = lax.broadcasted_iota(jnp.int32, col.shape, 1)
            bias = jnp.zeros(col.shape, F32)
            for hh in range(2):
                ck = jnp.sum(jnp.where(lane == 2 * hp + hh, col, 0.0), axis=-1, keepdims=True) * LOG2E
                for k, piece in enumerate(_split3_f32(ck)):
                    bias = jnp.where(lane == 3 * hh + k, piece, bias)
            kx_ref[pl.ds(r0, tq), LANES:] = bias.astype(BF16)
            vt = v_ref[0, pl.ds(r0, tq), :].astype(F32).T
            ones_row = jnp.where(lax.broadcasted_iota(jnp.int32, (FOX_VROWS - hd, tq), 0) == 0, 1.0, 0.0)
            for hh in range(2):
                vt_ref[hh, :hd, pl.ds(r0, tq)] = vt[hh * hd:(hh + 1) * hd].astype(BF16)
                vt_ref[hh, hd:, pl.ds(r0, tq)] = ones_row.astype(BF16)
            return carry

        lax.fori_loop(0, s_len // tq, prep, 0)

    lane = lax.broadcasted_iota(jnp.int32, (tq, LANES), 1)
    q = (q_ref[0].astype(F32) * (hd ** -0.5 * LOG2E)).astype(BF16)
    zero = jnp.zeros_like(q)
    qx = []
    for hh in range(2):
        qh = jnp.where(jnp.logical_and(lane >= hh * hd, lane < (hh + 1) * hd), q, zero)
        sel = jnp.where(jnp.logical_and(lane >= 3 * hh, lane < 3 * hh + 3), -1.0, 0.0).astype(BF16)
        qx.append(jnp.concatenate([qh, sel], axis=1))
    m_ref[...] = jnp.full(m_ref.shape, NEG, F32)
    acc_ref[...] = jnp.zeros(acc_ref.shape, F32)

    def scores(j, dst_ref):
        start = pl.multiple_of(j * tq, tq)
        kx = kx_ref[pl.ds(start, tq), :]
        for hh in range(2):
            dst_ref[hh] = _dot_nt(kx, qx[hh])

    def consume(j, src_ref, masked):
        start = pl.multiple_of(j * tq, tq)
        for hh in range(2):
            s = src_ref[hh]
            if masked:
                key = lax.broadcasted_iota(jnp.int32, (tq, tq), 0)
                qry = lax.broadcasted_iota(jnp.int32, (tq, tq), 1)
                s = jnp.where(key <= qry, s, NEG)
            m_old = m_ref[hh:hh + 1, :]
            m_new = jnp.maximum(m_old, jnp.max(s, axis=0, keepdims=True))
            alpha = jnp.exp2(m_old - m_new)
            p = jnp.exp2(s - m_new).astype(BF16)
            acc_ref[hh] = alpha * acc_ref[hh] + _dot(vt_ref[hh, :, pl.ds(start, tq)], p)
            m_ref[hh:hh + 1, :] = m_new

    scores(0, sa_ref)

    def loop_body(j, carry):
        @pl.when(j % 2 == 0)
        def _():
            scores(j + 1, sb_ref)
            consume(j, sa_ref, False)

        @pl.when(j % 2 == 1)
        def _():
            scores(j + 1, sa_ref)
            consume(j, sb_ref, False)

        return carry

    lax.fori_loop(0, qi, loop_body, 0)

    @pl.when(qi % 2 == 0)
    def _():
        consume(qi, sa_ref, True)

    @pl.when(qi % 2 == 1)
    def _():
        consume(qi, sb_ref, True)

    out_t = jnp.concatenate([acc_ref[hh, :hd] / acc_ref[hh, hd:hd + 1] for hh in range(2)], axis=0)
    o_ref[0] = out_t.T.astype(BF16)


def _fox(zq3d, col3d, tq):
    b, s, _ = zq3d.shape
    npair = FOX_HEADS // 2
    kblk = FOX_WIDTH // LANES
    return pl.pallas_call(
        functools.partial(_fox_body, tq=tq, s_len=s),
        grid=(b, npair, s // tq),
        in_specs=[pl.BlockSpec((1, tq, LANES), lambda bi, hp, qi: (bi, qi, hp)),
                  pl.BlockSpec((1, s, LANES), lambda bi, hp, qi: (bi, 0, kblk + hp)),
                  pl.BlockSpec((1, s, LANES), lambda bi, hp, qi: (bi, 0, 2 * kblk + hp)),
                  pl.BlockSpec((1, s, LANES), lambda bi, hp, qi: (bi, 0, 0))],
        out_specs=pl.BlockSpec((1, tq, LANES), lambda bi, hp, qi: (bi, qi, hp)),
        out_shape=jax.ShapeDtypeStruct((b, s, FOX_WIDTH), BF16),
        scratch_shapes=[pltpu.VMEM((s, 2 * LANES), BF16), pltpu.VMEM((2, FOX_VROWS, s), BF16),
                        pltpu.VMEM((2, tq, tq), F32), pltpu.VMEM((2, tq, tq), F32),
                        pltpu.VMEM((2, FOX_VROWS, tq), F32), pltpu.VMEM((2, tq), F32)],
        compiler_params=_cparams(("parallel", "parallel", "arbitrary")),
        name="fox",
    )(zq3d, zq3d, zq3d, col3d)


HALO = 16


def _gdnprep_body(x_ref, halo_ref, cw_ref, oq_ref, ok_ref, ov_ref, ext_ref, *, tp):
    prev = halo_ref[0].astype(F32)
    ext_ref[0:HALO, :] = jnp.where(pl.program_id(1) > 0, prev, 0.0)
    ext_ref[HALO:, :] = x_ref[0].astype(F32)
    acc = cw_ref[GDN_CONV - 1:GDN_CONV, :] * ext_ref[HALO:HALO + tp, :]
    for j in range(GDN_CONV - 1):
        off = HALO - (GDN_CONV - 1) + j
        acc = acc + cw_ref[j:j + 1, :] * ext_ref[off:off + tp, :]
    y = acc * _sigmoid(acc)

    def l2(v):
        return v * lax.rsqrt(jnp.sum(v * v, axis=-1, keepdims=True) + EPS)

    for h in range(GDN_HEADS):
        lo, hi = h * GDN_HEAD_DIM, (h + 1) * GDN_HEAD_DIM
        oq_ref[0, :, lo:hi] = (l2(y[:, lo:hi]) * GDN_HEAD_DIM ** -0.5).astype(BF16)
        ok_ref[0, :, lo:hi] = l2(y[:, GDN_WIDTH + lo:GDN_WIDTH + hi]).astype(BF16)
    ov_ref[0] = y[:, 2 * GDN_WIDTH:].astype(BF16)


def _gdnprep(zg3d, conv_w, tp):
    b, s, c = zg3d.shape
    blk = lambda bi, i: (bi, i, 0)
    return pl.pallas_call(
        functools.partial(_gdnprep_body, tp=tp),
        grid=(b, s // tp),
        in_specs=[pl.BlockSpec((1, tp, c), blk),
                  pl.BlockSpec((1, HALO, c), lambda bi, i: (bi, jnp.maximum(i * (tp // HALO) - 1, 0), 0)),
                  pl.BlockSpec(conv_w.shape, lambda bi, i: (0, 0))],
        out_specs=[pl.BlockSpec((1, tp, GDN_WIDTH), blk)] * 3,
        out_shape=[jax.ShapeDtypeStruct((b, s, GDN_WIDTH), BF16)] * 3,
        scratch_shapes=[pltpu.VMEM((tp + HALO, c), F32)],
        compiler_params=_cparams(("parallel", "parallel")),
        name="gdnprep",
    )(zg3d, zg3d, conv_w)


def _stack_heads(x):
    return jnp.concatenate([x[:, h * GDN_HEAD_DIM:(h + 1) * GDN_HEAD_DIM] for h in range(GDN_HEADS)], axis=0)


def _gdn_body(q_ref, k_ref, v_ref, col_ref, grow_ref, o_ref, state_ref, *bufs, tg):
    C = GDN_CHUNK
    R = GDN_HEADS * C
    dh = GDN_HEAD_DIM
    n_chunks = tg // C
    step = pl.program_id(1)
    buf_sets = (bufs[:len(bufs) // 2], bufs[len(bufs) // 2:])

    @pl.when(step == 0)
    def _():
        for ref in (state_ref,) + tuple(bufs):
            ref[...] = jnp.zeros_like(ref)

    r = lax.broadcasted_iota(jnp.int32, (R, R), 0)
    c = lax.broadcasted_iota(jnp.int32, (R, R), 1)
    same_head = (r // C) == (c // C)
    lower = jnp.logical_and(same_head, r >= c)
    strict = jnp.logical_and(same_head, r > c)
    gc_lane0 = FOX_HEADS
    beta_lane0 = FOX_HEADS + GDN_HEADS

    def advance(ci, rd):
        u_ref, w_ref, intra_ref, qd_ref, kd_ref, gl_ref = rd
        r0 = ci * C
        u, w, intra = u_ref[ci], w_ref[ci], intra_ref[ci]
        q_dec, k_dec = qd_ref[ci], kd_ref[ci]
        v_new = []
        o_state = []
        for h in range(GDN_HEADS):
            sl = slice(h * C, (h + 1) * C)
            st = state_ref[h].astype(BF16)
            v_new.append(u[sl] - _dot(w[sl], st))
            o_state.append(_dot(q_dec[sl], st))
        v_new = jnp.concatenate(v_new, axis=0)
        v_new_b = v_new.astype(BF16)
        o_all = jnp.concatenate(o_state, axis=0) + _dot(intra, v_new_b)
        for h in range(GDN_HEADS):
            sl = slice(h * C, (h + 1) * C)
            state_ref[h] = state_ref[h] * gl_ref[ci, h:h + 1, :] + _dot_tn(k_dec[sl], v_new_b[sl])
        o_ref[0, r0:r0 + C, :] = jnp.concatenate(
            [o_all[h * C:(h + 1) * C] for h in range(GDN_HEADS)], axis=1).astype(BF16)

    chunks = range(n_chunks)

    def prepare_all(wr):
        u_ref, w_ref, intra_ref, qd_ref, kd_ref, gl_ref = wr
        qs, ks, vs, gc_col, beta_col, gl_row, lmat, intra = [], [], [], [], [], [], [], []
        for ci in chunks:
            r0 = ci * C
            qs.append(_stack_heads(q_ref[0, r0:r0 + C, :]).astype(F32))
            ks.append(_stack_heads(k_ref[0, r0:r0 + C, :]).astype(F32))
            vs.append(_stack_heads(v_ref[0, r0:r0 + C, :]).astype(F32))
            col = col_ref[0, r0:r0 + C, :]
            gc_col.append(
                jnp.concatenate([col[:, gc_lane0 + h:gc_lane0 + h + 1] for h in range(GDN_HEADS)], axis=0))
            beta_col.append(
                jnp.concatenate([col[:, beta_lane0 + h:beta_lane0 + h + 1] for h in range(GDN_HEADS)], axis=0))
            gl_row.append(col[C - 1:C, :])
            gc_row = grow_ref[0, ci, :, :]
            decay = jnp.exp(jnp.where(lower, gc_col[ci] - gc_row, NEG))
            ksb = ks[ci].astype(BF16)
            kk = _dot_nt(ksb, ksb)
            qk = _dot_nt(qs[ci].astype(BF16), ksb)
            lmat.append(jnp.where(strict, kk * decay * beta_col[ci], 0.0))
            intra.append((qk * decay).astype(BF16))
        n_mat = [-l for l in lmat]
        pw = lmat
        for _ in range(5):
            pwb = [p.astype(BF16) for p in pw]
            pw = [_dot(p, p) for p in pwb]
            n_mat = [n + p + _dot(n.astype(BF16), p.astype(BF16)) for n, p in zip(n_mat, pw)]
        for ci in chunks:
            e_gc = jnp.exp(gc_col[ci])
            rhs = jnp.concatenate([vs[ci] * beta_col[ci], ks[ci] * (beta_col[ci] * e_gc)], axis=1)
            sol = rhs + _dot(n_mat[ci].astype(BF16), rhs.astype(BF16))
            gl_col = jnp.concatenate(
                [jnp.broadcast_to(gl_row[ci][:, gc_lane0 + h:gc_lane0 + h + 1], (C, 1))
                 for h in range(GDN_HEADS)], axis=0)
            u_ref[ci] = sol[:, :dh]
            w_ref[ci] = sol[:, dh:].astype(BF16)
            intra_ref[ci] = intra[ci]
            qd_ref[ci] = (qs[ci] * e_gc).astype(BF16)
            kd_ref[ci] = (ks[ci] * jnp.exp(gl_col - gc_col[ci])).astype(BF16)
            for h in range(GDN_HEADS):
                gl_ref[ci, h:h + 1, :] = jnp.broadcast_to(
                    jnp.exp(gl_row[ci][:, gc_lane0 + h:gc_lane0 + h + 1]), (1, LANES))

    def run(rd, wr):
        for ci in chunks:
            advance(ci, rd)
        prepare_all(wr)

    @pl.when(step % 2 == 0)
    def _():
        run(buf_sets[0], buf_sets[1])

    @pl.when(step % 2 == 1)
    def _():
        run(buf_sets[1], buf_sets[0])


def _gdn(qn, kn, vv, col3d, gcrow, tg):
    b, s, _ = qn.shape
    n = s // tg
    nck = tg // GDN_CHUNK
    rows = GDN_HEADS * GDN_CHUNK
    dh = GDN_HEAD_DIM
    blk_in = lambda bi, i: (bi, jnp.minimum(i, n - 1), 0)
    blk_out = lambda bi, i: (bi, jnp.maximum(i - 1, 0), 0)
    return pl.pallas_call(
        functools.partial(_gdn_body, tg=tg),
        grid=(b, n + 1),
        in_specs=[pl.BlockSpec((1, tg, GDN_WIDTH), blk_in)] * 3
        + [pl.BlockSpec((1, tg, LANES), blk_in),
           pl.BlockSpec((1, nck, 1, rows), lambda bi, i: (bi, jnp.minimum(i, n - 1), 0, 0))],
        out_specs=pl.BlockSpec((1, tg, GDN_WIDTH), blk_out),
        out_shape=jax.ShapeDtypeStruct((b, s, GDN_WIDTH), BF16),
        scratch_shapes=[pltpu.VMEM((GDN_HEADS, dh, dh), F32)] + 2 * [
            pltpu.VMEM((nck, rows, dh), F32), pltpu.VMEM((nck, rows, dh), BF16),
            pltpu.VMEM((nck, rows, rows), BF16), pltpu.VMEM((nck, rows, dh), BF16),
            pltpu.VMEM((nck, rows, dh), BF16), pltpu.VMEM((nck, 8, LANES), F32)],
        compiler_params=_cparams(("parallel", "arbitrary")),
        name="gdn",
    )(qn, kn, vv, col3d, gcrow)


MOE_SLOT_TILE = 512

RT_E1, RT_E2, RT_RANK1, RT_RANK2, RT_G1, RT_G2 = range(6)


def _combine_body(yf_ref, og_ref, dz_ref, gf_ref, gd_ref, x_ref, wof_ref, wod_ref, wout_ref,
                  gon_ref, gffn_ref, wrh_ref, wrl_ref, br_ref, x1_ref, t_ref, rt_ref, cnt_ref, carry_ref,
                  *, tc):
    @pl.when(pl.program_id(0) == 0)
    def _():
        carry_ref[...] = jnp.zeros_like(carry_ref)

    on = []
    for h in range(GDN_HEADS):
        sl = slice(h * GDN_HEAD_DIM, (h + 1) * GDN_HEAD_DIM)
        dz = dz_ref[:, sl].astype(F32)
        on.append(_rms(og_ref[:, sl].astype(F32), gon_ref[...]) * (dz * _sigmoid(dz)))
    on = jnp.concatenate(on, axis=1).astype(BF16)
    y_fox = _dot(yf_ref[...], wof_ref[...])
    y_delta = _dot(on, wod_ref[...])
    merged = _sigmoid(gf_ref[...].astype(F32)) * y_fox + _sigmoid(gd_ref[...].astype(F32)) * y_delta
    x1 = x_ref[...] + _dot(merged.astype(BF16), wout_ref[...])
    x1_ref[...] = x1
    t32 = _rms(x1, gffn_ref[...])
    th = t32.astype(BF16)
    tl = (t32 - th.astype(F32)).astype(BF16)
    t_ref[...] = t32
    logits = _dot(th, wrh_ref[...]) + _dot(tl, wrh_ref[...]) + _dot(th, wrl_ref[...]) + br_ref[...]
    lane = lax.broadcasted_iota(jnp.int32, logits.shape, 1)
    gl = jnp.where(lane < N_GROUPS, logits, NEG)
    gmax = jnp.max(gl, axis=-1, keepdims=True)
    g_sel = jnp.min(jnp.where(gl == gmax, lane, LANES), axis=-1, keepdims=True)
    p_sel = 1.0 / jnp.sum(jnp.exp(gl - gmax), axis=-1, keepdims=True)
    lo = EXPERT_LANE0 + EXPERTS_PER_GROUP * g_sel
    in_grp = jnp.logical_and(lane >= lo, lane < lo + EXPERTS_PER_GROUP)
    el = jnp.where(in_grp, logits, NEG)
    emax = jnp.max(el, axis=-1, keepdims=True)
    ee = jnp.where(in_grp, jnp.exp(el - emax), 0.0)
    pe = ee / jnp.sum(ee, axis=-1, keepdims=True)
    pe = jnp.where(in_grp, pe, -1.0)
    p1 = jnp.max(pe, axis=-1, keepdims=True)
    i1 = jnp.min(jnp.where(pe == p1, lane, LANES), axis=-1, keepdims=True)
    pe2 = jnp.where(lane == i1, -1.0, pe)
    p2 = jnp.max(pe2, axis=-1, keepdims=True)
    i2 = jnp.min(jnp.where(pe2 == p2, lane, LANES), axis=-1, keepdims=True)
    den = p1 + p2
    hit1 = lane == i1
    hit2 = lane == i2
    assign = jnp.where(jnp.logical_or(hit1, hit2), 1.0, 0.0)
    r = lax.broadcasted_iota(jnp.int32, (tc, tc), 0)
    c = lax.broadcasted_iota(jnp.int32, (tc, tc), 1)
    before = jnp.where(r > c, 1.0, 0.0).astype(BF16)
    prefix = _dot(before, assign.astype(BF16)) + carry_ref[...]
    rank1 = jnp.sum(jnp.where(hit1, prefix, 0.0), axis=-1, keepdims=True)
    rank2 = jnp.sum(jnp.where(hit2, prefix, 0.0), axis=-1, keepdims=True)
    carry_ref[...] = prefix[tc - 1:tc, :] + assign[tc - 1:tc, :]
    cnt_ref[...] = carry_ref[...]
    cols = ((i1 - EXPERT_LANE0).astype(F32), (i2 - EXPERT_LANE0).astype(F32), rank1, rank2,
            p_sel * (p1 / den), p_sel * (p2 / den))
    rt = jnp.zeros(logits.shape, F32)
    for k, v in enumerate(cols):
        rt = jnp.where(lane == k, v, rt)
    rt_ref[...] = rt


def _combine(yf, og, dz, gf, gd, x2d, wof, wod, wout, g_on, g_ffn, wrh, wrl, br, tc):
    t, d = x2d.shape
    row = lambda i: (i, 0)
    const = lambda i: (0, 0)
    acts = (yf, og, dz, gf, gd, x2d)
    consts = (wof, wod, wout, g_on, g_ffn, wrh, wrl, br)
    return pl.pallas_call(
        functools.partial(_combine_body, tc=tc),
        grid=(t // tc,),
        in_specs=[pl.BlockSpec((tc, a.shape[1]), row) for a in acts]
        + [pl.BlockSpec(c.shape, const) for c in consts],
        out_specs=[pl.BlockSpec((tc, d), row), pl.BlockSpec((tc, d), row), pl.BlockSpec((tc, LANES), row),
                   pl.BlockSpec((1, LANES), const)],
        out_shape=[jax.ShapeDtypeStruct((t, d), F32), jax.ShapeDtypeStruct((t, d), F32),
                   jax.ShapeDtypeStruct((t, LANES), F32), jax.ShapeDtypeStruct((1, LANES), F32)],
        scratch_shapes=[pltpu.VMEM((1, LANES), F32)],
        compiler_params=_cparams(("arbitrary",)),
        name="combine",
    )(*acts, *consts)


def _slots_body(rt_ref, cnt_ref, o_ref, *, tmx):
    lane1 = lax.broadcasted_iota(jnp.int32, (1, LANES), 1)
    is_expert = jnp.logical_and(lane1 >= EXPERT_LANE0, lane1 < EXPERT_LANE0 + N_EXPERTS)
    padded = jnp.where(is_expert, jnp.floor((cnt_ref[...] + (tmx - 1)) / tmx) * tmx, 0.0)
    r = lax.broadcasted_iota(jnp.int32, (LANES, LANES), 0)
    c = lax.broadcasted_iota(jnp.int32, (LANES, LANES), 1)
    before = jnp.where(r < c, 1.0, 0.0).astype(BF16)
    hi, mid, lo = _split3(jnp.broadcast_to(padded, (8, LANES)))
    offs = (_dot(hi, before) + _dot(mid, before) + _dot(lo, before))[0:1, :]
    rt = rt_ref[...]
    lane = lax.broadcasted_iota(jnp.int32, rt.shape, 1)
    expert_of_lane = (lane - EXPERT_LANE0).astype(F32)
    out = jnp.zeros(rt.shape, F32)
    for k, (ce, cr) in enumerate(((RT_E1, RT_RANK1), (RT_E2, RT_RANK2))):
        start = jnp.sum(jnp.where(expert_of_lane == rt[:, ce:ce + 1], offs, 0.0), axis=-1, keepdims=True)
        out = jnp.where(lane == k, start + rt[:, cr:cr + 1], out)
    o_ref[...] = out.T[:8, :].astype(jnp.int32)


def _slots(rt, cnt, tmx, ts):
    t = rt.shape[0]
    return pl.pallas_call(
        functools.partial(_slots_body, tmx=tmx),
        grid=(t // ts,),
        in_specs=[pl.BlockSpec((ts, LANES), lambda i: (i, 0)), pl.BlockSpec((1, LANES), lambda i: (0, 0))],
        out_specs=pl.BlockSpec((8, ts), lambda i: (0, i)),
        out_shape=jax.ShapeDtypeStruct((8, t), jnp.int32),
        compiler_params=_cparams(("parallel",)),
        name="slots",
    )(rt, cnt)


def _dispatch_body(zflag_ref, pos_ref, t_ref, xs_ref, zbuf, sem, zsem, *, td, tmx, n_tiles):
    @pl.when(pl.program_id(0) == 0)
    def _():
        zbuf[...] = jnp.zeros_like(zbuf)

        def zero_tile(k, carry):
            @pl.when(zflag_ref[k] != 0)
            def _():
                cp = pltpu.make_async_copy(zbuf, xs_ref.at[pl.ds(pl.multiple_of(k * tmx, tmx), tmx), :], zsem)
                cp.start()
                cp.wait()

            return carry

        lax.fori_loop(0, n_tiles, zero_tile, 0)

    def issue(r, carry):
        src = t_ref.at[pl.ds(r, 1), :]
        pltpu.make_async_copy(src, xs_ref.at[pl.ds(pos_ref[0, 0, r], 1), :], sem).start(priority=0)
        pltpu.make_async_copy(src, xs_ref.at[pl.ds(pos_ref[0, 0, td + r], 1), :], sem).start(priority=1)
        return carry

    lax.fori_loop(0, td, issue, 0, unroll=8)
    for _ in range(2):
        pltpu.make_async_copy(t_ref, xs_ref.at[pl.ds(0, td), :], sem).wait()


def _dispatch(zflag, tpk, pos12, n_slots, td, tmx):
    t, w = tpk.shape
    n_tiles = n_slots // tmx
    grid_spec = pltpu.PrefetchScalarGridSpec(
        num_scalar_prefetch=1,
        grid=(t // td,),
        in_specs=[pl.BlockSpec((1, 1, 2 * td), lambda i, zf: (i, 0, 0), memory_space=pltpu.SMEM),
                  pl.BlockSpec((td, w), lambda i, zf: (i, 0))],
        out_specs=pl.BlockSpec(memory_space=pl.ANY),
        scratch_shapes=[pltpu.VMEM((tmx, w), F32), pltpu.SemaphoreType.DMA(()),
                        pltpu.SemaphoreType.DMA(())],
    )
    return pl.pallas_call(
        functools.partial(_dispatch_body, td=td, tmx=tmx, n_tiles=n_tiles),
        grid_spec=grid_spec,
        out_shape=jax.ShapeDtypeStruct((n_slots, w), F32),
        compiler_params=_cparams(("arbitrary",)),
        name="dispatch",
    )(zflag, pos12, tpk)


def _experts_body(te_ref, nused_ref, xs_ref, wgu_ref, wd_ref, ys_ref):
    @pl.when(pl.program_id(0) < nused_ref[0])
    def _():
        hgu = _dot(xs_ref[...].astype(BF16), wgu_ref[0])
        a = hgu[:, :EXPERT_FF]
        hid = a * _sigmoid(a) * hgu[:, EXPERT_FF:]
        ys_ref[...] = _dot(hid.astype(BF16), wd_ref[0])

    @pl.when(pl.program_id(0) >= nused_ref[0])
    def _():
        ys_ref[...] = jnp.zeros_like(ys_ref)


def _experts(tile_expert, n_used, xs, wgu, wd, tmx):
    n_slots, w = xs.shape
    d = wgu.shape[1]
    grid_spec = pltpu.PrefetchScalarGridSpec(
        num_scalar_prefetch=2,
        grid=(n_slots // tmx,),
        in_specs=[pl.BlockSpec((tmx, w), lambda i, te, nu: (i, 0)),
                  pl.BlockSpec((1, d, 2 * EXPERT_FF), lambda i, te, nu: (te[i], 0, 0)),
                  pl.BlockSpec((1, EXPERT_FF, d), lambda i, te, nu: (te[i], 0, 0))],
        out_specs=pl.BlockSpec((tmx, w), lambda i, te, nu: (i, 0)),
    )
    return pl.pallas_call(
        _experts_body,
        grid_spec=grid_spec,
        out_shape=jax.ShapeDtypeStruct((n_slots, w), F32),
        compiler_params=_cparams(("arbitrary",)),
        name="experts",
    )(tile_expert, n_used, xs, wgu, wd)


def _final_body(pos_ref, posn_ref, x_ref, rt_ref, p_ref, gple_ref, wpg_ref, wpp_ref, gfin_ref, ys_ref,
                o_ref, gbuf, sems, *, tf):
    i = pl.program_id(0)
    n = pl.num_programs(0)
    slot = i % 2

    def gather(pref, dst_slot):
        def issue(r, carry):
            pltpu.make_async_copy(ys_ref.at[pl.ds(pref[0, 0, r], 1), :],
                                  gbuf.at[dst_slot, 0, pl.ds(r, 1), :], sems.at[dst_slot]).start(priority=0)
            pltpu.make_async_copy(ys_ref.at[pl.ds(pref[0, 0, tf + r], 1), :],
                                  gbuf.at[dst_slot, 1, pl.ds(r, 1), :], sems.at[dst_slot]).start(priority=1)
            return carry

        lax.fori_loop(0, tf, issue, 0, unroll=8)

    @pl.when(i == 0)
    def _():
        gather(pos_ref, 0)

    @pl.when(i + 1 < n)
    def _():
        gather(posn_ref, 1 - slot)

    for k in range(2):
        pltpu.make_async_copy(ys_ref.at[pl.ds(0, tf), :], gbuf.at[slot, k], sems.at[slot]).wait()

    rt = rt_ref[...]
    g1 = rt[:, RT_G1:RT_G1 + 1]
    g2 = rt[:, RT_G2:RT_G2 + 1]
    x = x_ref[...] + g1 * gbuf[slot, 0] + g2 * gbuf[slot, 1]
    r = _rms(x, gple_ref[...]).astype(BF16)
    ple_gate = _sigmoid(_dot(r, wpg_ref[...]))
    proj = _dot(p_ref[...].astype(BF16), wpp_ref[...])
    o_ref[...] = _rms(x + ple_gate * proj, gfin_ref[...])


def _final(pos12, x1, rt, p2d, g_ple, wpg, wpp, g_final, ys, tf):
    t, d = x1.shape
    n = t // tf
    row = lambda i: (i, 0)
    const = lambda i: (0, 0)
    smem_blk = lambda imap: pl.BlockSpec((1, 1, 2 * tf), imap, memory_space=pltpu.SMEM)
    return pl.pallas_call(
        functools.partial(_final_body, tf=tf),
        grid=(n,),
        in_specs=[smem_blk(lambda i: (i, 0, 0)), smem_blk(lambda i: (jnp.minimum(i + 1, n - 1), 0, 0)),
                  pl.BlockSpec((tf, d), row), pl.BlockSpec((tf, LANES), row),
                  pl.BlockSpec((tf, p2d.shape[1]), row),
                  pl.BlockSpec((1, d), const), pl.BlockSpec(wpg.shape, const),
                  pl.BlockSpec(wpp.shape, const), pl.BlockSpec((1, d), const),
                  pl.BlockSpec(memory_space=pl.ANY)],
        out_specs=pl.BlockSpec((tf, d), row),
        out_shape=jax.ShapeDtypeStruct((t, d), F32),
        scratch_shapes=[pltpu.VMEM((2, 2, tf, d), F32), pltpu.SemaphoreType.DMA((2,))],
        compiler_params=_cparams(("arbitrary",)),
        name="final",
    )(pos12, pos12, x1, rt, p2d, g_ple, wpg, wpp, g_final, ys)


def _pad_lanes(v, lane0):
    return jnp.zeros((1, LANES), F32).at[0, lane0:lane0 + v.shape[0]].set(v.astype(F32))


def _tile(n, pref):
    return pref if n % pref == 0 else n


def _layer(x, p_l, g_mix, w_in, b_forget, conv_w, a_log, dt_bias, g_onorm, w_o_fox, w_o_delta, w_out,
           g_ffn, w_group, b_group, w_router, b_router, w_gate, w_up, w_down, g_ple, w_ple_gate, w_ple_proj,
           g_post, apply_post):
    b, s, d = x.shape
    t = b * s
    x2d = x.reshape(t, d)

    o_ff = 3 * FOX_WIDTH
    o_qkv = o_ff + FOX_HEADS
    o_da = o_qkv + 3 * GDN_WIDTH
    o_db = o_da + GDN_HEADS
    o_dz = o_db + GDN_HEADS
    o_gf = o_dz + GDN_WIDTH
    o_gd = o_gf + d
    w_small = jnp.concatenate(
        [w_in[:, o_ff:o_qkv], w_in[:, o_da:o_db], w_in[:, o_db:o_dz],
         jnp.zeros((d, LANES - FOX_HEADS - 2 * GDN_HEADS), w_in.dtype)], axis=1)
    wq = w_in[:, :o_ff].astype(BF16)
    wg = w_in[:, o_qkv:o_da].astype(BF16)
    wzs = jnp.concatenate([w_in[:, o_dz:o_gf], w_small], axis=1).astype(BF16)
    wgf = w_in[:, o_gf:o_gd].astype(BF16)
    wgd = w_in[:, o_gd:].astype(BF16)

    zq, zg, dz, small, gf, gd = _inproj(x2d, g_mix.reshape(1, d), wq, wg, wzs, wgf, wgd, _tile(t, 512))

    par = jnp.concatenate(
        [_pad_lanes(b_forget, 0) + _pad_lanes(dt_bias, FOX_HEADS), _pad_lanes(a_log, FOX_HEADS),
         jnp.zeros((6, LANES), F32)], axis=0)
    col = _gates(small.reshape(b, s, LANES), par, _tile(s, 256))

    nc = s // GDN_CHUNK
    gc = col[:, :, FOX_HEADS:FOX_HEADS + GDN_HEADS].reshape(b, nc, GDN_CHUNK, GDN_HEADS)
    gcrow = jnp.transpose(gc, (0, 1, 3, 2)).reshape(b, nc, 1, GDN_HEADS * GDN_CHUNK)

    y_fox = _fox(zq.reshape(b, s, 3 * FOX_WIDTH), col, _tile(s, 512))

    qn, kn, vv = _gdnprep(zg.reshape(b, s, 3 * GDN_WIDTH), conv_w.astype(F32), _tile(s, 512))
    o_gdn = _gdn(qn, kn, vv, col, gcrow, _tile(s, 512))

    w_r = jnp.concatenate([w_group, w_router,
                           jnp.zeros((d, LANES - N_GROUPS - N_EXPERTS), w_group.dtype)], axis=1).astype(F32)
    wrh = w_r.astype(BF16)
    wrl = (w_r - wrh.astype(F32)).astype(BF16)
    br = _pad_lanes(b_group, 0) + _pad_lanes(b_router, EXPERT_LANE0)
    x1, tpk, rt, cnt = _combine(
        y_fox.reshape(t, FOX_WIDTH), o_gdn.reshape(t, GDN_WIDTH), dz, gf, gd, x2d,
        w_o_fox.astype(BF16), w_o_delta.astype(BF16), w_out.astype(BF16),
        g_onorm.reshape(1, GDN_HEAD_DIM).astype(F32), g_ffn.reshape(1, d).astype(F32), wrh, wrl, br,
        _tile(t, 512))

    tmx = MOE_SLOT_TILE
    n_slots = 2 * t + N_EXPERTS * tmx
    counts = cnt[0, EXPERT_LANE0:EXPERT_LANE0 + N_EXPERTS].astype(jnp.int32)
    padded = (counts + tmx - 1) // tmx * tmx
    ends = jnp.cumsum(padded)
    pos = _slots(rt, cnt, tmx, _tile(t, 512))
    pos1, pos2 = pos[0], pos[1]
    tile_start = jnp.arange(n_slots // tmx, dtype=jnp.int32) * tmx
    tile_expert = jnp.minimum(jnp.sum(tile_start[:, None] >= ends[None, :], axis=1), N_EXPERTS - 1).astype(jnp.int32)
    n_used = (ends[-1:] // tmx).astype(jnp.int32)
    last_of_expert = jnp.any(jnp.logical_and(tile_start[:, None] + tmx == ends[None, :], padded[None, :] > 0), axis=1)
    zflag = jnp.logical_or(last_of_expert, tile_start >= ends[-1]).astype(jnp.int32)

    def tiled_pos(tile):
        n = t // tile
        return jnp.concatenate([pos1.reshape(n, 1, tile), pos2.reshape(n, 1, tile)], axis=2)

    td = _tile(t, 512)
    xs = _dispatch(zflag, tpk, tiled_pos(td), n_slots, td, tmx)
    wgu = jnp.concatenate([w_gate, w_up], axis=2).astype(BF16)
    ys = _experts(tile_expert, n_used, xs, wgu, w_down.astype(BF16), tmx)

    tf = _tile(t, 256)
    out = _final(tiled_pos(tf), x1, rt, p_l.reshape(t, -1), g_ple.reshape(1, d).astype(F32),
                 w_ple_gate.astype(BF16), w_ple_proj.astype(BF16), g_post.reshape(1, d).astype(F32), ys, tf)
    return out.reshape(b, s, d)


def kernel(x, p, g_mix, w_in, b_forget, conv_w, a_log, dt_bias, g_onorm, w_o_fox, w_o_delta, w_out,
           g_ffn, w_group, b_group, w_router, b_router, w_gate, w_up, w_down, g_ple, w_ple_gate, w_ple_proj,
           g_final):
    depth = p.shape[0]
    assert depth == 1, "the final rmsnorm is fused into the last layer's epilogue; depth 1 only"
    i = 0
    return _layer(x, p[i], g_mix[i], w_in[i], b_forget[i], conv_w[i], a_log[i], dt_bias[i], g_onorm[i],
                  w_o_fox[i], w_o_delta[i], w_out[i], g_ffn[i], w_group[i], b_group[i], w_router[i],
                  b_router[i], w_gate[i], w_up[i], w_down[i], g_ple[i], w_ple_gate[i], w_ple_proj[i],
                  g_final, True)
```

```python
import functools

import jax
import jax.numpy as jnp
from jax import lax
from jax.experimental import pallas as pl
from jax.experimental.pallas import tpu as pltpu

F32 = jnp.float32
BF16 = jnp.bfloat16
EPS = 1e-6
NEG = -1e30

FOX_HEADS = 8
FOX_HEAD_DIM = 64
FOX_WIDTH = FOX_HEADS * FOX_HEAD_DIM
GDN_HEADS = 4
GDN_HEAD_DIM = 128
GDN_CONV = 4
GDN_CHUNK = 64
GDN_WIDTH = GDN_HEADS * GDN_HEAD_DIM
N_GROUPS = 4
EXPERTS_PER_GROUP = 8
N_EXPERTS = N_GROUPS * EXPERTS_PER_GROUP
EXPERT_FF = 256
LANES = 128
EXPERT_LANE0 = N_GROUPS

VMEM_LIMIT_BYTES = 56 * 1024 * 1024


def _cparams(sem):
    return pltpu.CompilerParams(dimension_semantics=sem, vmem_limit_bytes=VMEM_LIMIT_BYTES)


def _rms(x, g):
    return x * lax.rsqrt(jnp.mean(x * x, axis=-1, keepdims=True) + EPS) * g


def _sigmoid(x):
    return 1.0 / (1.0 + jnp.exp(-x))


def _dot(a, b):
    return jnp.dot(a, b, preferred_element_type=F32)


def _dot_nt(a, b):
    return lax.dot_general(a, b, (((1,), (1,)), ((), ())), preferred_element_type=F32)


def _dot_tn(a, b):
    return lax.dot_general(a, b, (((0,), (0,)), ((), ())), preferred_element_type=F32)


def _split3(v):
    hi = v.astype(BF16)
    r = v - hi.astype(F32)
    mid = r.astype(BF16)
    lo = (r - mid.astype(F32)).astype(BF16)
    return hi, mid, lo


def _dot_exact_lhs01(mat01, v):
    hi, mid, lo = _split3(v)
    return _dot(mat01, hi) + _dot(mat01, mid) + _dot(mat01, lo)


def _inproj_body(x_ref, g_ref, wq_ref, wg_ref, wzs_ref, wgf_ref, wgd_ref,
                 oq_ref, og_ref, oz_ref, osm_ref, ogf_ref, ogd_ref):
    hb = _rms(x_ref[...], g_ref[...]).astype(BF16)
    oq_ref[...] = _dot(hb, wq_ref[...]).astype(BF16)
    og_ref[...] = _dot(hb, wg_ref[...]).astype(BF16)
    zs = _dot(hb, wzs_ref[...])
    oz_ref[...] = zs[:, :GDN_WIDTH].astype(BF16)
    osm_ref[...] = zs[:, GDN_WIDTH:]
    ogf_ref[...] = _dot(hb, wgf_ref[...]).astype(BF16)
    ogd_ref[...] = _dot(hb, wgd_ref[...]).astype(BF16)


def _inproj(x2d, g_mix, wq, wg, wzs, wgf, wgd, tm):
    t, d = x2d.shape
    row = lambda i: (i, 0)
    const = lambda i: (0, 0)
    widths = (3 * FOX_WIDTH, 3 * GDN_WIDTH, GDN_WIDTH, LANES, d, d)
    dtypes = (BF16, BF16, BF16, F32, BF16, BF16)
    return pl.pallas_call(
        _inproj_body,
        grid=(t // tm,),
        in_specs=[pl.BlockSpec((tm, d), row), pl.BlockSpec((1, d), const)]
        + [pl.BlockSpec(w.shape, const) for w in (wq, wg, wzs, wgf, wgd)],
        out_specs=[pl.BlockSpec((tm, n), row) for n in widths],
        out_shape=[jax.ShapeDtypeStruct((t, n), dt) for n, dt in zip(widths, dtypes)],
        compiler_params=_cparams(("parallel",)),
        name="inproj",
    )(x2d, g_mix, wq, wg, wzs, wgf, wgd)


def _gates_body(sm_ref, par_ref, o_ref, carry_ref, *, tp):
    @pl.when(pl.program_id(1) == 0)
    def _():
        carry_ref[...] = jnp.zeros_like(carry_ref)

    sm = sm_ref[0]
    lane = lax.broadcasted_iota(jnp.int32, sm.shape, 1)
    z = sm + par_ref[0:1, :]
    soft = jnp.log1p(jnp.exp(-jnp.abs(z)))
    softplus = jnp.maximum(z, 0.0) + soft
    logf = jnp.minimum(z, 0.0) - soft
    g = -jnp.exp(par_ref[1:2, :]) * softplus
    beta = _sigmoid(sm)
    is_f = lane < FOX_HEADS
    is_g = jnp.logical_and(lane >= FOX_HEADS, lane < FOX_HEADS + GDN_HEADS)
    is_b = jnp.logical_and(lane >= FOX_HEADS + GDN_HEADS, lane < FOX_HEADS + 2 * GDN_HEADS)
    val = jnp.where(is_f, logf, jnp.where(is_g, g, 0.0))
    r = lax.broadcasted_iota(jnp.int32, (tp, tp), 0)
    c = lax.broadcasted_iota(jnp.int32, (tp, tp), 1)
    lower = r >= c
    tri = jnp.where(lower, 1.0, 0.0).astype(BF16)
    same_chunk = (r // GDN_CHUNK) == (c // GDN_CHUNK)
    tri_chunk = jnp.where(jnp.logical_and(lower, same_chunk), 1.0, 0.0).astype(BF16)
    cum = _dot_exact_lhs01(tri, val) + carry_ref[...]
    gcs = _dot_exact_lhs01(tri_chunk, val)
    carry_ref[...] = cum[tp - 1:tp, :]
    o_ref[0] = jnp.where(is_f, cum, jnp.where(is_g, gcs, jnp.where(is_b, beta, 0.0)))


def _gates(small3d, par, tp):
    b, s, _ = small3d.shape
    return pl.pallas_call(
        functools.partial(_gates_body, tp=tp),
        grid=(b, s // tp),
        in_specs=[pl.BlockSpec((1, tp, LANES), lambda bi, i: (bi, i, 0)),
                  pl.BlockSpec((8, LANES), lambda bi, i: (0, 0))],
        out_specs=pl.BlockSpec((1, tp, LANES), lambda bi, i: (bi, i, 0)),
        out_shape=jax.ShapeDtypeStruct((b, s, LANES), F32),
        scratch_shapes=[pltpu.VMEM((1, LANES), F32)],
        compiler_params=_cparams(("parallel", "arbitrary")),
        name="gates",
    )(small3d, par)


LOG2E = 1.4426950408889634


FOX_VROWS = FOX_HEAD_DIM + 16


def _split3_f32(v):
    hi = v.astype(BF16).astype(F32)
    r = v - hi
    mid = r.astype(BF16).astype(F32)
    lo = (r - mid).astype(BF16).astype(F32)
    return hi, mid, lo


def _fox_body(q_ref, k_ref, v_ref, col_ref, o_ref, kx_ref, vt_ref, sa_ref, sb_ref, acc_ref, m_ref,
              *, tq, s_len):
    hp = pl.program_id(1)
    qi = pl.program_id(2)
    hd = FOX_HEAD_DIM

    @pl.when(qi == 0)
    def _():
        def prep(ci, carry):
            r0 = pl.multiple_of(ci * tq, tq)
            kx_ref[pl.ds(r0, tq), :LANES] = k_ref[0, pl.ds(r0, tq), :]
            col = col_ref[0, pl.ds(r0, tq), :]
            lane = lax.broadcasted_iota(jnp.int32, col.shape, 1)
            bias = jnp.zeros(col.shape, F32)
            for hh in range(2):
                ck = jnp.sum(jnp.where(lane == 2 * hp + hh, col, 0.0), axis=-1, keepdims=True) * LOG2E
                for k, piece in enumerate(_split3_f32(ck)):
                    bias = jnp.where(lane == 3 * hh + k, piece, bias)
            kx_ref[pl.ds(r0, tq), LANES:] = bias.astype(BF16)
            vt = v_ref[0, pl.ds(r0, tq), :].astype(F32).T
            ones_row = jnp.where(lax.broadcasted_iota(jnp.int32, (FOX_VROWS - hd, tq), 0) == 0, 1.0, 0.0)
            for hh in range(2):
                vt_ref[hh, :hd, pl.ds(r0, tq)] = vt[hh * hd:(hh + 1) * hd].astype(BF16)
                vt_ref[hh, hd:, pl.ds(r0, tq)] = ones_row.astype(BF16)
            return carry

        lax.fori_loop(0, s_len // tq, prep, 0)

    lane = lax.broadcasted_iota(jnp.int32, (tq, LANES), 1)
    q = (q_ref[0].astype(F32) * (hd ** -0.5 * LOG2E)).astype(BF16)
    zero = jnp.zeros_like(q)
    qx = []
    for hh in range(2):
        qh = jnp.where(jnp.logical_and(lane >= hh * hd, lane < (hh + 1) * hd), q, zero)
        sel = jnp.where(jnp.logical_and(lane >= 3 * hh, lane < 3 * hh + 3), -1.0, 0.0).astype(BF16)
        qx.append(jnp.concatenate([qh, sel], axis=1))
    m_ref[...] = jnp.full(m_ref.shape, NEG, F32)
    acc_ref[...] = jnp.zeros(acc_ref.shape, F32)

    def scores(j, dst_ref):
        start = pl.multiple_of(j * tq, tq)
        kx = kx_ref[pl.ds(start, tq), :]
        for hh in range(2):
            dst_ref[hh] = _dot_nt(kx, qx[hh])

    def consume(j, src_ref, masked):
        start = pl.multiple_of(j * tq, tq)
        for hh in range(2):
            s = src_ref[hh]
            if masked:
                key = lax.broadcasted_iota(jnp.int32, (tq, tq), 0)
                qry = lax.broadcasted_iota(jnp.int32, (tq, tq), 1)
                s = jnp.where(key <= qry, s, NEG)
            m_old = m_ref[hh:hh + 1, :]
            m_new = jnp.maximum(m_old, jnp.max(s, axis=0, keepdims=True))
            alpha = jnp.exp2(m_old - m_new)
            p = jnp.exp2(s - m_new).astype(BF16)
            acc_ref[hh] = alpha * acc_ref[hh] + _dot(vt_ref[hh, :, pl.ds(start, tq)], p)
            m_ref[hh:hh + 1, :] = m_new

    scores(0, sa_ref)

    def loop_body(j, carry):
        @pl.when(j % 2 == 0)
        def _():
            scores(j + 1, sb_ref)
            consume(j, sa_ref, False)

        @pl.when(j % 2 == 1)
        def _():
            scores(j + 1, sa_ref)
            consume(j, sb_ref, False)

        return carry

    lax.fori_loop(0, qi, loop_body, 0)

    @pl.when(qi % 2 == 0)
    def _():
        consume(qi, sa_ref, True)

    @pl.when(qi % 2 == 1)
    def _():
        consume(qi, sb_ref, True)

    out_t = jnp.concatenate([acc_ref[hh, :hd] / acc_ref[hh, hd:hd + 1] for hh in range(2)], axis=0)
    o_ref[0] = out_t.T.astype(BF16)


def _fox(zq3d, col3d, tq):
    b, s, _ = zq3d.shape
    npair = FOX_HEADS // 2
    kblk = FOX_WIDTH // LANES
    return pl.pallas_call(
        functools.partial(_fox_body, tq=tq, s_len=s),
        grid=(b, npair, s // tq),
        in_specs=[pl.BlockSpec((1, tq, LANES), lambda bi, hp, qi: (bi, qi, hp)),
                  pl.BlockSpec((1, s, LANES), lambda bi, hp, qi: (bi, 0, kblk + hp)),
                  pl.BlockSpec((1, s, LANES), lambda bi, hp, qi: (bi, 0, 2 * kblk + hp)),
                  pl.BlockSpec((1, s, LANES), lambda bi, hp, qi: (bi, 0, 0))],
        out_specs=pl.BlockSpec((1, tq, LANES), lambda bi, hp, qi: (bi, qi, hp)),
        out_shape=jax.ShapeDtypeStruct((b, s, FOX_WIDTH), BF16),
        scratch_shapes=[pltpu.VMEM((s, 2 * LANES), BF16), pltpu.VMEM((2, FOX_VROWS, s), BF16),
                        pltpu.VMEM((2, tq, tq), F32), pltpu.VMEM((2, tq, tq), F32),
                        pltpu.VMEM((2, FOX_VROWS, tq), F32), pltpu.VMEM((2, tq), F32)],
        compiler_params=_cparams(("parallel", "parallel", "arbitrary")),
        name="fox",
    )(zq3d, zq3d, zq3d, col3d)


HALO = 16


def _gdnprep_body(x_ref, halo_ref, cw_ref, oq_ref, ok_ref, ov_ref, ext_ref, *, tp):
    prev = halo_ref[0].astype(F32)
    ext_ref[0:HALO, :] = jnp.where(pl.program_id(1) > 0, prev, 0.0)
    ext_ref[HALO:, :] = x_ref[0].astype(F32)
    acc = cw_ref[GDN_CONV - 1:GDN_CONV, :] * ext_ref[HALO:HALO + tp, :]
    for j in range(GDN_CONV - 1):
        off = HALO - (GDN_CONV - 1) + j
        acc = acc + cw_ref[j:j + 1, :] * ext_ref[off:off + tp, :]
    y = acc * _sigmoid(acc)

    def l2(v):
        return v * lax.rsqrt(jnp.sum(v * v, axis=-1, keepdims=True) + EPS)

    for h in range(GDN_HEADS):
        lo, hi = h * GDN_HEAD_DIM, (h + 1) * GDN_HEAD_DIM
        oq_ref[0, :, lo:hi] = (l2(y[:, lo:hi]) * GDN_HEAD_DIM ** -0.5).astype(BF16)
        ok_ref[0, :, lo:hi] = l2(y[:, GDN_WIDTH + lo:GDN_WIDTH + hi]).astype(BF16)
    ov_ref[0] = y[:, 2 * GDN_WIDTH:].astype(BF16)


def _gdnprep(zg3d, conv_w, tp):
    b, s, c = zg3d.shape
    blk = lambda bi, i: (bi, i, 0)
    return pl.pallas_call(
        functools.partial(_gdnprep_body, tp=tp),
        grid=(b, s // tp),
        in_specs=[pl.BlockSpec((1, tp, c), blk),
                  pl.BlockSpec((1, HALO, c), lambda bi, i: (bi, jnp.maximum(i * (tp // HALO) - 1, 0), 0)),
                  pl.BlockSpec(conv_w.shape, lambda bi, i: (0, 0))],
        out_specs=[pl.BlockSpec((1, tp, GDN_WIDTH), blk)] * 3,
        out_shape=[jax.ShapeDtypeStruct((b, s, GDN_WIDTH), BF16)] * 3,
        scratch_shapes=[pltpu.VMEM((tp + HALO, c), F32)],
        compiler_params=_cparams(("parallel", "parallel")),
        name="gdnprep",
    )(zg3d, zg3d, conv_w)


def _stack_heads(x):
    return jnp.concatenate([x[:, h * GDN_HEAD_DIM:(h + 1) * GDN_HEAD_DIM] for h in range(GDN_HEADS)], axis=0)


def _gdn_body(q_ref, k_ref, v_ref, col_ref, grow_ref, o_ref, state_ref, *bufs, tg):
    C = GDN_CHUNK
    R = GDN_HEADS * C
    dh = GDN_HEAD_DIM
    n_chunks = tg // C
    step = pl.program_id(1)
    buf_sets = (bufs[:len(bufs) // 2], bufs[len(bufs) // 2:])

    @pl.when(step == 0)
    def _():
        for ref in (state_ref,) + tuple(bufs):
            ref[...] = jnp.zeros_like(ref)

    r = lax.broadcasted_iota(jnp.int32, (R, R), 0)
    c = lax.broadcasted_iota(jnp.int32, (R, R), 1)
    same_head = (r // C) == (c // C)
    lower = jnp.logical_and(same_head, r >= c)
    strict = jnp.logical_and(same_head, r > c)
    gc_lane0 = FOX_HEADS
    beta_lane0 = FOX_HEADS + GDN_HEADS

    def advance(ci, rd):
        u_ref, w_ref, intra_ref, qd_ref, kd_ref, gl_ref = rd
        r0 = ci * C
        u, w, intra = u_ref[ci], w_ref[ci], intra_ref[ci]
        q_dec, k_dec = qd_ref[ci], kd_ref[ci]
        v_new = []
        o_state = []
        for h in range(GDN_HEADS):
            sl = slice(h * C, (h + 1) * C)
            st = state_ref[h].astype(BF16)
            v_new.append(u[sl] - _dot(w[sl], st))
            o_state.append(_dot(q_dec[sl], st))
        v_new = jnp.concatenate(v_new, axis=0)
        v_new_b = v_new.astype(BF16)
        o_all = jnp.concatenate(o_state, axis=0) + _dot(intra, v_new_b)
        for h in range(GDN_HEADS):
            sl = slice(h * C, (h + 1) * C)
            state_ref[h] = state_ref[h] * gl_ref[ci, h:h + 1, :] + _dot_tn(k_dec[sl], v_new_b[sl])
        o_ref[0, r0:r0 + C, :] = jnp.concatenate(
            [o_all[h * C:(h + 1) * C] for h in range(GDN_HEADS)], axis=1).astype(BF16)

    chunks = range(n_chunks)

    def prepare_all(wr):
        u_ref, w_ref, intra_ref, qd_ref, kd_ref, gl_ref = wr
        qs, ks, vs, gc_col, beta_col, gl_row, lmat, intra = [], [], [], [], [], [], [], []
        for ci in chunks:
            r0 = ci * C
            qs.append(_stack_heads(q_ref[0, r0:r0 + C, :]).astype(F32))
            ks.append(_stack_heads(k_ref[0, r0:r0 + C, :]).astype(F32))
            vs.append(_stack_heads(v_ref[0, r0:r0 + C, :]).astype(F32))
            col = col_ref[0, r0:r0 + C, :]
            gc_col.append(
                jnp.concatenate([col[:, gc_lane0 + h:gc_lane0 + h + 1] for h in range(GDN_HEADS)], axis=0))
            beta_col.append(
                jnp.concatenate([col[:, beta_lane0 + h:beta_lane0 + h + 1] for h in range(GDN_HEADS)], axis=0))
            gl_row.append(col[C - 1:C, :])
            gc_row = grow_ref[0, ci, :, :]
            decay = jnp.exp(jnp.where(lower, gc_col[ci] - gc_row, NEG))
            ksb = ks[ci].astype(BF16)
            kk = _dot_nt(ksb, ksb)
            qk = _dot_nt(qs[ci].astype(BF16), ksb)
            lmat.append(jnp.where(strict, kk * decay * beta_col[ci], 0.0))
            intra.append((qk * decay).astype(BF16))
        n_mat = [-l for l in lmat]
        pw = lmat
        for _ in range(5):
            pwb = [p.astype(BF16) for p in pw]
            pw = [_dot(p, p) for p in pwb]
            n_mat = [n + p + _dot(n.astype(BF16), p.astype(BF16)) for n, p in zip(n_mat, pw)]
        for ci in chunks:
            e_gc = jnp.exp(gc_col[ci])
            rhs = jnp.concatenate([vs[ci] * beta_col[ci], ks[ci] * (beta_col[ci] * e_gc)], axis=1)
            sol = rhs + _dot(n_mat[ci].astype(BF16), rhs.astype(BF16))
            gl_col = jnp.concatenate(
                [jnp.broadcast_to(gl_row[ci][:, gc_lane0 + h:gc_lane0 + h + 1], (C, 1))
                 for h in range(GDN_HEADS)], axis=0)
            u_ref[ci] = sol[:, :dh]
            w_ref[ci] = sol[:, dh:].astype(BF16)
            intra_ref[ci] = intra[ci]
            qd_ref[ci] = (qs[ci] * e_gc).astype(BF16)
            kd_ref[ci] = (ks[ci] * jnp.exp(gl_col - gc_col[ci])).astype(BF16)
            for h in range(GDN_HEADS):
                gl_ref[ci, h:h + 1, :] = jnp.broadcast_to(
                    jnp.exp(gl_row[ci][:, gc_lane0 + h:gc_lane0 + h + 1]), (1, LANES))

    def run(rd, wr):
        for ci in chunks:
            advance(ci, rd)
        prepare_all(wr)

    @pl.when(step % 2 == 0)
    def _():
        run(buf_sets[0], buf_sets[1])

    @pl.when(step % 2 == 1)
    def _():
        run(buf_sets[1], buf_sets[0])


def _gdn(qn, kn, vv, col3d, gcrow, tg):
    b, s, _ = qn.shape
    n = s // tg
    nck = tg // GDN_CHUNK
    rows = GDN_HEADS * GDN_CHUNK
    dh = GDN_HEAD_DIM
    blk_in = lambda bi, i: (bi, jnp.minimum(i, n - 1), 0)
    blk_out = lambda bi, i: (bi, jnp.maximum(i - 1, 0), 0)
    return pl.pallas_call(
        functools.partial(_gdn_body, tg=tg),
        grid=(b, n + 1),
        in_specs=[pl.BlockSpec((1, tg, GDN_WIDTH), blk_in)] * 3
        + [pl.BlockSpec((1, tg, LANES), blk_in),
           pl.BlockSpec((1, nck, 1, rows), lambda bi, i: (bi, jnp.minimum(i, n - 1), 0, 0))],
        out_specs=pl.BlockSpec((1, tg, GDN_WIDTH), blk_out),
        out_shape=jax.ShapeDtypeStruct((b, s, GDN_WIDTH), BF16),
        scratch_shapes=[pltpu.VMEM((GDN_HEADS, dh, dh), F32)] + 2 * [
            pltpu.VMEM((nck, rows, dh), F32), pltpu.VMEM((nck, rows, dh), BF16),
            pltpu.VMEM((nck, rows, rows), BF16), pltpu.VMEM((nck, rows, dh), BF16),
            pltpu.VMEM((nck, rows, dh), BF16), pltpu.VMEM((nck, 8, LANES), F32)],
        compiler_params=_cparams(("parallel", "arbitrary")),
        name="gdn",
    )(qn, kn, vv, col3d, gcrow)


MOE_SLOT_TILE = 512


def _rows_to_tiles(ref, v):
    m, width = v.shape
    n = width // LANES
    for s in range(n):
        ref[pl.ds(s, m, stride=n), :] = v[:, s * LANES:(s + 1) * LANES]


def _tiles_to_rows(ref, m):
    n = ref.shape[0] // m
    return jnp.concatenate([ref[pl.ds(s, m, stride=n), :] for s in range(n)], axis=1)


RT_E1, RT_E2, RT_RANK1, RT_RANK2, RT_G1, RT_G2 = range(6)


def _combine_body(yf_ref, og_ref, dz_ref, gf_ref, gd_ref, x_ref, wof_ref, wod_ref, wout_ref,
                  gon_ref, gffn_ref, wrh_ref, wrl_ref, br_ref, x1_ref, t_ref, rt_ref, cnt_ref, carry_ref,
                  *, tc):
    @pl.when(pl.program_id(0) == 0)
    def _():
        carry_ref[...] = jnp.zeros_like(carry_ref)

    on = []
    for h in range(GDN_HEADS):
        sl = slice(h * GDN_HEAD_DIM, (h + 1) * GDN_HEAD_DIM)
        dz = dz_ref[:, sl].astype(F32)
        on.append(_rms(og_ref[:, sl].astype(F32), gon_ref[...]) * (dz * _sigmoid(dz)))
    on = jnp.concatenate(on, axis=1).astype(BF16)
    y_fox = _dot(yf_ref[...], wof_ref[...])
    y_delta = _dot(on, wod_ref[...])
    merged = _sigmoid(gf_ref[...].astype(F32)) * y_fox + _sigmoid(gd_ref[...].astype(F32)) * y_delta
    x1 = x_ref[...] + _dot(merged.astype(BF16), wout_ref[...])
    x1_ref[...] = x1
    t32 = _rms(x1, gffn_ref[...])
    th = t32.astype(BF16)
    tl = (t32 - th.astype(F32)).astype(BF16)
    _rows_to_tiles(t_ref, t32)
    logits = _dot(th, wrh_ref[...]) + _dot(tl, wrh_ref[...]) + _dot(th, wrl_ref[...]) + br_ref[...]
    lane = lax.broadcasted_iota(jnp.int32, logits.shape, 1)
    gl = jnp.where(lane < N_GROUPS, logits, NEG)
    gmax = jnp.max(gl, axis=-1, keepdims=True)
    g_sel = jnp.min(jnp.where(gl == gmax, lane, LANES), axis=-1, keepdims=True)
    p_sel = 1.0 / jnp.sum(jnp.exp(gl - gmax), axis=-1, keepdims=True)
    lo = EXPERT_LANE0 + EXPERTS_PER_GROUP * g_sel
    in_grp = jnp.logical_and(lane >= lo, lane < lo + EXPERTS_PER_GROUP)
    el = jnp.where(in_grp, logits, NEG)
    emax = jnp.max(el, axis=-1, keepdims=True)
    ee = jnp.where(in_grp, jnp.exp(el - emax), 0.0)
    pe = ee / jnp.sum(ee, axis=-1, keepdims=True)
    pe = jnp.where(in_grp, pe, -1.0)
    p1 = jnp.max(pe, axis=-1, keepdims=True)
    i1 = jnp.min(jnp.where(pe == p1, lane, LANES), axis=-1, keepdims=True)
    pe2 = jnp.where(lane == i1, -1.0, pe)
    p2 = jnp.max(pe2, axis=-1, keepdims=True)
    i2 = jnp.min(jnp.where(pe2 == p2, lane, LANES), axis=-1, keepdims=True)
    den = p1 + p2
    hit1 = lane == i1
    hit2 = lane == i2
    assign = jnp.where(jnp.logical_or(hit1, hit2), 1.0, 0.0)
    r = lax.broadcasted_iota(jnp.int32, (tc, tc), 0)
    c = lax.broadcasted_iota(jnp.int32, (tc, tc), 1)
    before = jnp.where(r > c, 1.0, 0.0).astype(BF16)
    prefix = _dot(before, assign.astype(BF16)) + carry_ref[...]
    rank1 = jnp.sum(jnp.where(hit1, prefix, 0.0), axis=-1, keepdims=True)
    rank2 = jnp.sum(jnp.where(hit2, prefix, 0.0), axis=-1, keepdims=True)
    carry_ref[...] = prefix[tc - 1:tc, :] + assign[tc - 1:tc, :]
    cnt_ref[...] = carry_ref[...]
    cols = ((i1 - EXPERT_LANE0).astype(F32), (i2 - EXPERT_LANE0).astype(F32), rank1, rank2,
            p_sel * (p1 / den), p_sel * (p2 / den))
    rt = jnp.zeros(logits.shape, F32)
    for k, v in enumerate(cols):
        rt = jnp.where(lane == k, v, rt)
    rt_ref[...] = rt


def _combine(yf, og, dz, gf, gd, x2d, wof, wod, wout, g_on, g_ffn, wrh, wrl, br, tc):
    t, d = x2d.shape
    row = lambda i: (i, 0)
    const = lambda i: (0, 0)
    acts = (yf, og, dz, gf, gd, x2d)
    consts = (wof, wod, wout, g_on, g_ffn, wrh, wrl, br)
    return pl.pallas_call(
        functools.partial(_combine_body, tc=tc),
        grid=(t // tc,),
        in_specs=[pl.BlockSpec((tc, a.shape[1]), row) for a in acts]
        + [pl.BlockSpec(c.shape, const) for c in consts],
        out_specs=[pl.BlockSpec((tc, d), row), pl.BlockSpec((tc * (d // LANES), LANES), row),
                   pl.BlockSpec((tc, LANES), row), pl.BlockSpec((1, LANES), const)],
        out_shape=[jax.ShapeDtypeStruct((t, d), F32), jax.ShapeDtypeStruct((t * (d // LANES), LANES), F32),
                   jax.ShapeDtypeStruct((t, LANES), F32), jax.ShapeDtypeStruct((1, LANES), F32)],
        scratch_shapes=[pltpu.VMEM((1, LANES), F32)],
        compiler_params=_cparams(("arbitrary",)),
        name="combine",
    )(*acts, *consts)


def _slots_body(rt_ref, cnt_ref, o_ref, *, tmx):
    lane1 = lax.broadcasted_iota(jnp.int32, (1, LANES), 1)
    is_expert = jnp.logical_and(lane1 >= EXPERT_LANE0, lane1 < EXPERT_LANE0 + N_EXPERTS)
    padded = jnp.where(is_expert, jnp.floor((cnt_ref[...] + (tmx - 1)) / tmx) * tmx, 0.0)
    r = lax.broadcasted_iota(jnp.int32, (LANES, LANES), 0)
    c = lax.broadcasted_iota(jnp.int32, (LANES, LANES), 1)
    before = jnp.where(r < c, 1.0, 0.0).astype(BF16)
    hi, mid, lo = _split3(jnp.broadcast_to(padded, (8, LANES)))
    offs = (_dot(hi, before) + _dot(mid, before) + _dot(lo, before))[0:1, :]
    rt = rt_ref[...]
    lane = lax.broadcasted_iota(jnp.int32, rt.shape, 1)
    expert_of_lane = (lane - EXPERT_LANE0).astype(F32)
    out = jnp.zeros(rt.shape, F32)
    for k, (ce, cr) in enumerate(((RT_E1, RT_RANK1), (RT_E2, RT_RANK2))):
        start = jnp.sum(jnp.where(expert_of_lane == rt[:, ce:ce + 1], offs, 0.0), axis=-1, keepdims=True)
        out = jnp.where(lane == k, start + rt[:, cr:cr + 1], out)
    o_ref[...] = out.T[:8, :].astype(jnp.int32)


def _slots(rt, cnt, tmx, ts):
    t = rt.shape[0]
    return pl.pallas_call(
        functools.partial(_slots_body, tmx=tmx),
        grid=(t // ts,),
        in_specs=[pl.BlockSpec((ts, LANES), lambda i: (i, 0)), pl.BlockSpec((1, LANES), lambda i: (0, 0))],
        out_specs=pl.BlockSpec((8, ts), lambda i: (0, i)),
        out_shape=jax.ShapeDtypeStruct((8, t), jnp.int32),
        compiler_params=_cparams(("parallel",)),
        name="slots",
    )(rt, cnt)


def _move_rows_body(pos_ref, src_ref, dst_ref, sem, *, td, nsub, scatter):
    i = pl.program_id(0)
    n = pl.num_programs(0)
    base = i * td

    group = 8

    def issue(g, carry):
        r0 = g * group
        slots = [[pos_ref[0, 0, k * td + r0 + u] for k in range(2)] for u in range(group)]
        for u in range(group):
            tok = pl.ds(pl.multiple_of((base + r0 + u) * nsub, nsub), nsub)
            for k in range(2):
                slot = pl.ds(pl.multiple_of(slots[u][k] * nsub, nsub), nsub)
                if scatter:
                    cp = pltpu.make_async_copy(src_ref.at[tok, :], dst_ref.at[slot, :], sem)
                else:
                    cp = pltpu.make_async_copy(src_ref.at[slot, :], dst_ref.at[k, tok, :], sem)
                cp.start(priority=k)
        return carry

    lax.fori_loop(0, td // group, issue, 0)

    def wait_tile():
        rows = pl.ds(0, td * nsub)
        dst = dst_ref.at[rows, :] if scatter else dst_ref.at[0, rows, :]
        for _ in range(2):
            pltpu.make_async_copy(src_ref.at[rows, :], dst, sem).wait()

    @pl.when(i > 0)
    def _():
        wait_tile()

    @pl.when(i == n - 1)
    def _():
        wait_tile()


def _dispatch_body(zflag_ref, pos_ref, t_ref, xs_ref, zbuf, sem, zsem, *, td, tmx, n_tiles, nsub):
    @pl.when(pl.program_id(0) == 0)
    def _():
        zbuf[...] = jnp.zeros_like(zbuf)
        rows = tmx * nsub

        def zero_tile(k, carry):
            @pl.when(zflag_ref[k] != 0)
            def _():
                cp = pltpu.make_async_copy(zbuf, xs_ref.at[pl.ds(pl.multiple_of(k * rows, rows), rows), :], zsem)
                cp.start()
                cp.wait()

            return carry

        lax.fori_loop(0, n_tiles, zero_tile, 0)

    _move_rows_body(pos_ref, t_ref, xs_ref, sem, td=td, nsub=nsub, scatter=True)


def _dispatch(zflag, t_tiles, pos12, n_slots, td, tmx, nsub):
    t = t_tiles.shape[0] // nsub
    n_tiles = n_slots // tmx
    grid_spec = pltpu.PrefetchScalarGridSpec(
        num_scalar_prefetch=1,
        grid=(t // td,),
        in_specs=[pl.BlockSpec((1, 1, 2 * td), lambda i, zf: (i, 0, 0), memory_space=pltpu.SMEM),
                  pl.BlockSpec(memory_space=pl.ANY)],
        out_specs=pl.BlockSpec(memory_space=pl.ANY),
        scratch_shapes=[pltpu.VMEM((tmx * nsub, LANES), F32), pltpu.SemaphoreType.DMA(()),
                        pltpu.SemaphoreType.DMA(())],
    )
    return pl.pallas_call(
        functools.partial(_dispatch_body, td=td, tmx=tmx, n_tiles=n_tiles, nsub=nsub),
        grid_spec=grid_spec,
        out_shape=jax.ShapeDtypeStruct((n_slots * nsub, LANES), F32),
        compiler_params=_cparams(("arbitrary",)),
        name="dispatch",
    )(zflag, pos12, t_tiles)


def _undispatch(ys, pos12, t, td, nsub):
    return pl.pallas_call(
        functools.partial(_move_rows_body, td=td, nsub=nsub, scatter=False),
        grid=(t // td,),
        in_specs=[pl.BlockSpec((1, 1, 2 * td), lambda i: (i, 0, 0), memory_space=pltpu.SMEM),
                  pl.BlockSpec(memory_space=pl.ANY)],
        out_specs=pl.BlockSpec(memory_space=pl.ANY),
        out_shape=jax.ShapeDtypeStruct((2, t * nsub, LANES), F32),
        scratch_shapes=[pltpu.SemaphoreType.DMA(())],
        compiler_params=_cparams(("arbitrary",)),
        name="undispatch",
    )(pos12, ys)


def _experts_body(te_ref, nused_ref, xs_ref, wgu_ref, wd_ref, ys_ref, *, tmx):
    @pl.when(pl.program_id(0) < nused_ref[0])
    def _():
        hgu = _dot(_tiles_to_rows(xs_ref, tmx).astype(BF16), wgu_ref[0])
        a = hgu[:, :EXPERT_FF]
        hid = a * _sigmoid(a) * hgu[:, EXPERT_FF:]
        _rows_to_tiles(ys_ref, _dot(hid.astype(BF16), wd_ref[0]))

    @pl.when(pl.program_id(0) >= nused_ref[0])
    def _():
        ys_ref[...] = jnp.zeros_like(ys_ref)


def _experts(tile_expert, n_used, xs, wgu, wd, tmx):
    d = wgu.shape[1]
    nsub = d // LANES
    n_slots = xs.shape[0] // nsub
    grid_spec = pltpu.PrefetchScalarGridSpec(
        num_scalar_prefetch=2,
        grid=(n_slots // tmx,),
        in_specs=[pl.BlockSpec((tmx * nsub, LANES), lambda i, te, nu: (i, 0)),
                  pl.BlockSpec((1, d, 2 * EXPERT_FF), lambda i, te, nu: (te[i], 0, 0)),
                  pl.BlockSpec((1, EXPERT_FF, d), lambda i, te, nu: (te[i], 0, 0))],
        out_specs=pl.BlockSpec((tmx * nsub, LANES), lambda i, te, nu: (i, 0)),
    )
    return pl.pallas_call(
        functools.partial(_experts_body, tmx=tmx),
        grid_spec=grid_spec,
        out_shape=jax.ShapeDtypeStruct(xs.shape, F32),
        compiler_params=_cparams(("arbitrary",)),
        name="experts",
    )(tile_expert, n_used, xs, wgu, wd)


def _final_body(x_ref, rt_ref, y_ref, p_ref, gple_ref, wpg_ref, wpp_ref, gfin_ref, o_ref, *, tf):
    rt = rt_ref[...]
    g1 = rt[:, RT_G1:RT_G1 + 1]
    g2 = rt[:, RT_G2:RT_G2 + 1]
    x = x_ref[...] + g1 * _tiles_to_rows(y_ref.at[0], tf) + g2 * _tiles_to_rows(y_ref.at[1], tf)
    r = _rms(x, gple_ref[...]).astype(BF16)
    ple_gate = _sigmoid(_dot(r, wpg_ref[...]))
    proj = _dot(p_ref[...].astype(BF16), wpp_ref[...])
    o_ref[...] = _rms(x + ple_gate * proj, gfin_ref[...])


def _final(x1, rt, y12, p2d, g_ple, wpg, wpp, g_final, tf):
    t, d = x1.shape
    nsub = d // LANES
    row = lambda i: (i, 0)
    const = lambda i: (0, 0)
    return pl.pallas_call(
        functools.partial(_final_body, tf=tf),
        grid=(t // tf,),
        in_specs=[pl.BlockSpec((tf, d), row), pl.BlockSpec((tf, LANES), row),
                  pl.BlockSpec((2, tf * nsub, LANES), lambda i: (0, i, 0)),
                  pl.BlockSpec((tf, p2d.shape[1]), row),
                  pl.BlockSpec((1, d), const), pl.BlockSpec(wpg.shape, const),
                  pl.BlockSpec(wpp.shape, const), pl.BlockSpec((1, d), const)],
        out_specs=pl.BlockSpec((tf, d), row),
        out_shape=jax.ShapeDtypeStruct((t, d), F32),
        compiler_params=_cparams(("parallel",)),
        name="final",
    )(x1, rt, y12, p2d, g_ple, wpg, wpp, g_final)


def _pad_lanes(v, lane0):
    return jnp.zeros((1, LANES), F32).at[0, lane0:lane0 + v.shape[0]].set(v.astype(F32))


def _tile(n, pref):
    return pref if n % pref == 0 else n


def _layer(x, p_l, g_mix, w_in, b_forget, conv_w, a_log, dt_bias, g_onorm, w_o_fox, w_o_delta, w_out,
           g_ffn, w_group, b_group, w_router, b_router, w_gate, w_up, w_down, g_ple, w_ple_gate, w_ple_proj,
           g_post, apply_post):
    b, s, d = x.shape
    t = b * s
    x2d = x.reshape(t, d)

    o_ff = 3 * FOX_WIDTH
    o_qkv = o_ff + FOX_HEADS
    o_da = o_qkv + 3 * GDN_WIDTH
    o_db = o_da + GDN_HEADS
    o_dz = o_db + GDN_HEADS
    o_gf = o_dz + GDN_WIDTH
    o_gd = o_gf + d
    w_small = jnp.concatenate(
        [w_in[:, o_ff:o_qkv], w_in[:, o_da:o_db], w_in[:, o_db:o_dz],
         jnp.zeros((d, LANES - FOX_HEADS - 2 * GDN_HEADS), w_in.dtype)], axis=1)
    wq = w_in[:, :o_ff].astype(BF16)
    wg = w_in[:, o_qkv:o_da].astype(BF16)
    wzs = jnp.concatenate([w_in[:, o_dz:o_gf], w_small], axis=1).astype(BF16)
    wgf = w_in[:, o_gf:o_gd].astype(BF16)
    wgd = w_in[:, o_gd:].astype(BF16)

    zq, zg, dz, small, gf, gd = _inproj(x2d, g_mix.reshape(1, d), wq, wg, wzs, wgf, wgd, _tile(t, 512))

    par = jnp.concatenate(
        [_pad_lanes(b_forget, 0) + _pad_lanes(dt_bias, FOX_HEADS), _pad_lanes(a_log, FOX_HEADS),
         jnp.zeros((6, LANES), F32)], axis=0)
    col = _gates(small.reshape(b, s, LANES), par, _tile(s, 256))

    nc = s // GDN_CHUNK
    gc = col[:, :, FOX_HEADS:FOX_HEADS + GDN_HEADS].reshape(b, nc, GDN_CHUNK, GDN_HEADS)
    gcrow = jnp.transpose(gc, (0, 1, 3, 2)).reshape(b, nc, 1, GDN_HEADS * GDN_CHUNK)

    y_fox = _fox(zq.reshape(b, s, 3 * FOX_WIDTH), col, _tile(s, 512))

    qn, kn, vv = _gdnprep(zg.reshape(b, s, 3 * GDN_WIDTH), conv_w.astype(F32), _tile(s, 512))
    o_gdn = _gdn(qn, kn, vv, col, gcrow, _tile(s, 512))

    w_r = jnp.concatenate([w_group, w_router,
                           jnp.zeros((d, LANES - N_GROUPS - N_EXPERTS), w_group.dtype)], axis=1).astype(F32)
    wrh = w_r.astype(BF16)
    wrl = (w_r - wrh.astype(F32)).astype(BF16)
    br = _pad_lanes(b_group, 0) + _pad_lanes(b_router, EXPERT_LANE0)
    x1, tpk, rt, cnt = _combine(
        y_fox.reshape(t, FOX_WIDTH), o_gdn.reshape(t, GDN_WIDTH), dz, gf, gd, x2d,
        w_o_fox.astype(BF16), w_o_delta.astype(BF16), w_out.astype(BF16),
        g_onorm.reshape(1, GDN_HEAD_DIM).astype(F32), g_ffn.reshape(1, d).astype(F32), wrh, wrl, br,
        _tile(t, 512))

    tmx = MOE_SLOT_TILE
    n_slots = 2 * t + N_EXPERTS * tmx
    counts = cnt[0, EXPERT_LANE0:EXPERT_LANE0 + N_EXPERTS].astype(jnp.int32)
    padded = (counts + tmx - 1) // tmx * tmx
    ends = jnp.cumsum(padded)
    pos = _slots(rt, cnt, tmx, _tile(t, 512))
    pos1, pos2 = pos[0], pos[1]
    tile_start = jnp.arange(n_slots // tmx, dtype=jnp.int32) * tmx
    tile_expert = jnp.minimum(jnp.sum(tile_start[:, None] >= ends[None, :], axis=1), N_EXPERTS - 1).astype(jnp.int32)
    n_used = (ends[-1:] // tmx).astype(jnp.int32)
    last_of_expert = jnp.any(jnp.logical_and(tile_start[:, None] + tmx == ends[None, :], padded[None, :] > 0), axis=1)
    zflag = jnp.logical_or(last_of_expert, tile_start >= ends[-1]).astype(jnp.int32)

    def tiled_pos(tile):
        n = t // tile
        return jnp.concatenate([pos1.reshape(n, 1, tile), pos2.reshape(n, 1, tile)], axis=2)

    td = _tile(t, 512)
    nsub = d // LANES
    pos12 = tiled_pos(td)
    xs = _dispatch(zflag, tpk, pos12, n_slots, td, tmx, nsub)
    wgu = jnp.concatenate([w_gate, w_up], axis=2).astype(BF16)
    ys = _experts(tile_expert, n_used, xs, wgu, w_down.astype(BF16), tmx)
    y12 = _undispatch(ys, pos12, t, td, nsub)

    out = _final(x1, rt, y12, p_l.reshape(t, -1), g_ple.reshape(1, d).astype(F32),
                 w_ple_gate.astype(BF16), w_ple_proj.astype(BF16), g_post.reshape(1, d).astype(F32), _tile(t, 512))
    return out.reshape(b, s, d)


def kernel(x, p, g_mix, w_in, b_forget, conv_w, a_log, dt_bias, g_onorm, w_o_fox, w_o_delta, w_out,
           g_ffn, w_group, b_group, w_router, b_router, w_gate, w_up, w_down, g_ple, w_ple_gate, w_ple_proj,
           g_final):
    depth = p.shape[0]
    assert depth == 1, "the final rmsnorm is fused into the last layer's epilogue; depth 1 only"
    i = 0
    return _layer(x, p[i], g_mix[i], w_in[i], b_forget[i], conv_w[i], a_log[i], dt_bias[i], g_onorm[i],
                  w_o_fox[i], w_o_delta[i], w_out[i], g_ffn[i], w_group[i], b_group[i], w_router[i],
                  b_router[i], w_gate[i], w_up[i], w_down[i], g_ple[i], w_ple_gate[i], w_ple_proj[i],
                  g_final, True)
```

```python
import functools

import jax
import jax.numpy as jnp
from jax import lax
from jax.experimental import pallas as pl
from jax.experimental.pallas import tpu as pltpu

F32 = jnp.float32
BF16 = jnp.bfloat16
EPS = 1e-6
NEG = -1e30

FOX_HEADS = 8
FOX_HEAD_DIM = 64
FOX_WIDTH = FOX_HEADS * FOX_HEAD_DIM
GDN_HEADS = 4
GDN_HEAD_DIM = 128
GDN_CONV = 4
GDN_CHUNK = 64
GDN_WIDTH = GDN_HEADS * GDN_HEAD_DIM
N_GROUPS = 4
EXPERTS_PER_GROUP = 8
N_EXPERTS = N_GROUPS * EXPERTS_PER_GROUP
EXPERT_FF = 256
LANES = 128
EXPERT_LANE0 = N_GROUPS

VMEM_LIMIT_BYTES = 56 * 1024 * 1024


def _cparams(sem):
    return pltpu.CompilerParams(dimension_semantics=sem, vmem_limit_bytes=VMEM_LIMIT_BYTES)


def _rms(x, g):
    return x * lax.rsqrt(jnp.mean(x * x, axis=-1, keepdims=True) + EPS) * g


def _sigmoid(x):
    return 1.0 / (1.0 + jnp.exp(-x))


def _dot(a, b):
    return jnp.dot(a, b, preferred_element_type=F32)


def _dot_nt(a, b):
    return lax.dot_general(a, b, (((1,), (1,)), ((), ())), preferred_element_type=F32)


def _dot_tn(a, b):
    return lax.dot_general(a, b, (((0,), (0,)), ((), ())), preferred_element_type=F32)


def _split3(v):
    hi = v.astype(BF16)
    r = v - hi.astype(F32)
    mid = r.astype(BF16)
    lo = (r - mid.astype(F32)).astype(BF16)
    return hi, mid, lo


def _dot_exact_lhs01(mat01, v):
    hi, mid, lo = _split3(v)
    return _dot(mat01, hi) + _dot(mat01, mid) + _dot(mat01, lo)


def _inproj_body(x_ref, g_ref, wq_ref, wg_ref, wzs_ref, wgf_ref, wgd_ref,
                 oq_ref, og_ref, oz_ref, osm_ref, ogf_ref, ogd_ref):
    hb = _rms(x_ref[...], g_ref[...]).astype(BF16)
    oq_ref[...] = _dot(hb, wq_ref[...]).astype(BF16)
    og_ref[...] = _dot(hb, wg_ref[...]).astype(BF16)
    zs = _dot(hb, wzs_ref[...])
    oz_ref[...] = zs[:, :GDN_WIDTH].astype(BF16)
    osm_ref[...] = zs[:, GDN_WIDTH:]
    ogf_ref[...] = _dot(hb, wgf_ref[...]).astype(BF16)
    ogd_ref[...] = _dot(hb, wgd_ref[...]).astype(BF16)


def _inproj(x2d, g_mix, wq, wg, wzs, wgf, wgd, tm):
    t, d = x2d.shape
    row = lambda i: (i, 0)
    const = lambda i: (0, 0)
    widths = (3 * FOX_WIDTH, 3 * GDN_WIDTH, GDN_WIDTH, LANES, d, d)
    dtypes = (BF16, BF16, BF16, F32, BF16, BF16)
    return pl.pallas_call(
        _inproj_body,
        grid=(t // tm,),
        in_specs=[pl.BlockSpec((tm, d), row), pl.BlockSpec((1, d), const)]
        + [pl.BlockSpec(w.shape, const) for w in (wq, wg, wzs, wgf, wgd)],
        out_specs=[pl.BlockSpec((tm, n), row) for n in widths],
        out_shape=[jax.ShapeDtypeStruct((t, n), dt) for n, dt in zip(widths, dtypes)],
        compiler_params=_cparams(("parallel",)),
        name="inproj",
    )(x2d, g_mix, wq, wg, wzs, wgf, wgd)


def _gates_body(sm_ref, par_ref, o_ref, carry_ref, *, tp):
    @pl.when(pl.program_id(1) == 0)
    def _():
        carry_ref[...] = jnp.zeros_like(carry_ref)

    sm = sm_ref[0]
    lane = lax.broadcasted_iota(jnp.int32, sm.shape, 1)
    z = sm + par_ref[0:1, :]
    soft = jnp.log1p(jnp.exp(-jnp.abs(z)))
    softplus = jnp.maximum(z, 0.0) + soft
    logf = jnp.minimum(z, 0.0) - soft
    g = -jnp.exp(par_ref[1:2, :]) * softplus
    beta = _sigmoid(sm)
    is_f = lane < FOX_HEADS
    is_g = jnp.logical_and(lane >= FOX_HEADS, lane < FOX_HEADS + GDN_HEADS)
    is_b = jnp.logical_and(lane >= FOX_HEADS + GDN_HEADS, lane < FOX_HEADS + 2 * GDN_HEADS)
    val = jnp.where(is_f, logf, jnp.where(is_g, g, 0.0))
    r = lax.broadcasted_iota(jnp.int32, (tp, tp), 0)
    c = lax.broadcasted_iota(jnp.int32, (tp, tp), 1)
    lower = r >= c
    tri = jnp.where(lower, 1.0, 0.0).astype(BF16)
    same_chunk = (r // GDN_CHUNK) == (c // GDN_CHUNK)
    tri_chunk = jnp.where(jnp.logical_and(lower, same_chunk), 1.0, 0.0).astype(BF16)
    cum = _dot_exact_lhs01(tri, val) + carry_ref[...]
    gcs = _dot_exact_lhs01(tri_chunk, val)
    carry_ref[...] = cum[tp - 1:tp, :]
    o_ref[0] = jnp.where(is_f, cum, jnp.where(is_g, gcs, jnp.where(is_b, beta, 0.0)))


def _gates(small3d, par, tp):
    b, s, _ = small3d.shape
    return pl.pallas_call(
        functools.partial(_gates_body, tp=tp),
        grid=(b, s // tp),
        in_specs=[pl.BlockSpec((1, tp, LANES), lambda bi, i: (bi, i, 0)),
                  pl.BlockSpec((8, LANES), lambda bi, i: (0, 0))],
        out_specs=pl.BlockSpec((1, tp, LANES), lambda bi, i: (bi, i, 0)),
        out_shape=jax.ShapeDtypeStruct((b, s, LANES), F32),
        scratch_shapes=[pltpu.VMEM((1, LANES), F32)],
        compiler_params=_cparams(("parallel", "arbitrary")),
        name="gates",
    )(small3d, par)


LOG2E = 1.4426950408889634


FOX_VROWS = FOX_HEAD_DIM + 16


def _split3_f32(v):
    hi = v.astype(BF16).astype(F32)
    r = v - hi
    mid = r.astype(BF16).astype(F32)
    lo = (r - mid).astype(BF16).astype(F32)
    return hi, mid, lo


def _fox_body(q_ref, k_ref, v_ref, col_ref, o_ref, kx_ref, vt_ref, sa_ref, sb_ref, acc_ref, m_ref,
              *, tq, s_len):
    hp = pl.program_id(1)
    qi = pl.program_id(2)
    hd = FOX_HEAD_DIM

    @pl.when(qi == 0)
    def _():
        def prep(ci, carry):
            r0 = pl.multiple_of(ci * tq, tq)
            kx_ref[pl.ds(r0, tq), :LANES] = k_ref[0, pl.ds(r0, tq), :]
            col = col_ref[0, pl.ds(r0, tq), :]
            lane = lax.broadcasted_iota(jnp.int32, col.shape, 1)
            bias = jnp.zeros(col.shape, F32)
            for hh in range(2):
                ck = jnp.sum(jnp.where(lane == 2 * hp + hh, col, 0.0), axis=-1, keepdims=True) * LOG2E
                for k, piece in enumerate(_split3_f32(ck)):
                    bias = jnp.where(lane == 3 * hh + k, piece, bias)
            kx_ref[pl.ds(r0, tq), LANES:] = bias.astype(BF16)
            vt = v_ref[0, pl.ds(r0, tq), :].astype(F32).T
            ones_row = jnp.where(lax.broadcasted_iota(jnp.int32, (FOX_VROWS - hd, tq), 0) == 0, 1.0, 0.0)
            for hh in range(2):
                vt_ref[hh, :hd, pl.ds(r0, tq)] = vt[hh * hd:(hh + 1) * hd].astype(BF16)
                vt_ref[hh, hd:, pl.ds(r0, tq)] = ones_row.astype(BF16)
            return carry

        lax.fori_loop(0, s_len // tq, prep, 0)

    lane = lax.broadcasted_iota(jnp.int32, (tq, LANES), 1)
    q = (q_ref[0].astype(F32) * (hd ** -0.5 * LOG2E)).astype(BF16)
    zero = jnp.zeros_like(q)
    qx = []
    for hh in range(2):
        qh = jnp.where(jnp.logical_and(lane >= hh * hd, lane < (hh + 1) * hd), q, zero)
        sel = jnp.where(jnp.logical_and(lane >= 3 * hh, lane < 3 * hh + 3), -1.0, 0.0).astype(BF16)
        qx.append(jnp.concatenate([qh, sel], axis=1))
    m_ref[...] = jnp.full(m_ref.shape, NEG, F32)
    acc_ref[...] = jnp.zeros(acc_ref.shape, F32)

    def scores(j, dst_ref):
        start = pl.multiple_of(j * tq, tq)
        kx = kx_ref[pl.ds(start, tq), :]
        for hh in range(2):
            dst_ref[hh] = _dot_nt(kx, qx[hh])

    def consume(j, src_ref, masked):
        start = pl.multiple_of(j * tq, tq)
        for hh in range(2):
            s = src_ref[hh]
            if masked:
                key = lax.broadcasted_iota(jnp.int32, (tq, tq), 0)
                qry = lax.broadcasted_iota(jnp.int32, (tq, tq), 1)
                s = jnp.where(key <= qry, s, NEG)
            m_old = m_ref[hh:hh + 1, :]
            m_new = jnp.maximum(m_old, jnp.max(s, axis=0, keepdims=True))
            alpha = jnp.exp2(m_old - m_new)
            p = jnp.exp2(s - m_new).astype(BF16)
            acc_ref[hh] = alpha * acc_ref[hh] + _dot(vt_ref[hh, :, pl.ds(start, tq)], p)
            m_ref[hh:hh + 1, :] = m_new

    scores(0, sa_ref)

    def loop_body(j, carry):
        @pl.when(j % 2 == 0)
        def _():
            scores(j + 1, sb_ref)
            consume(j, sa_ref, False)

        @pl.when(j % 2 == 1)
        def _():
            scores(j + 1, sa_ref)
            consume(j, sb_ref, False)

        return carry

    lax.fori_loop(0, qi, loop_body, 0)

    @pl.when(qi % 2 == 0)
    def _():
        consume(qi, sa_ref, True)

    @pl.when(qi % 2 == 1)
    def _():
        consume(qi, sb_ref, True)

    out_t = jnp.concatenate([acc_ref[hh, :hd] / acc_ref[hh, hd:hd + 1] for hh in range(2)], axis=0)
    o_ref[0] = out_t.T.astype(BF16)


def _fox(zq3d, col3d, tq):
    b, s, _ = zq3d.shape
    npair = FOX_HEADS // 2
    kblk = FOX_WIDTH // LANES
    return pl.pallas_call(
        functools.partial(_fox_body, tq=tq, s_len=s),
        grid=(b, npair, s // tq),
        in_specs=[pl.BlockSpec((1, tq, LANES), lambda bi, hp, qi: (bi, qi, hp)),
                  pl.BlockSpec((1, s, LANES), lambda bi, hp, qi: (bi, 0, kblk + hp)),
                  pl.BlockSpec((1, s, LANES), lambda bi, hp, qi: (bi, 0, 2 * kblk + hp)),
                  pl.BlockSpec((1, s, LANES), lambda bi, hp, qi: (bi, 0, 0))],
        out_specs=pl.BlockSpec((1, tq, LANES), lambda bi, hp, qi: (bi, qi, hp)),
        out_shape=jax.ShapeDtypeStruct((b, s, FOX_WIDTH), BF16),
        scratch_shapes=[pltpu.VMEM((s, 2 * LANES), BF16), pltpu.VMEM((2, FOX_VROWS, s), BF16),
                        pltpu.VMEM((2, tq, tq), F32), pltpu.VMEM((2, tq, tq), F32),
                        pltpu.VMEM((2, FOX_VROWS, tq), F32), pltpu.VMEM((2, tq), F32)],
        compiler_params=_cparams(("parallel", "parallel", "arbitrary")),
        name="fox",
    )(zq3d, zq3d, zq3d, col3d)


HALO = 16


def _gdnprep_body(x_ref, halo_ref, cw_ref, oq_ref, ok_ref, ov_ref, ext_ref, *, tp):
    prev = halo_ref[0].astype(F32)
    ext_ref[0:HALO, :] = jnp.where(pl.program_id(1) > 0, prev, 0.0)
    ext_ref[HALO:, :] = x_ref[0].astype(F32)
    acc = cw_ref[GDN_CONV - 1:GDN_CONV, :] * ext_ref[HALO:HALO + tp, :]
    for j in range(GDN_CONV - 1):
        off = HALO - (GDN_CONV - 1) + j
        acc = acc + cw_ref[j:j + 1, :] * ext_ref[off:off + tp, :]
    y = acc * _sigmoid(acc)

    def l2(v):
        return v * lax.rsqrt(jnp.sum(v * v, axis=-1, keepdims=True) + EPS)

    for h in range(GDN_HEADS):
        lo, hi = h * GDN_HEAD_DIM, (h + 1) * GDN_HEAD_DIM
        oq_ref[0, :, lo:hi] = (l2(y[:, lo:hi]) * GDN_HEAD_DIM ** -0.5).astype(BF16)
        ok_ref[0, :, lo:hi] = l2(y[:, GDN_WIDTH + lo:GDN_WIDTH + hi]).astype(BF16)
    ov_ref[0] = y[:, 2 * GDN_WIDTH:].astype(BF16)


def _gdnprep(zg3d, conv_w, tp):
    b, s, c = zg3d.shape
    blk = lambda bi, i: (bi, i, 0)
    return pl.pallas_call(
        functools.partial(_gdnprep_body, tp=tp),
        grid=(b, s // tp),
        in_specs=[pl.BlockSpec((1, tp, c), blk),
                  pl.BlockSpec((1, HALO, c), lambda bi, i: (bi, jnp.maximum(i * (tp // HALO) - 1, 0), 0)),
                  pl.BlockSpec(conv_w.shape, lambda bi, i: (0, 0))],
        out_specs=[pl.BlockSpec((1, tp, GDN_WIDTH), blk)] * 3,
        out_shape=[jax.ShapeDtypeStruct((b, s, GDN_WIDTH), BF16)] * 3,
        scratch_shapes=[pltpu.VMEM((tp + HALO, c), F32)],
        compiler_params=_cparams(("parallel", "parallel")),
        name="gdnprep",
    )(zg3d, zg3d, conv_w)


def _stack_heads(x):
    return jnp.concatenate([x[:, h * GDN_HEAD_DIM:(h + 1) * GDN_HEAD_DIM] for h in range(GDN_HEADS)], axis=0)


def _gdn_body(q_ref, k_ref, v_ref, col_ref, grow_ref, o_ref, state_ref, *bufs, tg):
    C = GDN_CHUNK
    R = GDN_HEADS * C
    dh = GDN_HEAD_DIM
    n_chunks = tg // C
    step = pl.program_id(1)
    buf_sets = (bufs[:len(bufs) // 2], bufs[len(bufs) // 2:])

    @pl.when(step == 0)
    def _():
        for ref in (state_ref,) + tuple(bufs):
            ref[...] = jnp.zeros_like(ref)

    r = lax.broadcasted_iota(jnp.int32, (R, R), 0)
    c = lax.broadcasted_iota(jnp.int32, (R, R), 1)
    same_head = (r // C) == (c // C)
    lower = jnp.logical_and(same_head, r >= c)
    strict = jnp.logical_and(same_head, r > c)
    gc_lane0 = FOX_HEADS
    beta_lane0 = FOX_HEADS + GDN_HEADS

    def advance(ci, rd):
        u_ref, w_ref, intra_ref, qd_ref, kd_ref, gl_ref = rd
        r0 = ci * C
        u, w, intra = u_ref[ci], w_ref[ci], intra_ref[ci]
        q_dec, k_dec = qd_ref[ci], kd_ref[ci]
        v_new = []
        o_state = []
        for h in range(GDN_HEADS):
            sl = slice(h * C, (h + 1) * C)
            st = state_ref[h].astype(BF16)
            v_new.append(u[sl] - _dot(w[sl], st))
            o_state.append(_dot(q_dec[sl], st))
        v_new = jnp.concatenate(v_new, axis=0)
        v_new_b = v_new.astype(BF16)
        o_all = jnp.concatenate(o_state, axis=0) + _dot(intra, v_new_b)
        for h in range(GDN_HEADS):
            sl = slice(h * C, (h + 1) * C)
            state_ref[h] = state_ref[h] * gl_ref[ci, h:h + 1, :] + _dot_tn(k_dec[sl], v_new_b[sl])
        o_ref[0, r0:r0 + C, :] = jnp.concatenate(
            [o_all[h * C:(h + 1) * C] for h in range(GDN_HEADS)], axis=1).astype(BF16)

    chunks = range(n_chunks)

    def prepare_all(wr):
        u_ref, w_ref, intra_ref, qd_ref, kd_ref, gl_ref = wr
        qs, ks, vs, gc_col, beta_col, gl_row, lmat, intra = [], [], [], [], [], [], [], []
        for ci in chunks:
            r0 = ci * C
            qs.append(_stack_heads(q_ref[0, r0:r0 + C, :]).astype(F32))
            ks.append(_stack_heads(k_ref[0, r0:r0 + C, :]).astype(F32))
            vs.append(_stack_heads(v_ref[0, r0:r0 + C, :]).astype(F32))
            col = col_ref[0, r0:r0 + C, :]
            gc_col.append(
                jnp.concatenate([col[:, gc_lane0 + h:gc_lane0 + h + 1] for h in range(GDN_HEADS)], axis=0))
            beta_col.append(
                jnp.concatenate([col[:, beta_lane0 + h:beta_lane0 + h + 1] for h in range(GDN_HEADS)], axis=0))
            gl_row.append(col[C - 1:C, :])
            gc_row = grow_ref[0, ci, :, :]
            decay = jnp.exp(jnp.where(lower, gc_col[ci] - gc_row, NEG))
            ksb = ks[ci].astype(BF16)
            kk = _dot_nt(ksb, ksb)
            qk = _dot_nt(qs[ci].astype(BF16), ksb)
            lmat.append(jnp.where(strict, kk * decay * beta_col[ci], 0.0))
            intra.append((qk * decay).astype(BF16))
        n_mat = [-l for l in lmat]
        pw = lmat
        for _ in range(5):
            pwb = [p.astype(BF16) for p in pw]
            pw = [_dot(p, p) for p in pwb]
            n_mat = [n + p + _dot(n.astype(BF16), p.astype(BF16)) for n, p in zip(n_mat, pw)]
        for ci in chunks:
            e_gc = jnp.exp(gc_col[ci])
            rhs = jnp.concatenate([vs[ci] * beta_col[ci], ks[ci] * (beta_col[ci] * e_gc)], axis=1)
            sol = rhs + _dot(n_mat[ci].astype(BF16), rhs.astype(BF16))
            gl_col = jnp.concatenate(
                [jnp.broadcast_to(gl_row[ci][:, gc_lane0 + h:gc_lane0 + h + 1], (C, 1))
                 for h in range(GDN_HEADS)], axis=0)
            u_ref[ci] = sol[:, :dh]
            w_ref[ci] = sol[:, dh:].astype(BF16)
            intra_ref[ci] = intra[ci]
            qd_ref[ci] = (qs[ci] * e_gc).astype(BF16)
            kd_ref[ci] = (ks[ci] * jnp.exp(gl_col - gc_col[ci])).astype(BF16)
            for h in range(GDN_HEADS):
                gl_ref[ci, h:h + 1, :] = jnp.broadcast_to(
                    jnp.exp(gl_row[ci][:, gc_lane0 + h:gc_lane0 + h + 1]), (1, LANES))

    def run(rd, wr):
        for ci in chunks:
            advance(ci, rd)
        prepare_all(wr)

    @pl.when(step % 2 == 0)
    def _():
        run(buf_sets[0], buf_sets[1])

    @pl.when(step % 2 == 1)
    def _():
        run(buf_sets[1], buf_sets[0])


def _gdn(qn, kn, vv, col3d, gcrow, tg):
    b, s, _ = qn.shape
    n = s // tg
    nck = tg // GDN_CHUNK
    rows = GDN_HEADS * GDN_CHUNK
    dh = GDN_HEAD_DIM
    blk_in = lambda bi, i: (bi, jnp.minimum(i, n - 1), 0)
    blk_out = lambda bi, i: (bi, jnp.maximum(i - 1, 0), 0)
    return pl.pallas_call(
        functools.partial(_gdn_body, tg=tg),
        grid=(b, n + 1),
        in_specs=[pl.BlockSpec((1, tg, GDN_WIDTH), blk_in)] * 3
        + [pl.BlockSpec((1, tg, LANES), blk_in),
           pl.BlockSpec((1, nck, 1, rows), lambda bi, i: (bi, jnp.minimum(i, n - 1), 0, 0))],
        out_specs=pl.BlockSpec((1, tg, GDN_WIDTH), blk_out),
        out_shape=jax.ShapeDtypeStruct((b, s, GDN_WIDTH), BF16),
        scratch_shapes=[pltpu.VMEM((GDN_HEADS, dh, dh), F32)] + 2 * [
            pltpu.VMEM((nck, rows, dh), F32), pltpu.VMEM((nck, rows, dh), BF16),
            pltpu.VMEM((nck, rows, rows), BF16), pltpu.VMEM((nck, rows, dh), BF16),
            pltpu.VMEM((nck, rows, dh), BF16), pltpu.VMEM((nck, 8, LANES), F32)],
        compiler_params=_cparams(("parallel", "arbitrary")),
        name="gdn",
    )(qn, kn, vv, col3d, gcrow)


MOE_SLOT_TILE = 512


def _rows_to_tiles(ref, v):
    m, width = v.shape
    n = width // LANES
    for s in range(n):
        ref[pl.ds(s, m, stride=n), :] = v[:, s * LANES:(s + 1) * LANES]


def _tiles_to_rows(ref, m):
    n = ref.shape[0] // m
    return jnp.concatenate([ref[pl.ds(s, m, stride=n), :] for s in range(n)], axis=1)


RT_E1, RT_E2, RT_RANK1, RT_RANK2, RT_G1, RT_G2 = range(6)


def _combine_body(yf_ref, og_ref, dz_ref, gf_ref, gd_ref, x_ref, wof_ref, wod_ref, wout_ref,
                  gon_ref, gffn_ref, wrh_ref, wrl_ref, br_ref, x1_ref, t_ref, rt_ref, cnt_ref, carry_ref,
                  *, tc):
    @pl.when(pl.program_id(0) == 0)
    def _():
        carry_ref[...] = jnp.zeros_like(carry_ref)

    on = []
    for h in range(GDN_HEADS):
        sl = slice(h * GDN_HEAD_DIM, (h + 1) * GDN_HEAD_DIM)
        dz = dz_ref[:, sl].astype(F32)
        on.append(_rms(og_ref[:, sl].astype(F32), gon_ref[...]) * (dz * _sigmoid(dz)))
    on = jnp.concatenate(on, axis=1).astype(BF16)
    y_fox = _dot(yf_ref[...], wof_ref[...])
    y_delta = _dot(on, wod_ref[...])
    merged = _sigmoid(gf_ref[...].astype(F32)) * y_fox + _sigmoid(gd_ref[...].astype(F32)) * y_delta
    x1 = x_ref[...] + _dot(merged.astype(BF16), wout_ref[...])
    x1_ref[...] = x1
    t32 = _rms(x1, gffn_ref[...])
    th = t32.astype(BF16)
    tl = (t32 - th.astype(F32)).astype(BF16)
    _rows_to_tiles(t_ref, t32)
    logits = _dot(th, wrh_ref[...]) + _dot(tl, wrh_ref[...]) + _dot(th, wrl_ref[...]) + br_ref[...]
    lane = lax.broadcasted_iota(jnp.int32, logits.shape, 1)
    gl = jnp.where(lane < N_GROUPS, logits, NEG)
    gmax = jnp.max(gl, axis=-1, keepdims=True)
    g_sel = jnp.min(jnp.where(gl == gmax, lane, LANES), axis=-1, keepdims=True)
    p_sel = 1.0 / jnp.sum(jnp.exp(gl - gmax), axis=-1, keepdims=True)
    lo = EXPERT_LANE0 + EXPERTS_PER_GROUP * g_sel
    in_grp = jnp.logical_and(lane >= lo, lane < lo + EXPERTS_PER_GROUP)
    el = jnp.where(in_grp, logits, NEG)
    emax = jnp.max(el, axis=-1, keepdims=True)
    ee = jnp.where(in_grp, jnp.exp(el - emax), 0.0)
    pe = ee / jnp.sum(ee, axis=-1, keepdims=True)
    pe = jnp.where(in_grp, pe, -1.0)
    p1 = jnp.max(pe, axis=-1, keepdims=True)
    i1 = jnp.min(jnp.where(pe == p1, lane, LANES), axis=-1, keepdims=True)
    pe2 = jnp.where(lane == i1, -1.0, pe)
    p2 = jnp.max(pe2, axis=-1, keepdims=True)
    i2 = jnp.min(jnp.where(pe2 == p2, lane, LANES), axis=-1, keepdims=True)
    den = p1 + p2
    hit1 = lane == i1
    hit2 = lane == i2
    assign = jnp.where(jnp.logical_or(hit1, hit2), 1.0, 0.0)
    r = lax.broadcasted_iota(jnp.int32, (tc, tc), 0)
    c = lax.broadcasted_iota(jnp.int32, (tc, tc), 1)
    before = jnp.where(r > c, 1.0, 0.0).astype(BF16)
    prefix = _dot(before, assign.astype(BF16)) + carry_ref[...]
    rank1 = jnp.sum(jnp.where(hit1, prefix, 0.0), axis=-1, keepdims=True)
    rank2 = jnp.sum(jnp.where(hit2, prefix, 0.0), axis=-1, keepdims=True)
    carry_ref[...] = prefix[tc - 1:tc, :] + assign[tc - 1:tc, :]
    cnt_ref[...] = carry_ref[...]
    cols = ((i1 - EXPERT_LANE0).astype(F32), (i2 - EXPERT_LANE0).astype(F32), rank1, rank2,
            p_sel * (p1 / den), p_sel * (p2 / den))
    rt = jnp.zeros(logits.shape, F32)
    for k, v in enumerate(cols):
        rt = jnp.where(lane == k, v, rt)
    rt_ref[...] = rt


def _combine(yf, og, dz, gf, gd, x2d, wof, wod, wout, g_on, g_ffn, wrh, wrl, br, tc):
    t, d = x2d.shape
    row = lambda i: (i, 0)
    const = lambda i: (0, 0)
    acts = (yf, og, dz, gf, gd, x2d)
    consts = (wof, wod, wout, g_on, g_ffn, wrh, wrl, br)
    return pl.pallas_call(
        functools.partial(_combine_body, tc=tc),
        grid=(t // tc,),
        in_specs=[pl.BlockSpec((tc, a.shape[1]), row) for a in acts]
        + [pl.BlockSpec(c.shape, const) for c in consts],
        out_specs=[pl.BlockSpec((tc, d), row), pl.BlockSpec((tc * (d // LANES), LANES), row),
                   pl.BlockSpec((tc, LANES), row), pl.BlockSpec((1, LANES), const)],
        out_shape=[jax.ShapeDtypeStruct((t, d), F32), jax.ShapeDtypeStruct((t * (d // LANES), LANES), F32),
                   jax.ShapeDtypeStruct((t, LANES), F32), jax.ShapeDtypeStruct((1, LANES), F32)],
        scratch_shapes=[pltpu.VMEM((1, LANES), F32)],
        compiler_params=_cparams(("arbitrary",)),
        name="combine",
    )(*acts, *consts)


def _slots_body(rt_ref, cnt_ref, o_ref, *, tmx):
    lane1 = lax.broadcasted_iota(jnp.int32, (1, LANES), 1)
    is_expert = jnp.logical_and(lane1 >= EXPERT_LANE0, lane1 < EXPERT_LANE0 + N_EXPERTS)
    padded = jnp.where(is_expert, jnp.floor((cnt_ref[...] + (tmx - 1)) / tmx) * tmx, 0.0)
    r = lax.broadcasted_iota(jnp.int32, (LANES, LANES), 0)
    c = lax.broadcasted_iota(jnp.int32, (LANES, LANES), 1)
    before = jnp.where(r < c, 1.0, 0.0).astype(BF16)
    hi, mid, lo = _split3(jnp.broadcast_to(padded, (8, LANES)))
    offs = (_dot(hi, before) + _dot(mid, before) + _dot(lo, before))[0:1, :]
    rt = rt_ref[...]
    lane = lax.broadcasted_iota(jnp.int32, rt.shape, 1)
    expert_of_lane = (lane - EXPERT_LANE0).astype(F32)
    out = jnp.zeros(rt.shape, F32)
    for k, (ce, cr) in enumerate(((RT_E1, RT_RANK1), (RT_E2, RT_RANK2))):
        start = jnp.sum(jnp.where(expert_of_lane == rt[:, ce:ce + 1], offs, 0.0), axis=-1, keepdims=True)
        out = jnp.where(lane == k, start + rt[:, cr:cr + 1], out)
    o_ref[...] = out.T[:8, :].astype(jnp.int32)


def _slots(rt, cnt, tmx, ts):
    t = rt.shape[0]
    return pl.pallas_call(
        functools.partial(_slots_body, tmx=tmx),
        grid=(t // ts,),
        in_specs=[pl.BlockSpec((ts, LANES), lambda i: (i, 0)), pl.BlockSpec((1, LANES), lambda i: (0, 0))],
        out_specs=pl.BlockSpec((8, ts), lambda i: (0, i)),
        out_shape=jax.ShapeDtypeStruct((8, t), jnp.int32),
        compiler_params=_cparams(("parallel",)),
        name="slots",
    )(rt, cnt)


ROW_GROUP = 8


def _issue_row_copies(pos_ref, n_rows, nsub, make_copy):
    def issue(g, carry):
        r0 = g * ROW_GROUP
        slots = [[pos_ref[0, 0, k * n_rows + r0 + u] for k in range(2)] for u in range(ROW_GROUP)]
        for u in range(ROW_GROUP):
            tok = pl.ds(pl.multiple_of((r0 + u) * nsub, nsub), nsub)
            for k in range(2):
                slot = pl.ds(pl.multiple_of(slots[u][k] * nsub, nsub), nsub)
                make_copy(k, tok, slot).start(priority=k)
        return carry

    lax.fori_loop(0, n_rows // ROW_GROUP, issue, 0)


def _dispatch_body(zflag_ref, pos_ref, t_ref, xs_ref, zbuf, sem, zsem, *, td, tmx, n_tiles, nsub):
    @pl.when(pl.program_id(0) == 0)
    def _():
        zbuf[...] = jnp.zeros_like(zbuf)
        rows = tmx * nsub

        def zero_tile(k, carry):
            @pl.when(zflag_ref[k] != 0)
            def _():
                cp = pltpu.make_async_copy(zbuf, xs_ref.at[pl.ds(pl.multiple_of(k * rows, rows), rows), :], zsem)
                cp.start()
                cp.wait()

            return carry

        lax.fori_loop(0, n_tiles, zero_tile, 0)

    _issue_row_copies(pos_ref, td, nsub,
                      lambda k, tok, slot: pltpu.make_async_copy(t_ref.at[tok, :], xs_ref.at[slot, :], sem))
    for _ in range(2):
        pltpu.make_async_copy(t_ref, xs_ref.at[pl.ds(0, td * nsub), :], sem).wait()


def _dispatch(zflag, t_tiles, pos12, n_slots, td, tmx, nsub):
    t = t_tiles.shape[0] // nsub
    n_tiles = n_slots // tmx
    grid_spec = pltpu.PrefetchScalarGridSpec(
        num_scalar_prefetch=1,
        grid=(t // td,),
        in_specs=[pl.BlockSpec((1, 1, 2 * td), lambda i, zf: (i, 0, 0), memory_space=pltpu.SMEM),
                  pl.BlockSpec((td * nsub, LANES), lambda i, zf: (i, 0))],
        out_specs=pl.BlockSpec(memory_space=pl.ANY),
        scratch_shapes=[pltpu.VMEM((tmx * nsub, LANES), F32), pltpu.SemaphoreType.DMA(()),
                        pltpu.SemaphoreType.DMA(())],
    )
    return pl.pallas_call(
        functools.partial(_dispatch_body, td=td, tmx=tmx, n_tiles=n_tiles, nsub=nsub),
        grid_spec=grid_spec,
        out_shape=jax.ShapeDtypeStruct((n_slots * nsub, LANES), F32),
        compiler_params=_cparams(("arbitrary",)),
        name="dispatch",
    )(zflag, pos12, t_tiles)


def _experts_body(te_ref, nused_ref, xs_ref, wgu_ref, wd_ref, ys_ref, *, tmx):
    @pl.when(pl.program_id(0) < nused_ref[0])
    def _():
        hgu = _dot(_tiles_to_rows(xs_ref, tmx).astype(BF16), wgu_ref[0])
        a = hgu[:, :EXPERT_FF]
        hid = a * _sigmoid(a) * hgu[:, EXPERT_FF:]
        _rows_to_tiles(ys_ref, _dot(hid.astype(BF16), wd_ref[0]))

    @pl.when(pl.program_id(0) >= nused_ref[0])
    def _():
        ys_ref[...] = jnp.zeros_like(ys_ref)


def _experts(tile_expert, n_used, xs, wgu, wd, tmx):
    d = wgu.shape[1]
    nsub = d // LANES
    n_slots = xs.shape[0] // nsub
    grid_spec = pltpu.PrefetchScalarGridSpec(
        num_scalar_prefetch=2,
        grid=(n_slots // tmx,),
        in_specs=[pl.BlockSpec((tmx * nsub, LANES), lambda i, te, nu: (i, 0)),
                  pl.BlockSpec((1, d, 2 * EXPERT_FF), lambda i, te, nu: (te[i], 0, 0)),
                  pl.BlockSpec((1, EXPERT_FF, d), lambda i, te, nu: (te[i], 0, 0))],
        out_specs=pl.BlockSpec((tmx * nsub, LANES), lambda i, te, nu: (i, 0)),
    )
    return pl.pallas_call(
        functools.partial(_experts_body, tmx=tmx),
        grid_spec=grid_spec,
        out_shape=jax.ShapeDtypeStruct(xs.shape, F32),
        compiler_params=_cparams(("arbitrary",)),
        name="experts",
    )(tile_expert, n_used, xs, wgu, wd)


def _final_body(pos_ref, posn_ref, x_ref, rt_ref, p_ref, gple_ref, wpg_ref, wpp_ref, gfin_ref, ys_ref,
                o_ref, gbuf, sems, *, tf, nsub):
    i = pl.program_id(0)
    n = pl.num_programs(0)
    slot = i % 2

    def gather(pref, dst_slot):
        _issue_row_copies(pref, tf, nsub, lambda k, tok, src: pltpu.make_async_copy(
            ys_ref.at[src, :], gbuf.at[dst_slot, k, tok, :], sems.at[dst_slot]))

    @pl.when(i == 0)
    def _():
        gather(pos_ref, 0)

    @pl.when(i + 1 < n)
    def _():
        gather(posn_ref, 1 - slot)

    for k in range(2):
        pltpu.make_async_copy(ys_ref.at[pl.ds(0, tf * nsub), :], gbuf.at[slot, k], sems.at[slot]).wait()

    rt = rt_ref[...]
    g1 = rt[:, RT_G1:RT_G1 + 1]
    g2 = rt[:, RT_G2:RT_G2 + 1]
    x = x_ref[...] + g1 * _tiles_to_rows(gbuf.at[slot, 0], tf) + g2 * _tiles_to_rows(gbuf.at[slot, 1], tf)
    r = _rms(x, gple_ref[...]).astype(BF16)
    ple_gate = _sigmoid(_dot(r, wpg_ref[...]))
    proj = _dot(p_ref[...].astype(BF16), wpp_ref[...])
    o_ref[...] = _rms(x + ple_gate * proj, gfin_ref[...])


def _final(pos12, x1, rt, p2d, g_ple, wpg, wpp, g_final, ys, tf):
    t, d = x1.shape
    nsub = d // LANES
    n = t // tf
    row = lambda i: (i, 0)
    const = lambda i: (0, 0)
    smem_blk = lambda imap: pl.BlockSpec((1, 1, 2 * tf), imap, memory_space=pltpu.SMEM)
    return pl.pallas_call(
        functools.partial(_final_body, tf=tf, nsub=nsub),
        grid=(n,),
        in_specs=[smem_blk(lambda i: (i, 0, 0)), smem_blk(lambda i: (jnp.minimum(i + 1, n - 1), 0, 0)),
                  pl.BlockSpec((tf, d), row), pl.BlockSpec((tf, LANES), row),
                  pl.BlockSpec((tf, p2d.shape[1]), row),
                  pl.BlockSpec((1, d), const), pl.BlockSpec(wpg.shape, const),
                  pl.BlockSpec(wpp.shape, const), pl.BlockSpec((1, d), const),
                  pl.BlockSpec(memory_space=pl.ANY)],
        out_specs=pl.BlockSpec((tf, d), row),
        out_shape=jax.ShapeDtypeStruct((t, d), F32),
        scratch_shapes=[pltpu.VMEM((2, 2, tf * nsub, LANES), F32), pltpu.SemaphoreType.DMA((2,))],
        compiler_params=_cparams(("arbitrary",)),
        name="final",
    )(pos12, pos12, x1, rt, p2d, g_ple, wpg, wpp, g_final, ys)


def _pad_lanes(v, lane0):
    return jnp.zeros((1, LANES), F32).at[0, lane0:lane0 + v.shape[0]].set(v.astype(F32))


def _tile(n, pref):
    return pref if n % pref == 0 else n


def _layer(x, p_l, g_mix, w_in, b_forget, conv_w, a_log, dt_bias, g_onorm, w_o_fox, w_o_delta, w_out,
           g_ffn, w_group, b_group, w_router, b_router, w_gate, w_up, w_down, g_ple, w_ple_gate, w_ple_proj,
           g_post, apply_post):
    b, s, d = x.shape
    t = b * s
    x2d = x.reshape(t, d)

    o_ff = 3 * FOX_WIDTH
    o_qkv = o_ff + FOX_HEADS
    o_da = o_qkv + 3 * GDN_WIDTH
    o_db = o_da + GDN_HEADS
    o_dz = o_db + GDN_HEADS
    o_gf = o_dz + GDN_WIDTH
    o_gd = o_gf + d
    w_small = jnp.concatenate(
        [w_in[:, o_ff:o_qkv], w_in[:, o_da:o_db], w_in[:, o_db:o_dz],
         jnp.zeros((d, LANES - FOX_HEADS - 2 * GDN_HEADS), w_in.dtype)], axis=1)
    wq = w_in[:, :o_ff].astype(BF16)
    wg = w_in[:, o_qkv:o_da].astype(BF16)
    wzs = jnp.concatenate([w_in[:, o_dz:o_gf], w_small], axis=1).astype(BF16)
    wgf = w_in[:, o_gf:o_gd].astype(BF16)
    wgd = w_in[:, o_gd:].astype(BF16)

    zq, zg, dz, small, gf, gd = _inproj(x2d, g_mix.reshape(1, d), wq, wg, wzs, wgf, wgd, _tile(t, 512))

    par = jnp.concatenate(
        [_pad_lanes(b_forget, 0) + _pad_lanes(dt_bias, FOX_HEADS), _pad_lanes(a_log, FOX_HEADS),
         jnp.zeros((6, LANES), F32)], axis=0)
    col = _gates(small.reshape(b, s, LANES), par, _tile(s, 256))

    nc = s // GDN_CHUNK
    gc = col[:, :, FOX_HEADS:FOX_HEADS + GDN_HEADS].reshape(b, nc, GDN_CHUNK, GDN_HEADS)
    gcrow = jnp.transpose(gc, (0, 1, 3, 2)).reshape(b, nc, 1, GDN_HEADS * GDN_CHUNK)

    y_fox = _fox(zq.reshape(b, s, 3 * FOX_WIDTH), col, _tile(s, 512))

    qn, kn, vv = _gdnprep(zg.reshape(b, s, 3 * GDN_WIDTH), conv_w.astype(F32), _tile(s, 512))
    o_gdn = _gdn(qn, kn, vv, col, gcrow, _tile(s, 512))

    w_r = jnp.concatenate([w_group, w_router,
                           jnp.zeros((d, LANES - N_GROUPS - N_EXPERTS), w_group.dtype)], axis=1).astype(F32)
    wrh = w_r.astype(BF16)
    wrl = (w_r - wrh.astype(F32)).astype(BF16)
    br = _pad_lanes(b_group, 0) + _pad_lanes(b_router, EXPERT_LANE0)
    x1, tpk, rt, cnt = _combine(
        y_fox.reshape(t, FOX_WIDTH), o_gdn.reshape(t, GDN_WIDTH), dz, gf, gd, x2d,
        w_o_fox.astype(BF16), w_o_delta.astype(BF16), w_out.astype(BF16),
        g_onorm.reshape(1, GDN_HEAD_DIM).astype(F32), g_ffn.reshape(1, d).astype(F32), wrh, wrl, br,
        _tile(t, 512))

    tmx = MOE_SLOT_TILE
    n_slots = 2 * t + N_EXPERTS * tmx
    counts = cnt[0, EXPERT_LANE0:EXPERT_LANE0 + N_EXPERTS].astype(jnp.int32)
    padded = (counts + tmx - 1) // tmx * tmx
    ends = jnp.cumsum(padded)
    pos = _slots(rt, cnt, tmx, _tile(t, 512))
    pos1, pos2 = pos[0], pos[1]
    tile_start = jnp.arange(n_slots // tmx, dtype=jnp.int32) * tmx
    tile_expert = jnp.minimum(jnp.sum(tile_start[:, None] >= ends[None, :], axis=1), N_EXPERTS - 1).astype(jnp.int32)
    n_used = (ends[-1:] // tmx).astype(jnp.int32)
    last_of_expert = jnp.any(jnp.logical_and(tile_start[:, None] + tmx == ends[None, :], padded[None, :] > 0), axis=1)
    zflag = jnp.logical_or(last_of_expert, tile_start >= ends[-1]).astype(jnp.int32)

    def tiled_pos(tile):
        n = t // tile
        return jnp.concatenate([pos1.reshape(n, 1, tile), pos2.reshape(n, 1, tile)], axis=2)

    td = _tile(t, 512)
    nsub = d // LANES
    xs = _dispatch(zflag, tpk, tiled_pos(td), n_slots, td, tmx, nsub)
    wgu = jnp.concatenate([w_gate, w_up], axis=2).astype(BF16)
    ys = _experts(tile_expert, n_used, xs, wgu, w_down.astype(BF16), tmx)

    tf = _tile(t, 256)
    out = _final(tiled_pos(tf), x1, rt, p_l.reshape(t, -1), g_ple.reshape(1, d).astype(F32),
                 w_ple_gate.astype(BF16), w_ple_proj.astype(BF16), g_post.reshape(1, d).astype(F32), ys, tf)
    return out.reshape(b, s, d)


def kernel(x, p, g_mix, w_in, b_forget, conv_w, a_log, dt_bias, g_onorm, w_o_fox, w_o_delta, w_out,
           g_ffn, w_group, b_group, w_router, b_router, w_gate, w_up, w_down, g_ple, w_ple_gate, w_ple_proj,
           g_final):
    depth = p.shape[0]
    assert depth == 1, "the final rmsnorm is fused into the last layer's epilogue; depth 1 only"
    i = 0
    return _layer(x, p[i], g_mix[i], w_in[i], b_forget[i], conv_w[i], a_log[i], dt_bias[i], g_onorm[i],
                  w_o_fox[i], w_o_delta[i], w_out[i], g_ffn[i], w_group[i], b_group[i], w_router[i],
                  b_router[i], w_gate[i], w_up[i], w_down[i], g_ple[i], w_ple_gate[i], w_ple_proj[i],
                  g_final, True)
```

```python
import functools

import jax
import jax.numpy as jnp
from jax import lax
from jax.experimental import pallas as pl
from jax.experimental.pallas import tpu as pltpu

F32 = jnp.float32
BF16 = jnp.bfloat16
EPS = 1e-6
NEG = -1e30

FOX_HEADS = 8
FOX_HEAD_DIM = 64
FOX_WIDTH = FOX_HEADS * FOX_HEAD_DIM
GDN_HEADS = 4
GDN_HEAD_DIM = 128
GDN_CONV = 4
GDN_CHUNK = 64
GDN_WIDTH = GDN_HEADS * GDN_HEAD_DIM
N_GROUPS = 4
EXPERTS_PER_GROUP = 8
N_EXPERTS = N_GROUPS * EXPERTS_PER_GROUP
EXPERT_FF = 256
LANES = 128
EXPERT_LANE0 = N_GROUPS

VMEM_LIMIT_BYTES = 56 * 1024 * 1024


def _cparams(sem):
    return pltpu.CompilerParams(dimension_semantics=sem, vmem_limit_bytes=VMEM_LIMIT_BYTES)


def _rms(x, g):
    return x * lax.rsqrt(jnp.mean(x * x, axis=-1, keepdims=True) + EPS) * g


def _sigmoid(x):
    return 1.0 / (1.0 + jnp.exp(-x))


def _dot(a, b):
    return jnp.dot(a, b, preferred_element_type=F32)


def _dot_nt(a, b):
    return lax.dot_general(a, b, (((1,), (1,)), ((), ())), preferred_element_type=F32)


def _dot_tn(a, b):
    return lax.dot_general(a, b, (((0,), (0,)), ((), ())), preferred_element_type=F32)


def _split3(v):
    hi = v.astype(BF16)
    r = v - hi.astype(F32)
    mid = r.astype(BF16)
    lo = (r - mid.astype(F32)).astype(BF16)
    return hi, mid, lo


def _dot_exact_lhs01(mat01, v):
    hi, mid, lo = _split3(v)
    return _dot(mat01, hi) + _dot(mat01, mid) + _dot(mat01, lo)


def _inproj_body(x_ref, g_ref, wq_ref, wg_ref, wzs_ref, wgf_ref, wgd_ref,
                 oq_ref, og_ref, oz_ref, osm_ref, ogf_ref, ogd_ref):
    hb = _rms(x_ref[...], g_ref[...]).astype(BF16)
    oq_ref[...] = _dot(hb, wq_ref[...]).astype(BF16)
    og_ref[...] = _dot(hb, wg_ref[...]).astype(BF16)
    zs = _dot(hb, wzs_ref[...])
    oz_ref[...] = zs[:, :GDN_WIDTH].astype(BF16)
    osm_ref[...] = zs[:, GDN_WIDTH:]
    ogf_ref[...] = _dot(hb, wgf_ref[...]).astype(BF16)
    ogd_ref[...] = _dot(hb, wgd_ref[...]).astype(BF16)


def _inproj(x2d, g_mix, wq, wg, wzs, wgf, wgd, tm):
    t, d = x2d.shape
    row = lambda i: (i, 0)
    const = lambda i: (0, 0)
    widths = (3 * FOX_WIDTH, 3 * GDN_WIDTH, GDN_WIDTH, LANES, d, d)
    dtypes = (BF16, BF16, BF16, F32, BF16, BF16)
    return pl.pallas_call(
        _inproj_body,
        grid=(t // tm,),
        in_specs=[pl.BlockSpec((tm, d), row), pl.BlockSpec((1, d), const)]
        + [pl.BlockSpec(w.shape, const) for w in (wq, wg, wzs, wgf, wgd)],
        out_specs=[pl.BlockSpec((tm, n), row) for n in widths],
        out_shape=[jax.ShapeDtypeStruct((t, n), dt) for n, dt in zip(widths, dtypes)],
        compiler_params=_cparams(("parallel",)),
        name="inproj",
    )(x2d, g_mix, wq, wg, wzs, wgf, wgd)


def _gates_tile(sm, par_ref, carry_ref):
    tp = sm.shape[0]
    lane = lax.broadcasted_iota(jnp.int32, sm.shape, 1)
    z = sm + par_ref[0:1, :]
    soft = jnp.log1p(jnp.exp(-jnp.abs(z)))
    softplus = jnp.maximum(z, 0.0) + soft
    logf = jnp.minimum(z, 0.0) - soft
    g = -jnp.exp(par_ref[1:2, :]) * softplus
    beta = _sigmoid(sm)
    is_f = lane < FOX_HEADS
    is_g = jnp.logical_and(lane >= FOX_HEADS, lane < FOX_HEADS + GDN_HEADS)
    is_b = jnp.logical_and(lane >= FOX_HEADS + GDN_HEADS, lane < FOX_HEADS + 2 * GDN_HEADS)
    val = jnp.where(is_f, logf, jnp.where(is_g, g, 0.0))
    r = lax.broadcasted_iota(jnp.int32, (tp, tp), 0)
    c = lax.broadcasted_iota(jnp.int32, (tp, tp), 1)
    lower = r >= c
    tri = jnp.where(lower, 1.0, 0.0).astype(BF16)
    same_chunk = (r // GDN_CHUNK) == (c // GDN_CHUNK)
    tri_chunk = jnp.where(jnp.logical_and(lower, same_chunk), 1.0, 0.0).astype(BF16)
    cum = _dot_exact_lhs01(tri, val) + carry_ref[...]
    gcs = _dot_exact_lhs01(tri_chunk, val)
    carry_ref[...] = cum[tp - 1:tp, :]
    return jnp.where(is_f, cum, jnp.where(is_g, gcs, jnp.where(is_b, beta, 0.0)))


LOG2E = 1.4426950408889634


FOX_VROWS = FOX_HEAD_DIM + 16


def _split3_f32(v):
    hi = v.astype(BF16).astype(F32)
    r = v - hi
    mid = r.astype(BF16).astype(F32)
    lo = (r - mid).astype(BF16).astype(F32)
    return hi, mid, lo


def _fox_body(q_ref, k_ref, v_ref, col_ref, o_ref, kx_ref, vt_ref, sa_ref, sb_ref, acc_ref, m_ref,
              *, tq, s_len):
    hp = pl.program_id(1)
    qi = pl.program_id(2)
    hd = FOX_HEAD_DIM

    @pl.when(qi == 0)
    def _():
        def prep(ci, carry):
            r0 = pl.multiple_of(ci * tq, tq)
            kx_ref[pl.ds(r0, tq), :LANES] = k_ref[0, pl.ds(r0, tq), :]
            col = col_ref[0, pl.ds(r0, tq), :]
            lane = lax.broadcasted_iota(jnp.int32, col.shape, 1)
            bias = jnp.zeros(col.shape, F32)
            for hh in range(2):
                ck = jnp.sum(jnp.where(lane == 2 * hp + hh, col, 0.0), axis=-1, keepdims=True) * LOG2E
                for k, piece in enumerate(_split3_f32(ck)):
                    bias = jnp.where(lane == 3 * hh + k, piece, bias)
            kx_ref[pl.ds(r0, tq), LANES:] = bias.astype(BF16)
            vt = v_ref[0, pl.ds(r0, tq), :].astype(F32).T
            ones_row = jnp.where(lax.broadcasted_iota(jnp.int32, (FOX_VROWS - hd, tq), 0) == 0, 1.0, 0.0)
            for hh in range(2):
                vt_ref[hh, :hd, pl.ds(r0, tq)] = vt[hh * hd:(hh + 1) * hd].astype(BF16)
                vt_ref[hh, hd:, pl.ds(r0, tq)] = ones_row.astype(BF16)
            return carry

        lax.fori_loop(0, s_len // tq, prep, 0)

    lane = lax.broadcasted_iota(jnp.int32, (tq, LANES), 1)
    q = (q_ref[0].astype(F32) * (hd ** -0.5 * LOG2E)).astype(BF16)
    zero = jnp.zeros_like(q)
    qx = []
    for hh in range(2):
        qh = jnp.where(jnp.logical_and(lane >= hh * hd, lane < (hh + 1) * hd), q, zero)
        sel = jnp.where(jnp.logical_and(lane >= 3 * hh, lane < 3 * hh + 3), -1.0, 0.0).astype(BF16)
        qx.append(jnp.concatenate([qh, sel], axis=1))
    m_ref[...] = jnp.full(m_ref.shape, NEG, F32)
    acc_ref[...] = jnp.zeros(acc_ref.shape, F32)

    def scores(j, dst_ref):
        start = pl.multiple_of(j * tq, tq)
        kx = kx_ref[pl.ds(start, tq), :]
        for hh in range(2):
            dst_ref[hh] = _dot_nt(kx, qx[hh])

    def consume(j, src_ref, masked):
        start = pl.multiple_of(j * tq, tq)
        for hh in range(2):
            s = src_ref[hh]
            if masked:
                key = lax.broadcasted_iota(jnp.int32, (tq, tq), 0)
                qry = lax.broadcasted_iota(jnp.int32, (tq, tq), 1)
                s = jnp.where(key <= qry, s, NEG)
            m_old = m_ref[hh:hh + 1, :]
            m_new = jnp.maximum(m_old, jnp.max(s, axis=0, keepdims=True))
            alpha = jnp.exp2(m_old - m_new)
            p = jnp.exp2(s - m_new).astype(BF16)
            acc_ref[hh] = alpha * acc_ref[hh] + _dot(vt_ref[hh, :, pl.ds(start, tq)], p)
            m_ref[hh:hh + 1, :] = m_new

    scores(0, sa_ref)

    def loop_body(j, carry):
        @pl.when(j % 2 == 0)
        def _():
            scores(j + 1, sb_ref)
            consume(j, sa_ref, False)

        @pl.when(j % 2 == 1)
        def _():
            scores(j + 1, sa_ref)
            consume(j, sb_ref, False)

        return carry

    lax.fori_loop(0, qi, loop_body, 0)

    @pl.when(qi % 2 == 0)
    def _():
        consume(qi, sa_ref, True)

    @pl.when(qi % 2 == 1)
    def _():
        consume(qi, sb_ref, True)

    out_t = jnp.concatenate([acc_ref[hh, :hd] / acc_ref[hh, hd:hd + 1] for hh in range(2)], axis=0)
    o_ref[0] = out_t.T.astype(BF16)


def _fox(zq3d, col3d, tq):
    b, s, _ = zq3d.shape
    npair = FOX_HEADS // 2
    kblk = FOX_WIDTH // LANES
    return pl.pallas_call(
        functools.partial(_fox_body, tq=tq, s_len=s),
        grid=(b, npair, s // tq),
        in_specs=[pl.BlockSpec((1, tq, LANES), lambda bi, hp, qi: (bi, qi, hp)),
                  pl.BlockSpec((1, s, LANES), lambda bi, hp, qi: (bi, 0, kblk + hp)),
                  pl.BlockSpec((1, s, LANES), lambda bi, hp, qi: (bi, 0, 2 * kblk + hp)),
                  pl.BlockSpec((1, s, LANES), lambda bi, hp, qi: (bi, 0, 0))],
        out_specs=pl.BlockSpec((1, tq, LANES), lambda bi, hp, qi: (bi, qi, hp)),
        out_shape=jax.ShapeDtypeStruct((b, s, FOX_WIDTH), BF16),
        scratch_shapes=[pltpu.VMEM((s, 2 * LANES), BF16), pltpu.VMEM((2, FOX_VROWS, s), BF16),
                        pltpu.VMEM((2, tq, tq), F32), pltpu.VMEM((2, tq, tq), F32),
                        pltpu.VMEM((2, FOX_VROWS, tq), F32), pltpu.VMEM((2, tq), F32)],
        compiler_params=_cparams(("parallel", "parallel", "arbitrary")),
        name="fox",
    )(zq3d, zq3d, zq3d, col3d)


HALO = 16


def _gdnprep_body(x_ref, halo_ref, cw_ref, sm_ref, par_ref, oq_ref, ok_ref, ov_ref, ocol_ref, ext_ref, carry_ref,
                  *, tp):
    @pl.when(pl.program_id(1) == 0)
    def _():
        carry_ref[...] = jnp.zeros_like(carry_ref)

    ocol_ref[0] = _gates_tile(sm_ref[0], par_ref, carry_ref)
    prev = halo_ref[0].astype(F32)
    ext_ref[0:HALO, :] = jnp.where(pl.program_id(1) > 0, prev, 0.0)
    ext_ref[HALO:, :] = x_ref[0].astype(F32)
    acc = cw_ref[GDN_CONV - 1:GDN_CONV, :] * ext_ref[HALO:HALO + tp, :]
    for j in range(GDN_CONV - 1):
        off = HALO - (GDN_CONV - 1) + j
        acc = acc + cw_ref[j:j + 1, :] * ext_ref[off:off + tp, :]
    y = acc * _sigmoid(acc)

    def l2(v):
        return v * lax.rsqrt(jnp.sum(v * v, axis=-1, keepdims=True) + EPS)

    for h in range(GDN_HEADS):
        lo, hi = h * GDN_HEAD_DIM, (h + 1) * GDN_HEAD_DIM
        oq_ref[0, :, lo:hi] = (l2(y[:, lo:hi]) * GDN_HEAD_DIM ** -0.5).astype(BF16)
        ok_ref[0, :, lo:hi] = l2(y[:, GDN_WIDTH + lo:GDN_WIDTH + hi]).astype(BF16)
    ov_ref[0] = y[:, 2 * GDN_WIDTH:].astype(BF16)


def _gdnprep(zg3d, conv_w, small3d, par, tp):
    b, s, c = zg3d.shape
    blk = lambda bi, i: (bi, i, 0)
    return pl.pallas_call(
        functools.partial(_gdnprep_body, tp=tp),
        grid=(b, s // tp),
        in_specs=[pl.BlockSpec((1, tp, c), blk),
                  pl.BlockSpec((1, HALO, c), lambda bi, i: (bi, jnp.maximum(i * (tp // HALO) - 1, 0), 0)),
                  pl.BlockSpec(conv_w.shape, lambda bi, i: (0, 0)),
                  pl.BlockSpec((1, tp, LANES), blk), pl.BlockSpec(par.shape, lambda bi, i: (0, 0))],
        out_specs=[pl.BlockSpec((1, tp, GDN_WIDTH), blk)] * 3 + [pl.BlockSpec((1, tp, LANES), blk)],
        out_shape=[jax.ShapeDtypeStruct((b, s, GDN_WIDTH), BF16)] * 3 + [jax.ShapeDtypeStruct((b, s, LANES), F32)],
        scratch_shapes=[pltpu.VMEM((tp + HALO, c), F32), pltpu.VMEM((1, LANES), F32)],
        compiler_params=_cparams(("parallel", "arbitrary")),
        name="gdnprep",
    )(zg3d, zg3d, conv_w, small3d, par)


def _stack_heads(x):
    return jnp.concatenate([x[:, h * GDN_HEAD_DIM:(h + 1) * GDN_HEAD_DIM] for h in range(GDN_HEADS)], axis=0)


def _gdn_body(q_ref, k_ref, v_ref, col_ref, grow_ref, o_ref, state_ref, *bufs, tg):
    C = GDN_CHUNK
    R = GDN_HEADS * C
    dh = GDN_HEAD_DIM
    n_chunks = tg // C
    step = pl.program_id(1)
    buf_sets = (bufs[:len(bufs) // 2], bufs[len(bufs) // 2:])

    @pl.when(step == 0)
    def _():
        for ref in (state_ref,) + tuple(bufs):
            ref[...] = jnp.zeros_like(ref)

    r = lax.broadcasted_iota(jnp.int32, (R, R), 0)
    c = lax.broadcasted_iota(jnp.int32, (R, R), 1)
    same_head = (r // C) == (c // C)
    lower = jnp.logical_and(same_head, r >= c)
    strict = jnp.logical_and(same_head, r > c)
    gc_lane0 = FOX_HEADS
    beta_lane0 = FOX_HEADS + GDN_HEADS

    def advance(ci, rd):
        u_ref, w_ref, intra_ref, qd_ref, kd_ref, gl_ref = rd
        r0 = ci * C
        u, w, intra = u_ref[ci], w_ref[ci], intra_ref[ci]
        q_dec, k_dec = qd_ref[ci], kd_ref[ci]
        v_new = []
        o_state = []
        for h in range(GDN_HEADS):
            sl = slice(h * C, (h + 1) * C)
            st = state_ref[h].astype(BF16)
            v_new.append(u[sl] - _dot(w[sl], st))
            o_state.append(_dot(q_dec[sl], st))
        v_new = jnp.concatenate(v_new, axis=0)
        v_new_b = v_new.astype(BF16)
        o_all = jnp.concatenate(o_state, axis=0) + _dot(intra, v_new_b)
        for h in range(GDN_HEADS):
            sl = slice(h * C, (h + 1) * C)
            state_ref[h] = state_ref[h] * gl_ref[ci, h:h + 1, :] + _dot_tn(k_dec[sl], v_new_b[sl])
        o_ref[0, r0:r0 + C, :] = jnp.concatenate(
            [o_all[h * C:(h + 1) * C] for h in range(GDN_HEADS)], axis=1).astype(BF16)

    chunks = range(n_chunks)

    def prepare_all(wr):
        u_ref, w_ref, intra_ref, qd_ref, kd_ref, gl_ref = wr
        qs, ks, vs, gc_col, beta_col, gl_row, lmat, intra = [], [], [], [], [], [], [], []
        for ci in chunks:
            r0 = ci * C
            qs.append(_stack_heads(q_ref[0, r0:r0 + C, :]).astype(F32))
            ks.append(_stack_heads(k_ref[0, r0:r0 + C, :]).astype(F32))
            vs.append(_stack_heads(v_ref[0, r0:r0 + C, :]).astype(F32))
            col = col_ref[0, r0:r0 + C, :]
            gc_col.append(
                jnp.concatenate([col[:, gc_lane0 + h:gc_lane0 + h + 1] for h in range(GDN_HEADS)], axis=0))
            beta_col.append(
                jnp.concatenate([col[:, beta_lane0 + h:beta_lane0 + h + 1] for h in range(GDN_HEADS)], axis=0))
            gl_row.append(col[C - 1:C, :])
            gc_row = grow_ref[0, ci, :, :]
            decay = jnp.exp(jnp.where(lower, gc_col[ci] - gc_row, NEG))
            ksb = ks[ci].astype(BF16)
            kk = _dot_nt(ksb, ksb)
            qk = _dot_nt(qs[ci].astype(BF16), ksb)
            lmat.append(jnp.where(strict, kk * decay * beta_col[ci], 0.0))
            intra.append((qk * decay).astype(BF16))
        n_mat = [-l for l in lmat]
        pw = lmat
        for _ in range(5):
            pwb = [p.astype(BF16) for p in pw]
            pw = [_dot(p, p) for p in pwb]
            n_mat = [n + p + _dot(n.astype(BF16), p.astype(BF16)) for n, p in zip(n_mat, pw)]
        for ci in chunks:
            e_gc = jnp.exp(gc_col[ci])
            rhs = jnp.concatenate([vs[ci] * beta_col[ci], ks[ci] * (beta_col[ci] * e_gc)], axis=1)
            sol = rhs + _dot(n_mat[ci].astype(BF16), rhs.astype(BF16))
            gl_col = jnp.concatenate(
                [jnp.broadcast_to(gl_row[ci][:, gc_lane0 + h:gc_lane0 + h + 1], (C, 1))
                 for h in range(GDN_HEADS)], axis=0)
            u_ref[ci] = sol[:, :dh]
            w_ref[ci] = sol[:, dh:].astype(BF16)
            intra_ref[ci] = intra[ci]
            qd_ref[ci] = (qs[ci] * e_gc).astype(BF16)
            kd_ref[ci] = (ks[ci] * jnp.exp(gl_col - gc_col[ci])).astype(BF16)
            for h in range(GDN_HEADS):
                gl_ref[ci, h:h + 1, :] = jnp.broadcast_to(
                    jnp.exp(gl_row[ci][:, gc_lane0 + h:gc_lane0 + h + 1]), (1, LANES))

    def run(rd, wr):
        for ci in chunks:
            advance(ci, rd)
        prepare_all(wr)

    @pl.when(step % 2 == 0)
    def _():
        run(buf_sets[0], buf_sets[1])

    @pl.when(step % 2 == 1)
    def _():
        run(buf_sets[1], buf_sets[0])


def _gdn(qn, kn, vv, col3d, gcrow, tg):
    b, s, _ = qn.shape
    n = s // tg
    nck = tg // GDN_CHUNK
    rows = GDN_HEADS * GDN_CHUNK
    dh = GDN_HEAD_DIM
    blk_in = lambda bi, i: (bi, jnp.minimum(i, n - 1), 0)
    blk_out = lambda bi, i: (bi, jnp.maximum(i - 1, 0), 0)
    return pl.pallas_call(
        functools.partial(_gdn_body, tg=tg),
        grid=(b, n + 1),
        in_specs=[pl.BlockSpec((1, tg, GDN_WIDTH), blk_in)] * 3
        + [pl.BlockSpec((1, tg, LANES), blk_in),
           pl.BlockSpec((1, nck, 1, rows), lambda bi, i: (bi, jnp.minimum(i, n - 1), 0, 0))],
        out_specs=pl.BlockSpec((1, tg, GDN_WIDTH), blk_out),
        out_shape=jax.ShapeDtypeStruct((b, s, GDN_WIDTH), BF16),
        scratch_shapes=[pltpu.VMEM((GDN_HEADS, dh, dh), F32)] + 2 * [
            pltpu.VMEM((nck, rows, dh), F32), pltpu.VMEM((nck, rows, dh), BF16),
            pltpu.VMEM((nck, rows, rows), BF16), pltpu.VMEM((nck, rows, dh), BF16),
            pltpu.VMEM((nck, rows, dh), BF16), pltpu.VMEM((nck, 8, LANES), F32)],
        compiler_params=_cparams(("parallel", "arbitrary")),
        name="gdn",
    )(qn, kn, vv, col3d, gcrow)


MOE_SLOT_TILE = 512


def _rows_to_tiles(ref, v):
    m, width = v.shape
    n = width // LANES
    for s in range(n):
        ref[pl.ds(s, m, stride=n), :] = v[:, s * LANES:(s + 1) * LANES]


def _tiles_to_rows(ref, m):
    n = ref.shape[0] // m
    return jnp.concatenate([ref[pl.ds(s, m, stride=n), :] for s in range(n)], axis=1)


RT_E1, RT_E2, RT_RANK1, RT_RANK2, RT_G1, RT_G2 = range(6)


def _combine_body(yf_ref, og_ref, dz_ref, gf_ref, gd_ref, x_ref, wof_ref, wod_ref, wout_ref,
                  gon_ref, gffn_ref, wrh_ref, wrl_ref, br_ref, x1_ref, t_ref, rt_ref, cnt_ref, carry_ref,
                  *, tc):
    @pl.when(pl.program_id(0) == 0)
    def _():
        carry_ref[...] = jnp.zeros_like(carry_ref)

    on = []
    for h in range(GDN_HEADS):
        sl = slice(h * GDN_HEAD_DIM, (h + 1) * GDN_HEAD_DIM)
        dz = dz_ref[:, sl].astype(F32)
        on.append(_rms(og_ref[:, sl].astype(F32), gon_ref[...]) * (dz * _sigmoid(dz)))
    on = jnp.concatenate(on, axis=1).astype(BF16)
    y_fox = _dot(yf_ref[...], wof_ref[...])
    y_delta = _dot(on, wod_ref[...])
    merged = _sigmoid(gf_ref[...].astype(F32)) * y_fox + _sigmoid(gd_ref[...].astype(F32)) * y_delta
    x1 = x_ref[...] + _dot(merged.astype(BF16), wout_ref[...])
    x1_ref[...] = x1
    t32 = _rms(x1, gffn_ref[...])
    th = t32.astype(BF16)
    tl = (t32 - th.astype(F32)).astype(BF16)
    _rows_to_tiles(t_ref, t32)
    logits = _dot(th, wrh_ref[...]) + _dot(tl, wrh_ref[...]) + _dot(th, wrl_ref[...]) + br_ref[...]
    lane = lax.broadcasted_iota(jnp.int32, logits.shape, 1)
    gl = jnp.where(lane < N_GROUPS, logits, NEG)
    gmax = jnp.max(gl, axis=-1, keepdims=True)
    g_sel = jnp.min(jnp.where(gl == gmax, lane, LANES), axis=-1, keepdims=True)
    p_sel = 1.0 / jnp.sum(jnp.exp(gl - gmax), axis=-1, keepdims=True)
    lo = EXPERT_LANE0 + EXPERTS_PER_GROUP * g_sel
    in_grp = jnp.logical_and(lane >= lo, lane < lo + EXPERTS_PER_GROUP)
    el = jnp.where(in_grp, logits, NEG)
    emax = jnp.max(el, axis=-1, keepdims=True)
    ee = jnp.where(in_grp, jnp.exp(el - emax), 0.0)
    pe = ee / jnp.sum(ee, axis=-1, keepdims=True)
    pe = jnp.where(in_grp, pe, -1.0)
    p1 = jnp.max(pe, axis=-1, keepdims=True)
    i1 = jnp.min(jnp.where(pe == p1, lane, LANES), axis=-1, keepdims=True)
    pe2 = jnp.where(lane == i1, -1.0, pe)
    p2 = jnp.max(pe2, axis=-1, keepdims=True)
    i2 = jnp.min(jnp.where(pe2 == p2, lane, LANES), axis=-1, keepdims=True)
    den = p1 + p2
    hit1 = lane == i1
    hit2 = lane == i2
    assign = jnp.where(jnp.logical_or(hit1, hit2), 1.0, 0.0)
    r = lax.broadcasted_iota(jnp.int32, (tc, tc), 0)
    c = lax.broadcasted_iota(jnp.int32, (tc, tc), 1)
    before = jnp.where(r > c, 1.0, 0.0).astype(BF16)
    prefix = _dot(before, assign.astype(BF16)) + carry_ref[...]
    rank1 = jnp.sum(jnp.where(hit1, prefix, 0.0), axis=-1, keepdims=True)
    rank2 = jnp.sum(jnp.where(hit2, prefix, 0.0), axis=-1, keepdims=True)
    carry_ref[...] = prefix[tc - 1:tc, :] + assign[tc - 1:tc, :]
    cnt_ref[...] = carry_ref[...]
    cols = ((i1 - EXPERT_LANE0).astype(F32), (i2 - EXPERT_LANE0).astype(F32), rank1, rank2,
            p_sel * (p1 / den), p_sel * (p2 / den))
    rt = jnp.zeros(logits.shape, F32)
    for k, v in enumerate(cols):
        rt = jnp.where(lane == k, v, rt)
    rt_ref[...] = rt


def _combine(yf, og, dz, gf, gd, x2d, wof, wod, wout, g_on, g_ffn, wrh, wrl, br, tc):
    t, d = x2d.shape
    row = lambda i: (i, 0)
    const = lambda i: (0, 0)
    acts = (yf, og, dz, gf, gd, x2d)
    consts = (wof, wod, wout, g_on, g_ffn, wrh, wrl, br)
    return pl.pallas_call(
        functools.partial(_combine_body, tc=tc),
        grid=(t // tc,),
        in_specs=[pl.BlockSpec((tc, a.shape[1]), row) for a in acts]
        + [pl.BlockSpec(c.shape, const) for c in consts],
        out_specs=[pl.BlockSpec((tc, d), row), pl.BlockSpec((tc * (d // LANES), LANES), row),
                   pl.BlockSpec((tc, LANES), row), pl.BlockSpec((1, LANES), const)],
        out_shape=[jax.ShapeDtypeStruct((t, d), F32), jax.ShapeDtypeStruct((t * (d // LANES), LANES), F32),
                   jax.ShapeDtypeStruct((t, LANES), F32), jax.ShapeDtypeStruct((1, LANES), F32)],
        scratch_shapes=[pltpu.VMEM((1, LANES), F32)],
        compiler_params=_cparams(("arbitrary",)),
        name="combine",
    )(*acts, *consts)


def _slots_body(rt_ref, cnt_ref, o_ref, *, tmx):
    lane1 = lax.broadcasted_iota(jnp.int32, (1, LANES), 1)
    is_expert = jnp.logical_and(lane1 >= EXPERT_LANE0, lane1 < EXPERT_LANE0 + N_EXPERTS)
    padded = jnp.where(is_expert, jnp.floor((cnt_ref[...] + (tmx - 1)) / tmx) * tmx, 0.0)
    r = lax.broadcasted_iota(jnp.int32, (LANES, LANES), 0)
    c = lax.broadcasted_iota(jnp.int32, (LANES, LANES), 1)
    before = jnp.where(r < c, 1.0, 0.0).astype(BF16)
    hi, mid, lo = _split3(jnp.broadcast_to(padded, (8, LANES)))
    offs = (_dot(hi, before) + _dot(mid, before) + _dot(lo, before))[0:1, :]
    rt = rt_ref[...]
    lane = lax.broadcasted_iota(jnp.int32, rt.shape, 1)
    expert_of_lane = (lane - EXPERT_LANE0).astype(F32)
    out = jnp.zeros(rt.shape, F32)
    for k, (ce, cr) in enumerate(((RT_E1, RT_RANK1), (RT_E2, RT_RANK2))):
        start = jnp.sum(jnp.where(expert_of_lane == rt[:, ce:ce + 1], offs, 0.0), axis=-1, keepdims=True)
        out = jnp.where(lane == k, start + rt[:, cr:cr + 1], out)
    o_ref[...] = out.T[:8, :].astype(jnp.int32)


def _slots(rt, cnt, tmx, ts):
    t = rt.shape[0]
    return pl.pallas_call(
        functools.partial(_slots_body, tmx=tmx),
        grid=(t // ts,),
        in_specs=[pl.BlockSpec((ts, LANES), lambda i: (i, 0)), pl.BlockSpec((1, LANES), lambda i: (0, 0))],
        out_specs=pl.BlockSpec((8, ts), lambda i: (0, i)),
        out_shape=jax.ShapeDtypeStruct((8, t), jnp.int32),
        compiler_params=_cparams(("parallel",)),
        name="slots",
    )(rt, cnt)


ROW_GROUP = 8


def _issue_row_copies(pos_ref, n_rows, nsub, make_copy):
    def issue(g, carry):
        r0 = g * ROW_GROUP
        slots = [[pos_ref[0, 0, k * n_rows + r0 + u] for k in range(2)] for u in range(ROW_GROUP)]
        for u in range(ROW_GROUP):
            tok = pl.ds(pl.multiple_of((r0 + u) * nsub, nsub), nsub)
            for k in range(2):
                slot = pl.ds(pl.multiple_of(slots[u][k] * nsub, nsub), nsub)
                make_copy(k, tok, slot).start(priority=k)
        return carry

    lax.fori_loop(0, n_rows // ROW_GROUP, issue, 0)


def _dispatch_body(zflag_ref, pos_ref, t_ref, xs_ref, zbuf, sem, zsem, *, td, tmx, n_tiles, nsub):
    @pl.when(pl.program_id(0) == 0)
    def _():
        zbuf[...] = jnp.zeros_like(zbuf)
        rows = tmx * nsub

        def zero_tile(k, carry):
            @pl.when(zflag_ref[k] != 0)
            def _():
                cp = pltpu.make_async_copy(zbuf, xs_ref.at[pl.ds(pl.multiple_of(k * rows, rows), rows), :], zsem)
                cp.start()
                cp.wait()

            return carry

        lax.fori_loop(0, n_tiles, zero_tile, 0)

    _issue_row_copies(pos_ref, td, nsub,
                      lambda k, tok, slot: pltpu.make_async_copy(t_ref.at[tok, :], xs_ref.at[slot, :], sem))
    for _ in range(2):
        pltpu.make_async_copy(t_ref, xs_ref.at[pl.ds(0, td * nsub), :], sem).wait()


def _dispatch(zflag, t_tiles, pos12, n_slots, td, tmx, nsub):
    t = t_tiles.shape[0] // nsub
    n_tiles = n_slots // tmx
    grid_spec = pltpu.PrefetchScalarGridSpec(
        num_scalar_prefetch=1,
        grid=(t // td,),
        in_specs=[pl.BlockSpec((1, 1, 2 * td), lambda i, zf: (i, 0, 0), memory_space=pltpu.SMEM),
                  pl.BlockSpec((td * nsub, LANES), lambda i, zf: (i, 0))],
        out_specs=pl.BlockSpec(memory_space=pl.ANY),
        scratch_shapes=[pltpu.VMEM((tmx * nsub, LANES), F32), pltpu.SemaphoreType.DMA(()),
                        pltpu.SemaphoreType.DMA(())],
    )
    return pl.pallas_call(
        functools.partial(_dispatch_body, td=td, tmx=tmx, n_tiles=n_tiles, nsub=nsub),
        grid_spec=grid_spec,
        out_shape=jax.ShapeDtypeStruct((n_slots * nsub, LANES), F32),
        compiler_params=_cparams(("arbitrary",)),
        name="dispatch",
    )(zflag, pos12, t_tiles)


def _experts_body(te_ref, nused_ref, xs_ref, wgu_ref, wd_ref, ys_ref, *, tmx):
    @pl.when(pl.program_id(0) < nused_ref[0])
    def _():
        hgu = _dot(_tiles_to_rows(xs_ref, tmx).astype(BF16), wgu_ref[0])
        a = hgu[:, :EXPERT_FF]
        hid = a * _sigmoid(a) * hgu[:, EXPERT_FF:]
        _rows_to_tiles(ys_ref, _dot(hid.astype(BF16), wd_ref[0]))

    @pl.when(pl.program_id(0) >= nused_ref[0])
    def _():
        ys_ref[...] = jnp.zeros_like(ys_ref)


def _experts(tile_expert, n_used, xs, wgu, wd, tmx):
    d = wgu.shape[1]
    nsub = d // LANES
    n_slots = xs.shape[0] // nsub
    grid_spec = pltpu.PrefetchScalarGridSpec(
        num_scalar_prefetch=2,
        grid=(n_slots // tmx,),
        in_specs=[pl.BlockSpec((tmx * nsub, LANES), lambda i, te, nu: (i, 0)),
                  pl.BlockSpec((1, d, 2 * EXPERT_FF), lambda i, te, nu: (te[i], 0, 0)),
                  pl.BlockSpec((1, EXPERT_FF, d), lambda i, te, nu: (te[i], 0, 0))],
        out_specs=pl.BlockSpec((tmx * nsub, LANES), lambda i, te, nu: (i, 0)),
    )
    return pl.pallas_call(
        functools.partial(_experts_body, tmx=tmx),
        grid_spec=grid_spec,
        out_shape=jax.ShapeDtypeStruct(xs.shape, F32),
        compiler_params=_cparams(("arbitrary",)),
        name="experts",
    )(tile_expert, n_used, xs, wgu, wd)


def _final_body(pos_ref, posn_ref, x_ref, rt_ref, p_ref, gple_ref, wpg_ref, wpp_ref, gfin_ref, ys_ref,
                o_ref, gbuf, sems, *, tf, nsub):
    i = pl.program_id(0)
    n = pl.num_programs(0)
    slot = i % 2

    def gather(pref, dst_slot):
        _issue_row_copies(pref, tf, nsub, lambda k, tok, src: pltpu.make_async_copy(
            ys_ref.at[src, :], gbuf.at[dst_slot, k, tok, :], sems.at[dst_slot]))

    @pl.when(i == 0)
    def _():
        gather(pos_ref, 0)

    @pl.when(i + 1 < n)
    def _():
        gather(posn_ref, 1 - slot)

    for k in range(2):
        pltpu.make_async_copy(ys_ref.at[pl.ds(0, tf * nsub), :], gbuf.at[slot, k], sems.at[slot]).wait()

    rt = rt_ref[...]
    g1 = rt[:, RT_G1:RT_G1 + 1]
    g2 = rt[:, RT_G2:RT_G2 + 1]
    x = x_ref[...] + g1 * _tiles_to_rows(gbuf.at[slot, 0], tf) + g2 * _tiles_to_rows(gbuf.at[slot, 1], tf)
    r = _rms(x, gple_ref[...]).astype(BF16)
    ple_gate = _sigmoid(_dot(r, wpg_ref[...]))
    proj = _dot(p_ref[...].astype(BF16), wpp_ref[...])
    o_ref[...] = _rms(x + ple_gate * proj, gfin_ref[...])


def _final(pos12, x1, rt, p2d, g_ple, wpg, wpp, g_final, ys, tf):
    t, d = x1.shape
    nsub = d // LANES
    n = t // tf
    row = lambda i: (i, 0)
    const = lambda i: (0, 0)
    smem_blk = lambda imap: pl.BlockSpec((1, 1, 2 * tf), imap, memory_space=pltpu.SMEM)
    return pl.pallas_call(
        functools.partial(_final_body, tf=tf, nsub=nsub),
        grid=(n,),
        in_specs=[smem_blk(lambda i: (i, 0, 0)), smem_blk(lambda i: (jnp.minimum(i + 1, n - 1), 0, 0)),
                  pl.BlockSpec((tf, d), row), pl.BlockSpec((tf, LANES), row),
                  pl.BlockSpec((tf, p2d.shape[1]), row),
                  pl.BlockSpec((1, d), const), pl.BlockSpec(wpg.shape, const),
                  pl.BlockSpec(wpp.shape, const), pl.BlockSpec((1, d), const),
                  pl.BlockSpec(memory_space=pl.ANY)],
        out_specs=pl.BlockSpec((tf, d), row),
        out_shape=jax.ShapeDtypeStruct((t, d), F32),
        scratch_shapes=[pltpu.VMEM((2, 2, tf * nsub, LANES), F32), pltpu.SemaphoreType.DMA((2,))],
        compiler_params=_cparams(("arbitrary",)),
        name="final",
    )(pos12, pos12, x1, rt, p2d, g_ple, wpg, wpp, g_final, ys)


def _pad_lanes(v, lane0):
    return jnp.zeros((1, LANES), F32).at[0, lane0:lane0 + v.shape[0]].set(v.astype(F32))


def _tile(n, pref):
    return pref if n % pref == 0 else n


def _layer(x, p_l, g_mix, w_in, b_forget, conv_w, a_log, dt_bias, g_onorm, w_o_fox, w_o_delta, w_out,
           g_ffn, w_group, b_group, w_router, b_router, w_gate, w_up, w_down, g_ple, w_ple_gate, w_ple_proj,
           g_post, apply_post):
    b, s, d = x.shape
    t = b * s
    x2d = x.reshape(t, d)

    o_ff = 3 * FOX_WIDTH
    o_qkv = o_ff + FOX_HEADS
    o_da = o_qkv + 3 * GDN_WIDTH
    o_db = o_da + GDN_HEADS
    o_dz = o_db + GDN_HEADS
    o_gf = o_dz + GDN_WIDTH
    o_gd = o_gf + d
    w_small = jnp.concatenate(
        [w_in[:, o_ff:o_qkv], w_in[:, o_da:o_db], w_in[:, o_db:o_dz],
         jnp.zeros((d, LANES - FOX_HEADS - 2 * GDN_HEADS), w_in.dtype)], axis=1)
    wq = w_in[:, :o_ff].astype(BF16)
    wg = w_in[:, o_qkv:o_da].astype(BF16)
    wzs = jnp.concatenate([w_in[:, o_dz:o_gf], w_small], axis=1).astype(BF16)
    wgf = w_in[:, o_gf:o_gd].astype(BF16)
    wgd = w_in[:, o_gd:].astype(BF16)

    zq, zg, dz, small, gf, gd = _inproj(x2d, g_mix.reshape(1, d), wq, wg, wzs, wgf, wgd, _tile(t, 512))

    par = jnp.concatenate(
        [_pad_lanes(b_forget, 0) + _pad_lanes(dt_bias, FOX_HEADS), _pad_lanes(a_log, FOX_HEADS),
         jnp.zeros((6, LANES), F32)], axis=0)
    qn, kn, vv, col = _gdnprep(zg.reshape(b, s, 3 * GDN_WIDTH), conv_w.astype(F32),
                               small.reshape(b, s, LANES), par, _tile(s, 512))

    nc = s // GDN_CHUNK
    gc = col[:, :, FOX_HEADS:FOX_HEADS + GDN_HEADS].reshape(b, nc, GDN_CHUNK, GDN_HEADS)
    gcrow = jnp.transpose(gc, (0, 1, 3, 2)).reshape(b, nc, 1, GDN_HEADS * GDN_CHUNK)

    y_fox = _fox(zq.reshape(b, s, 3 * FOX_WIDTH), col, _tile(s, 1024))
    o_gdn = _gdn(qn, kn, vv, col, gcrow, _tile(s, 512))

    w_r = jnp.concatenate([w_group, w_router,
                           jnp.zeros((d, LANES - N_GROUPS - N_EXPERTS), w_group.dtype)], axis=1).astype(F32)
    wrh = w_r.astype(BF16)
    wrl = (w_r - wrh.astype(F32)).astype(BF16)
    br = _pad_lanes(b_group, 0) + _pad_lanes(b_router, EXPERT_LANE0)
    x1, tpk, rt, cnt = _combine(
        y_fox.reshape(t, FOX_WIDTH), o_gdn.reshape(t, GDN_WIDTH), dz, gf, gd, x2d,
        w_o_fox.astype(BF16), w_o_delta.astype(BF16), w_out.astype(BF16),
        g_onorm.reshape(1, GDN_HEAD_DIM).astype(F32), g_ffn.reshape(1, d).astype(F32), wrh, wrl, br,
        _tile(t, 512))

    tmx = MOE_SLOT_TILE
    n_slots = 2 * t + N_EXPERTS * tmx
    counts = cnt[0, EXPERT_LANE0:EXPERT_LANE0 + N_EXPERTS].astype(jnp.int32)
    padded = (counts + tmx - 1) // tmx * tmx
    ends = jnp.cumsum(padded)
    pos = _slots(rt, cnt, tmx, _tile(t, 2048))
    pos1, pos2 = pos[0], pos[1]
    tile_start = jnp.arange(n_slots // tmx, dtype=jnp.int32) * tmx
    tile_expert = jnp.minimum(jnp.sum(tile_start[:, None] >= ends[None, :], axis=1), N_EXPERTS - 1).astype(jnp.int32)
    n_used = (ends[-1:] // tmx).astype(jnp.int32)
    last_of_expert = jnp.any(jnp.logical_and(tile_start[:, None] + tmx == ends[None, :], padded[None, :] > 0), axis=1)
    zflag = jnp.logical_or(last_of_expert, tile_start >= ends[-1]).astype(jnp.int32)

    def tiled_pos(tile):
        n = t // tile
        return jnp.concatenate([pos1.reshape(n, 1, tile), pos2.reshape(n, 1, tile)], axis=2)

    td = _tile(t, 1024)
    nsub = d // LANES
    xs = _dispatch(zflag, tpk, tiled_pos(td), n_slots, td, tmx, nsub)
    wgu = jnp.concatenate([w_gate, w_up], axis=2).astype(BF16)
    ys = _experts(tile_expert, n_used, xs, wgu, w_down.astype(BF16), tmx)

    tf = _tile(t, 256)
    out = _final(tiled_pos(tf), x1, rt, p_l.reshape(t, -1), g_ple.reshape(1, d).astype(F32),
                 w_ple_gate.astype(BF16), w_ple_proj.astype(BF16), g_post.reshape(1, d).astype(F32), ys, tf)
    return out.reshape(b, s, d)


def kernel(x, p, g_mix, w_in, b_forget, conv_w, a_log, dt_bias, g_onorm, w_o_fox, w_o_delta, w_out,
           g_ffn, w_group, b_group, w_router, b_router, w_gate, w_up, w_down, g_ple, w_ple_gate, w_ple_proj,
           g_final):
    depth = p.shape[0]
    assert depth == 1, "the final rmsnorm is fused into the last layer's epilogue; depth 1 only"
    i = 0
    return _layer(x, p[i], g_mix[i], w_in[i], b_forget[i], conv_w[i], a_log[i], dt_bias[i], g_onorm[i],
                  w_o_fox[i], w_o_delta[i], w_out[i], g_ffn[i], w_group[i], b_group[i], w_router[i],
                  b_router[i], w_gate[i], w_up[i], w_down[i], g_ple[i], w_ple_gate[i], w_ple_proj[i],
                  g_final, True)
```

```python
import functools

import jax
import jax.numpy as jnp
from jax import lax
from jax.experimental import pallas as pl
from jax.experimental.pallas import tpu as pltpu

F32 = jnp.float32
BF16 = jnp.bfloat16
EPS = 1e-6
NEG = -1e30

FOX_HEADS = 8
FOX_HEAD_DIM = 64
FOX_WIDTH = FOX_HEADS * FOX_HEAD_DIM
GDN_HEADS = 4
GDN_HEAD_DIM = 128
GDN_CONV = 4
GDN_CHUNK = 64
GDN_WIDTH = GDN_HEADS * GDN_HEAD_DIM
N_GROUPS = 4
EXPERTS_PER_GROUP = 8
N_EXPERTS = N_GROUPS * EXPERTS_PER_GROUP
EXPERT_FF = 256
LANES = 128
EXPERT_LANE0 = N_GROUPS

VMEM_LIMIT_BYTES = 56 * 1024 * 1024


def _cparams(sem):
    return pltpu.CompilerParams(dimension_semantics=sem, vmem_limit_bytes=VMEM_LIMIT_BYTES)


def _rms(x, g):
    return x * lax.rsqrt(jnp.mean(x * x, axis=-1, keepdims=True) + EPS) * g


def _sigmoid(x):
    return 1.0 / (1.0 + jnp.exp(-x))


def _dot(a, b):
    return jnp.dot(a, b, preferred_element_type=F32)


def _dot_nt(a, b):
    return lax.dot_general(a, b, (((1,), (1,)), ((), ())), preferred_element_type=F32)


def _dot_tn(a, b):
    return lax.dot_general(a, b, (((0,), (0,)), ((), ())), preferred_element_type=F32)


def _split3(v):
    hi = v.astype(BF16)
    r = v - hi.astype(F32)
    mid = r.astype(BF16)
    lo = (r - mid.astype(F32)).astype(BF16)
    return hi, mid, lo


def _dot_exact_lhs01(mat01, v):
    hi, mid, lo = _split3(v)
    return _dot(mat01, hi) + _dot(mat01, mid) + _dot(mat01, lo)


def _inproj_body(x_ref, g_ref, wq_ref, wg_ref, wzs_ref, wgf_ref, wgd_ref,
                 oq_ref, og_ref, oz_ref, osm_ref, ogf_ref, ogd_ref):
    hb = _rms(x_ref[...], g_ref[...]).astype(BF16)
    oq_ref[...] = _dot(hb, wq_ref[...]).astype(BF16)
    og_ref[...] = _dot(hb, wg_ref[...]).astype(BF16)
    zs = _dot(hb, wzs_ref[...])
    oz_ref[...] = zs[:, :GDN_WIDTH].astype(BF16)
    osm_ref[...] = zs[:, GDN_WIDTH:]
    ogf_ref[...] = _dot(hb, wgf_ref[...]).astype(BF16)
    ogd_ref[...] = _dot(hb, wgd_ref[...]).astype(BF16)


def _inproj(x2d, g_mix, wq, wg, wzs, wgf, wgd, tm):
    t, d = x2d.shape
    row = lambda i: (i, 0)
    const = lambda i: (0, 0)
    widths = (3 * FOX_WIDTH, 3 * GDN_WIDTH, GDN_WIDTH, LANES, d, d)
    dtypes = (BF16, BF16, BF16, F32, BF16, BF16)
    return pl.pallas_call(
        _inproj_body,
        grid=(t // tm,),
        in_specs=[pl.BlockSpec((tm, d), row), pl.BlockSpec((1, d), const)]
        + [pl.BlockSpec(w.shape, const) for w in (wq, wg, wzs, wgf, wgd)],
        out_specs=[pl.BlockSpec((tm, n), row) for n in widths],
        out_shape=[jax.ShapeDtypeStruct((t, n), dt) for n, dt in zip(widths, dtypes)],
        compiler_params=_cparams(("parallel",)),
        name="inproj",
    )(x2d, g_mix, wq, wg, wzs, wgf, wgd)


def _gates_tile(sm, par_ref, carry_ref):
    tp = sm.shape[0]
    lane = lax.broadcasted_iota(jnp.int32, sm.shape, 1)
    z = sm + par_ref[0:1, :]
    soft = jnp.log1p(jnp.exp(-jnp.abs(z)))
    softplus = jnp.maximum(z, 0.0) + soft
    logf = jnp.minimum(z, 0.0) - soft
    g = -jnp.exp(par_ref[1:2, :]) * softplus
    beta = _sigmoid(sm)
    is_f = lane < FOX_HEADS
    is_g = jnp.logical_and(lane >= FOX_HEADS, lane < FOX_HEADS + GDN_HEADS)
    is_b = jnp.logical_and(lane >= FOX_HEADS + GDN_HEADS, lane < FOX_HEADS + 2 * GDN_HEADS)
    val = jnp.where(is_f, logf, jnp.where(is_g, g, 0.0))
    r = lax.broadcasted_iota(jnp.int32, (tp, tp), 0)
    c = lax.broadcasted_iota(jnp.int32, (tp, tp), 1)
    lower = r >= c
    tri = jnp.where(lower, 1.0, 0.0).astype(BF16)
    same_chunk = (r // GDN_CHUNK) == (c // GDN_CHUNK)
    tri_chunk = jnp.where(jnp.logical_and(lower, same_chunk), 1.0, 0.0).astype(BF16)
    cum = _dot_exact_lhs01(tri, val) + carry_ref[...]
    gcs = _dot_exact_lhs01(tri_chunk, val)
    carry_ref[...] = cum[tp - 1:tp, :]
    return jnp.where(is_f, cum, jnp.where(is_g, gcs, jnp.where(is_b, beta, 0.0)))


LOG2E = 1.4426950408889634


FOX_VROWS = FOX_HEAD_DIM + 16


def _split3_f32(v):
    hi = v.astype(BF16).astype(F32)
    r = v - hi
    mid = r.astype(BF16).astype(F32)
    lo = (r - mid).astype(BF16).astype(F32)
    return hi, mid, lo


def _fox_body(q_ref, k_ref, v_ref, col_ref, o_ref, kx_ref, vt_ref, sa_ref, sb_ref, acc_ref, m_ref,
              *, tq, s_len):
    hp = pl.program_id(1)
    qi = pl.program_id(2)
    hd = FOX_HEAD_DIM

    @pl.when(qi == 0)
    def _():
        def prep(ci, carry):
            r0 = pl.multiple_of(ci * tq, tq)
            kx_ref[pl.ds(r0, tq), :LANES] = k_ref[0, pl.ds(r0, tq), :]
            col = col_ref[0, pl.ds(r0, tq), :]
            lane = lax.broadcasted_iota(jnp.int32, col.shape, 1)
            bias = jnp.zeros(col.shape, F32)
            for hh in range(2):
                ck = jnp.sum(jnp.where(lane == 2 * hp + hh, col, 0.0), axis=-1, keepdims=True) * LOG2E
                for k, piece in enumerate(_split3_f32(ck)):
                    bias = jnp.where(lane == 3 * hh + k, piece, bias)
            kx_ref[pl.ds(r0, tq), LANES:] = bias.astype(BF16)
            vt = v_ref[0, pl.ds(r0, tq), :].astype(F32).T
            ones_row = jnp.where(lax.broadcasted_iota(jnp.int32, (FOX_VROWS - hd, tq), 0) == 0, 1.0, 0.0)
            for hh in range(2):
                vt_ref[hh, :hd, pl.ds(r0, tq)] = vt[hh * hd:(hh + 1) * hd].astype(BF16)
                vt_ref[hh, hd:, pl.ds(r0, tq)] = ones_row.astype(BF16)
            return carry

        lax.fori_loop(0, s_len // tq, prep, 0)

    lane = lax.broadcasted_iota(jnp.int32, (tq, LANES), 1)
    q = (q_ref[0].astype(F32) * (hd ** -0.5 * LOG2E)).astype(BF16)
    zero = jnp.zeros_like(q)
    qx = []
    for hh in range(2):
        qh = jnp.where(jnp.logical_and(lane >= hh * hd, lane < (hh + 1) * hd), q, zero)
        sel = jnp.where(jnp.logical_and(lane >= 3 * hh, lane < 3 * hh + 3), -1.0, 0.0).astype(BF16)
        qx.append(jnp.concatenate([qh, sel], axis=1))
    m_ref[...] = jnp.full(m_ref.shape, NEG, F32)
    acc_ref[...] = jnp.zeros(acc_ref.shape, F32)

    def scores(j, dst_ref):
        start = pl.multiple_of(j * tq, tq)
        kx = kx_ref[pl.ds(start, tq), :]
        for hh in range(2):
            dst_ref[hh] = _dot_nt(kx, qx[hh])

    def consume(j, src_ref, masked):
        start = pl.multiple_of(j * tq, tq)
        for hh in range(2):
            s = src_ref[hh]
            if masked:
                key = lax.broadcasted_iota(jnp.int32, (tq, tq), 0)
                qry = lax.broadcasted_iota(jnp.int32, (tq, tq), 1)
                s = jnp.where(key <= qry, s, NEG)
            m_old = m_ref[hh:hh + 1, :]
            m_new = jnp.maximum(m_old, jnp.max(s, axis=0, keepdims=True))
            alpha = jnp.exp2(m_old - m_new)
            p = jnp.exp2(s - m_new).astype(BF16)
            acc_ref[hh] = alpha * acc_ref[hh] + _dot(vt_ref[hh, :, pl.ds(start, tq)], p)
            m_ref[hh:hh + 1, :] = m_new

    scores(0, sa_ref)

    def loop_body(j, carry):
        @pl.when(j % 2 == 0)
        def _():
            scores(j + 1, sb_ref)
            consume(j, sa_ref, False)

        @pl.when(j % 2 == 1)
        def _():
            scores(j + 1, sa_ref)
            consume(j, sb_ref, False)

        return carry

    lax.fori_loop(0, qi, loop_body, 0)

    @pl.when(qi % 2 == 0)
    def _():
        consume(qi, sa_ref, True)

    @pl.when(qi % 2 == 1)
    def _():
        consume(qi, sb_ref, True)

    out_t = jnp.concatenate([acc_ref[hh, :hd] / acc_ref[hh, hd:hd + 1] for hh in range(2)], axis=0)
    o_ref[0] = out_t.T.astype(BF16)


def _fox(zq3d, col3d, tq):
    b, s, _ = zq3d.shape
    npair = FOX_HEADS // 2
    kblk = FOX_WIDTH // LANES
    return pl.pallas_call(
        functools.partial(_fox_body, tq=tq, s_len=s),
        grid=(b, npair, s // tq),
        in_specs=[pl.BlockSpec((1, tq, LANES), lambda bi, hp, qi: (bi, qi, hp)),
                  pl.BlockSpec((1, s, LANES), lambda bi, hp, qi: (bi, 0, kblk + hp)),
                  pl.BlockSpec((1, s, LANES), lambda bi, hp, qi: (bi, 0, 2 * kblk + hp)),
                  pl.BlockSpec((1, s, LANES), lambda bi, hp, qi: (bi, 0, 0))],
        out_specs=pl.BlockSpec((1, tq, LANES), lambda bi, hp, qi: (bi, qi, hp)),
        out_shape=jax.ShapeDtypeStruct((b, s, FOX_WIDTH), BF16),
        scratch_shapes=[pltpu.VMEM((s, 2 * LANES), BF16), pltpu.VMEM((2, FOX_VROWS, s), BF16),
                        pltpu.VMEM((2, tq, tq), F32), pltpu.VMEM((2, tq, tq), F32),
                        pltpu.VMEM((2, FOX_VROWS, tq), F32), pltpu.VMEM((2, tq), F32)],
        compiler_params=_cparams(("parallel", "parallel", "arbitrary")),
        name="fox",
    )(zq3d, zq3d, zq3d, col3d)


HALO = 16


def _gdnprep_body(x_ref, halo_ref, cw_ref, sm_ref, par_ref, oq_ref, ok_ref, ov_ref, ocol_ref, ext_ref, carry_ref,
                  *, tp):
    @pl.when(pl.program_id(1) == 0)
    def _():
        carry_ref[...] = jnp.zeros_like(carry_ref)

    ocol_ref[0] = _gates_tile(sm_ref[0], par_ref, carry_ref)
    prev = halo_ref[0].astype(F32)
    ext_ref[0:HALO, :] = jnp.where(pl.program_id(1) > 0, prev, 0.0)
    ext_ref[HALO:, :] = x_ref[0].astype(F32)
    acc = cw_ref[GDN_CONV - 1:GDN_CONV, :] * ext_ref[HALO:HALO + tp, :]
    for j in range(GDN_CONV - 1):
        off = HALO - (GDN_CONV - 1) + j
        acc = acc + cw_ref[j:j + 1, :] * ext_ref[off:off + tp, :]
    y = acc * _sigmoid(acc)

    def l2(v):
        return v * lax.rsqrt(jnp.sum(v * v, axis=-1, keepdims=True) + EPS)

    for h in range(GDN_HEADS):
        lo, hi = h * GDN_HEAD_DIM, (h + 1) * GDN_HEAD_DIM
        oq_ref[0, :, lo:hi] = (l2(y[:, lo:hi]) * GDN_HEAD_DIM ** -0.5).astype(BF16)
        ok_ref[0, :, lo:hi] = l2(y[:, GDN_WIDTH + lo:GDN_WIDTH + hi]).astype(BF16)
    ov_ref[0] = y[:, 2 * GDN_WIDTH:].astype(BF16)


def _gdnprep(zg3d, conv_w, small3d, par, tp):
    b, s, c = zg3d.shape
    blk = lambda bi, i: (bi, i, 0)
    return pl.pallas_call(
        functools.partial(_gdnprep_body, tp=tp),
        grid=(b, s // tp),
        in_specs=[pl.BlockSpec((1, tp, c), blk),
                  pl.BlockSpec((1, HALO, c), lambda bi, i: (bi, jnp.maximum(i * (tp // HALO) - 1, 0), 0)),
                  pl.BlockSpec(conv_w.shape, lambda bi, i: (0, 0)),
                  pl.BlockSpec((1, tp, LANES), blk), pl.BlockSpec(par.shape, lambda bi, i: (0, 0))],
        out_specs=[pl.BlockSpec((1, tp, GDN_WIDTH), blk)] * 3 + [pl.BlockSpec((1, tp, LANES), blk)],
        out_shape=[jax.ShapeDtypeStruct((b, s, GDN_WIDTH), BF16)] * 3 + [jax.ShapeDtypeStruct((b, s, LANES), F32)],
        scratch_shapes=[pltpu.VMEM((tp + HALO, c), F32), pltpu.VMEM((1, LANES), F32)],
        compiler_params=_cparams(("parallel", "arbitrary")),
        name="gdnprep",
    )(zg3d, zg3d, conv_w, small3d, par)


def _stack_heads(x):
    return jnp.concatenate([x[:, h * GDN_HEAD_DIM:(h + 1) * GDN_HEAD_DIM] for h in range(GDN_HEADS)], axis=0)


def _gdn_body(q_ref, k_ref, v_ref, col_ref, grow_ref, o_ref, state_ref, *bufs, tg):
    C = GDN_CHUNK
    R = GDN_HEADS * C
    dh = GDN_HEAD_DIM
    n_chunks = tg // C
    step = pl.program_id(1)
    buf_sets = (bufs[:len(bufs) // 2], bufs[len(bufs) // 2:])

    @pl.when(step == 0)
    def _():
        for ref in (state_ref,) + tuple(bufs):
            ref[...] = jnp.zeros_like(ref)

    r = lax.broadcasted_iota(jnp.int32, (R, R), 0)
    c = lax.broadcasted_iota(jnp.int32, (R, R), 1)
    same_head = (r // C) == (c // C)
    lower = jnp.logical_and(same_head, r >= c)
    strict = jnp.logical_and(same_head, r > c)
    gc_lane0 = FOX_HEADS
    beta_lane0 = FOX_HEADS + GDN_HEADS

    def advance(ci, rd):
        u_ref, w_ref, intra_ref, qd_ref, kd_ref, gl_ref = rd
        r0 = ci * C
        u, w, intra = u_ref[ci], w_ref[ci], intra_ref[ci]
        q_dec, k_dec = qd_ref[ci], kd_ref[ci]
        v_new = []
        o_state = []
        for h in range(GDN_HEADS):
            sl = slice(h * C, (h + 1) * C)
            st = state_ref[h].astype(BF16)
            v_new.append(u[sl] - _dot(w[sl], st))
            o_state.append(_dot(q_dec[sl], st))
        v_new = jnp.concatenate(v_new, axis=0)
        v_new_b = v_new.astype(BF16)
        o_all = jnp.concatenate(o_state, axis=0) + _dot(intra, v_new_b)
        for h in range(GDN_HEADS):
            sl = slice(h * C, (h + 1) * C)
            state_ref[h] = state_ref[h] * gl_ref[ci, h:h + 1, :] + _dot_tn(k_dec[sl], v_new_b[sl])
        o_ref[0, r0:r0 + C, :] = jnp.concatenate(
            [o_all[h * C:(h + 1) * C] for h in range(GDN_HEADS)], axis=1).astype(BF16)

    chunks = range(n_chunks)

    def prepare_all(wr):
        u_ref, w_ref, intra_ref, qd_ref, kd_ref, gl_ref = wr
        qs, ks, vs, gc_col, beta_col, gl_row, lmat, intra = [], [], [], [], [], [], [], []
        for ci in chunks:
            r0 = ci * C
            qs.append(_stack_heads(q_ref[0, r0:r0 + C, :]).astype(F32))
            ks.append(_stack_heads(k_ref[0, r0:r0 + C, :]).astype(F32))
            vs.append(_stack_heads(v_ref[0, r0:r0 + C, :]).astype(F32))
            col = col_ref[0, r0:r0 + C, :]
            gc_col.append(
                jnp.concatenate([col[:, gc_lane0 + h:gc_lane0 + h + 1] for h in range(GDN_HEADS)], axis=0))
            beta_col.append(
                jnp.concatenate([col[:, beta_lane0 + h:beta_lane0 + h + 1] for h in range(GDN_HEADS)], axis=0))
            gl_row.append(col[C - 1:C, :])
            gc_row = grow_ref[0, ci, :, :]
            decay = jnp.exp(jnp.where(lower, gc_col[ci] - gc_row, NEG))
            ksb = ks[ci].astype(BF16)
            kk = _dot_nt(ksb, ksb)
            qk = _dot_nt(qs[ci].astype(BF16), ksb)
            lmat.append(jnp.where(strict, kk * decay * beta_col[ci], 0.0))
            intra.append((qk * decay).astype(BF16))
        def side_by_side(bd):
            return functools.reduce(lambda a, b: a + b, [bd[h * C:(h + 1) * C] for h in range(GDN_HEADS)])

        def block_diag(sbs):
            return jnp.where(same_head, jnp.concatenate([sbs] * GDN_HEADS, axis=0), 0.0)

        pw_bd = [l.astype(BF16) for l in lmat]
        pw = [side_by_side(l) for l in lmat]
        n_sbs = [-p for p in pw]
        for _ in range(5):
            pw = [_dot(p.astype(BF16), b) for p, b in zip(pw, pw_bd)]
            pw_bd = [block_diag(p).astype(BF16) for p in pw]
            n_sbs = [n + p + _dot(n.astype(BF16), b) for n, p, b in zip(n_sbs, pw, pw_bd)]
        n_mat = [block_diag(n) for n in n_sbs]
        for ci in chunks:
            e_gc = jnp.exp(gc_col[ci])
            rhs = jnp.concatenate([vs[ci] * beta_col[ci], ks[ci] * (beta_col[ci] * e_gc)], axis=1)
            sol = rhs + _dot(n_mat[ci].astype(BF16), rhs.astype(BF16))
            gl_col = jnp.concatenate(
                [jnp.broadcast_to(gl_row[ci][:, gc_lane0 + h:gc_lane0 + h + 1], (C, 1))
                 for h in range(GDN_HEADS)], axis=0)
            u_ref[ci] = sol[:, :dh]
            w_ref[ci] = sol[:, dh:].astype(BF16)
            intra_ref[ci] = intra[ci]
            qd_ref[ci] = (qs[ci] * e_gc).astype(BF16)
            kd_ref[ci] = (ks[ci] * jnp.exp(gl_col - gc_col[ci])).astype(BF16)
            for h in range(GDN_HEADS):
                gl_ref[ci, h:h + 1, :] = jnp.broadcast_to(
                    jnp.exp(gl_row[ci][:, gc_lane0 + h:gc_lane0 + h + 1]), (1, LANES))

    def run(rd, wr):
        for ci in chunks:
            advance(ci, rd)
        prepare_all(wr)

    @pl.when(step % 2 == 0)
    def _():
        run(buf_sets[0], buf_sets[1])

    @pl.when(step % 2 == 1)
    def _():
        run(buf_sets[1], buf_sets[0])


def _gdn(qn, kn, vv, col3d, gcrow, tg):
    b, s, _ = qn.shape
    n = s // tg
    nck = tg // GDN_CHUNK
    rows = GDN_HEADS * GDN_CHUNK
    dh = GDN_HEAD_DIM
    blk_in = lambda bi, i: (bi, jnp.minimum(i, n - 1), 0)
    blk_out = lambda bi, i: (bi, jnp.maximum(i - 1, 0), 0)
    return pl.pallas_call(
        functools.partial(_gdn_body, tg=tg),
        grid=(b, n + 1),
        in_specs=[pl.BlockSpec((1, tg, GDN_WIDTH), blk_in)] * 3
        + [pl.BlockSpec((1, tg, LANES), blk_in),
           pl.BlockSpec((1, nck, 1, rows), lambda bi, i: (bi, jnp.minimum(i, n - 1), 0, 0))],
        out_specs=pl.BlockSpec((1, tg, GDN_WIDTH), blk_out),
        out_shape=jax.ShapeDtypeStruct((b, s, GDN_WIDTH), BF16),
        scratch_shapes=[pltpu.VMEM((GDN_HEADS, dh, dh), F32)] + 2 * [
            pltpu.VMEM((nck, rows, dh), F32), pltpu.VMEM((nck, rows, dh), BF16),
            pltpu.VMEM((nck, rows, rows), BF16), pltpu.VMEM((nck, rows, dh), BF16),
            pltpu.VMEM((nck, rows, dh), BF16), pltpu.VMEM((nck, 8, LANES), F32)],
        compiler_params=_cparams(("parallel", "arbitrary")),
        name="gdn",
    )(qn, kn, vv, col3d, gcrow)


MOE_SLOT_TILE = 512


def _rows_to_tiles(ref, v):
    m, width = v.shape
    n = width // LANES
    for s in range(n):
        ref[pl.ds(s, m, stride=n), :] = v[:, s * LANES:(s + 1) * LANES]


def _tiles_to_rows(ref, m):
    n = ref.shape[0] // m
    return jnp.concatenate([ref[pl.ds(s, m, stride=n), :] for s in range(n)], axis=1)


RT_E1, RT_E2, RT_RANK1, RT_RANK2, RT_G1, RT_G2 = range(6)


def _combine_body(yf_ref, og_ref, dz_ref, gf_ref, gd_ref, x_ref, wof_ref, wod_ref, wout_ref,
                  gon_ref, gffn_ref, wrh_ref, wrl_ref, br_ref, x1_ref, t_ref, rt_ref, cnt_ref, carry_ref,
                  *, tc):
    @pl.when(pl.program_id(0) == 0)
    def _():
        carry_ref[...] = jnp.zeros_like(carry_ref)

    on = []
    for h in range(GDN_HEADS):
        sl = slice(h * GDN_HEAD_DIM, (h + 1) * GDN_HEAD_DIM)
        dz = dz_ref[:, sl].astype(F32)
        on.append(_rms(og_ref[:, sl].astype(F32), gon_ref[...]) * (dz * _sigmoid(dz)))
    on = jnp.concatenate(on, axis=1).astype(BF16)
    y_fox = _dot(yf_ref[...], wof_ref[...])
    y_delta = _dot(on, wod_ref[...])
    merged = _sigmoid(gf_ref[...].astype(F32)) * y_fox + _sigmoid(gd_ref[...].astype(F32)) * y_delta
    x1 = x_ref[...] + _dot(merged.astype(BF16), wout_ref[...])
    x1_ref[...] = x1
    t32 = _rms(x1, gffn_ref[...])
    th = t32.astype(BF16)
    tl = (t32 - th.astype(F32)).astype(BF16)
    _rows_to_tiles(t_ref, t32)
    logits = _dot(th, wrh_ref[...]) + _dot(tl, wrh_ref[...]) + _dot(th, wrl_ref[...]) + br_ref[...]
    lane = lax.broadcasted_iota(jnp.int32, logits.shape, 1)
    gl = jnp.where(lane < N_GROUPS, logits, NEG)
    gmax = jnp.max(gl, axis=-1, keepdims=True)
    g_sel = jnp.min(jnp.where(gl == gmax, lane, LANES), axis=-1, keepdims=True)
    p_sel = 1.0 / jnp.sum(jnp.exp(gl - gmax), axis=-1, keepdims=True)
    lo = EXPERT_LANE0 + EXPERTS_PER_GROUP * g_sel
    in_grp = jnp.logical_and(lane >= lo, lane < lo + EXPERTS_PER_GROUP)
    el = jnp.where(in_grp, logits, NEG)
    emax = jnp.max(el, axis=-1, keepdims=True)
    ee = jnp.where(in_grp, jnp.exp(el - emax), 0.0)
    pe = ee / jnp.sum(ee, axis=-1, keepdims=True)
    pe = jnp.where(in_grp, pe, -1.0)
    p1 = jnp.max(pe, axis=-1, keepdims=True)
    i1 = jnp.min(jnp.where(pe == p1, lane, LANES), axis=-1, keepdims=True)
    pe2 = jnp.where(lane == i1, -1.0, pe)
    p2 = jnp.max(pe2, axis=-1, keepdims=True)
    i2 = jnp.min(jnp.where(pe2 == p2, lane, LANES), axis=-1, keepdims=True)
    den = p1 + p2
    hit1 = lane == i1
    hit2 = lane == i2
    assign = jnp.where(jnp.logical_or(hit1, hit2), 1.0, 0.0)
    r = lax.broadcasted_iota(jnp.int32, (tc, tc), 0)
    c = lax.broadcasted_iota(jnp.int32, (tc, tc), 1)
    before = jnp.where(r > c, 1.0, 0.0).astype(BF16)
    prefix = _dot(before, assign.astype(BF16)) + carry_ref[...]
    rank1 = jnp.sum(jnp.where(hit1, prefix, 0.0), axis=-1, keepdims=True)
    rank2 = jnp.sum(jnp.where(hit2, prefix, 0.0), axis=-1, keepdims=True)
    carry_ref[...] = prefix[tc - 1:tc, :] + assign[tc - 1:tc, :]
    cnt_ref[...] = carry_ref[...]
    cols = ((i1 - EXPERT_LANE0).astype(F32), (i2 - EXPERT_LANE0).astype(F32), rank1, rank2,
            p_sel * (p1 / den), p_sel * (p2 / den))
    rt = jnp.zeros(logits.shape, F32)
    for k, v in enumerate(cols):
        rt = jnp.where(lane == k, v, rt)
    rt_ref[...] = rt


def _combine(yf, og, dz, gf, gd, x2d, wof, wod, wout, g_on, g_ffn, wrh, wrl, br, tc):
    t, d = x2d.shape
    row = lambda i: (i, 0)
    const = lambda i: (0, 0)
    acts = (yf, og, dz, gf, gd, x2d)
    consts = (wof, wod, wout, g_on, g_ffn, wrh, wrl, br)
    return pl.pallas_call(
        functools.partial(_combine_body, tc=tc),
        grid=(t // tc,),
        in_specs=[pl.BlockSpec((tc, a.shape[1]), row) for a in acts]
        + [pl.BlockSpec(c.shape, const) for c in consts],
        out_specs=[pl.BlockSpec((tc, d), row), pl.BlockSpec((tc * (d // LANES), LANES), row),
                   pl.BlockSpec((tc, LANES), row), pl.BlockSpec((1, LANES), const)],
        out_shape=[jax.ShapeDtypeStruct((t, d), F32), jax.ShapeDtypeStruct((t * (d // LANES), LANES), F32),
                   jax.ShapeDtypeStruct((t, LANES), F32), jax.ShapeDtypeStruct((1, LANES), F32)],
        scratch_shapes=[pltpu.VMEM((1, LANES), F32)],
        compiler_params=_cparams(("arbitrary",)),
        name="combine",
    )(*acts, *consts)


def _slots_body(rt_ref, cnt_ref, o_ref, *, tmx):
    lane1 = lax.broadcasted_iota(jnp.int32, (1, LANES), 1)
    is_expert = jnp.logical_and(lane1 >= EXPERT_LANE0, lane1 < EXPERT_LANE0 + N_EXPERTS)
    padded = jnp.where(is_expert, jnp.floor((cnt_ref[...] + (tmx - 1)) / tmx) * tmx, 0.0)
    r = lax.broadcasted_iota(jnp.int32, (LANES, LANES), 0)
    c = lax.broadcasted_iota(jnp.int32, (LANES, LANES), 1)
    before = jnp.where(r < c, 1.0, 0.0).astype(BF16)
    hi, mid, lo = _split3(jnp.broadcast_to(padded, (8, LANES)))
    offs = (_dot(hi, before) + _dot(mid, before) + _dot(lo, before))[0:1, :]
    rt = rt_ref[...]
    lane = lax.broadcasted_iota(jnp.int32, rt.shape, 1)
    expert_of_lane = (lane - EXPERT_LANE0).astype(F32)
    out = jnp.zeros(rt.shape, F32)
    for k, (ce, cr) in enumerate(((RT_E1, RT_RANK1), (RT_E2, RT_RANK2))):
        start = jnp.sum(jnp.where(expert_of_lane == rt[:, ce:ce + 1], offs, 0.0), axis=-1, keepdims=True)
        out = jnp.where(lane == k, start + rt[:, cr:cr + 1], out)
    o_ref[...] = out.T[:8, :].astype(jnp.int32)


def _slots(rt, cnt, tmx, ts):
    t = rt.shape[0]
    return pl.pallas_call(
        functools.partial(_slots_body, tmx=tmx),
        grid=(t // ts,),
        in_specs=[pl.BlockSpec((ts, LANES), lambda i: (i, 0)), pl.BlockSpec((1, LANES), lambda i: (0, 0))],
        out_specs=pl.BlockSpec((8, ts), lambda i: (0, i)),
        out_shape=jax.ShapeDtypeStruct((8, t), jnp.int32),
        compiler_params=_cparams(("parallel",)),
        name="slots",
    )(rt, cnt)


ROW_GROUP = 8


def _issue_row_copies(pos_ref, n_rows, nsub, make_copy):
    def issue(g, carry):
        r0 = g * ROW_GROUP
        slots = [[pos_ref[0, 0, k * n_rows + r0 + u] for k in range(2)] for u in range(ROW_GROUP)]
        for u in range(ROW_GROUP):
            tok = pl.ds(pl.multiple_of((r0 + u) * nsub, nsub), nsub)
            for k in range(2):
                slot = pl.ds(pl.multiple_of(slots[u][k] * nsub, nsub), nsub)
                make_copy(k, tok, slot).start(priority=k)
        return carry

    lax.fori_loop(0, n_rows // ROW_GROUP, issue, 0)


def _dispatch_body(zflag_ref, pos_ref, t_ref, xs_ref, zbuf, sem, zsem, *, td, tmx, n_tiles, nsub):
    @pl.when(pl.program_id(0) == 0)
    def _():
        zbuf[...] = jnp.zeros_like(zbuf)
        rows = tmx * nsub

        def zero_tile(k, carry):
            @pl.when(zflag_ref[k] != 0)
            def _():
                cp = pltpu.make_async_copy(zbuf, xs_ref.at[pl.ds(pl.multiple_of(k * rows, rows), rows), :], zsem)
                cp.start()
                cp.wait()

            return carry

        lax.fori_loop(0, n_tiles, zero_tile, 0)

    _issue_row_copies(pos_ref, td, nsub,
                      lambda k, tok, slot: pltpu.make_async_copy(t_ref.at[tok, :], xs_ref.at[slot, :], sem))
    for _ in range(2):
        pltpu.make_async_copy(t_ref, xs_ref.at[pl.ds(0, td * nsub), :], sem).wait()


def _dispatch(zflag, t_tiles, pos12, n_slots, td, tmx, nsub):
    t = t_tiles.shape[0] // nsub
    n_tiles = n_slots // tmx
    grid_spec = pltpu.PrefetchScalarGridSpec(
        num_scalar_prefetch=1,
        grid=(t // td,),
        in_specs=[pl.BlockSpec((1, 1, 2 * td), lambda i, zf: (i, 0, 0), memory_space=pltpu.SMEM),
                  pl.BlockSpec((td * nsub, LANES), lambda i, zf: (i, 0))],
        out_specs=pl.BlockSpec(memory_space=pl.ANY),
        scratch_shapes=[pltpu.VMEM((tmx * nsub, LANES), F32), pltpu.SemaphoreType.DMA(()),
                        pltpu.SemaphoreType.DMA(())],
    )
    return pl.pallas_call(
        functools.partial(_dispatch_body, td=td, tmx=tmx, n_tiles=n_tiles, nsub=nsub),
        grid_spec=grid_spec,
        out_shape=jax.ShapeDtypeStruct((n_slots * nsub, LANES), F32),
        compiler_params=_cparams(("arbitrary",)),
        name="dispatch",
    )(zflag, pos12, t_tiles)


def _experts_body(te_ref, nused_ref, xs_ref, wgu_ref, wd_ref, ys_ref, *, tmx):
    @pl.when(pl.program_id(0) < nused_ref[0])
    def _():
        hgu = _dot(_tiles_to_rows(xs_ref, tmx).astype(BF16), wgu_ref[0])
        a = hgu[:, :EXPERT_FF]
        hid = a * _sigmoid(a) * hgu[:, EXPERT_FF:]
        _rows_to_tiles(ys_ref, _dot(hid.astype(BF16), wd_ref[0]))

    @pl.when(pl.program_id(0) >= nused_ref[0])
    def _():
        ys_ref[...] = jnp.zeros_like(ys_ref)


def _experts(tile_expert, n_used, xs, wgu, wd, tmx):
    d = wgu.shape[1]
    nsub = d // LANES
    n_slots = xs.shape[0] // nsub
    grid_spec = pltpu.PrefetchScalarGridSpec(
        num_scalar_prefetch=2,
        grid=(n_slots // tmx,),
        in_specs=[pl.BlockSpec((tmx * nsub, LANES), lambda i, te, nu: (i, 0)),
                  pl.BlockSpec((1, d, 2 * EXPERT_FF), lambda i, te, nu: (te[i], 0, 0)),
                  pl.BlockSpec((1, EXPERT_FF, d), lambda i, te, nu: (te[i], 0, 0))],
        out_specs=pl.BlockSpec((tmx * nsub, LANES), lambda i, te, nu: (i, 0)),
    )
    return pl.pallas_call(
        functools.partial(_experts_body, tmx=tmx),
        grid_spec=grid_spec,
        out_shape=jax.ShapeDtypeStruct(xs.shape, F32),
        compiler_params=_cparams(("arbitrary",)),
        name="experts",
    )(tile_expert, n_used, xs, wgu, wd)


def _final_body(pos_ref, posn_ref, x_ref, rt_ref, p_ref, gple_ref, wpg_ref, wpp_ref, gfin_ref, ys_ref,
                o_ref, gbuf, sems, *, tf, nsub):
    i = pl.program_id(0)
    n = pl.num_programs(0)
    slot = i % 2

    def gather(pref, dst_slot):
        _issue_row_copies(pref, tf, nsub, lambda k, tok, src: pltpu.make_async_copy(
            ys_ref.at[src, :], gbuf.at[dst_slot, k, tok, :], sems.at[dst_slot]))

    @pl.when(i == 0)
    def _():
        gather(pos_ref, 0)

    @pl.when(i + 1 < n)
    def _():
        gather(posn_ref, 1 - slot)

    for k in range(2):
        pltpu.make_async_copy(ys_ref.at[pl.ds(0, tf * nsub), :], gbuf.at[slot, k], sems.at[slot]).wait()

    rt = rt_ref[...]
    g1 = rt[:, RT_G1:RT_G1 + 1]
    g2 = rt[:, RT_G2:RT_G2 + 1]
    x = x_ref[...] + g1 * _tiles_to_rows(gbuf.at[slot, 0], tf) + g2 * _tiles_to_rows(gbuf.at[slot, 1], tf)
    r = _rms(x, gple_ref[...]).astype(BF16)
    ple_gate = _sigmoid(_dot(r, wpg_ref[...]))
    proj = _dot(p_ref[...].astype(BF16), wpp_ref[...])
    o_ref[...] = _rms(x + ple_gate * proj, gfin_ref[...])


def _final(pos12, x1, rt, p2d, g_ple, wpg, wpp, g_final, ys, tf):
    t, d = x1.shape
    nsub = d // LANES
    n = t // tf
    row = lambda i: (i, 0)
    const = lambda i: (0, 0)
    smem_blk = lambda imap: pl.BlockSpec((1, 1, 2 * tf), imap, memory_space=pltpu.SMEM)
    return pl.pallas_call(
        functools.partial(_final_body, tf=tf, nsub=nsub),
        grid=(n,),
        in_specs=[smem_blk(lambda i: (i, 0, 0)), smem_blk(lambda i: (jnp.minimum(i + 1, n - 1), 0, 0)),
                  pl.BlockSpec((tf, d), row), pl.BlockSpec((tf, LANES), row),
                  pl.BlockSpec((tf, p2d.shape[1]), row),
                  pl.BlockSpec((1, d), const), pl.BlockSpec(wpg.shape, const),
                  pl.BlockSpec(wpp.shape, const), pl.BlockSpec((1, d), const),
                  pl.BlockSpec(memory_space=pl.ANY)],
        out_specs=pl.BlockSpec((tf, d), row),
        out_shape=jax.ShapeDtypeStruct((t, d), F32),
        scratch_shapes=[pltpu.VMEM((2, 2, tf * nsub, LANES), F32), pltpu.SemaphoreType.DMA((2,))],
        compiler_params=_cparams(("arbitrary",)),
        name="final",
    )(pos12, pos12, x1, rt, p2d, g_ple, wpg, wpp, g_final, ys)


def _pad_lanes(v, lane0):
    return jnp.zeros((1, LANES), F32).at[0, lane0:lane0 + v.shape[0]].set(v.astype(F32))


def _tile(n, pref):
    return pref if n % pref == 0 else n


def _layer(x, p_l, g_mix, w_in, b_forget, conv_w, a_log, dt_bias, g_onorm, w_o_fox, w_o_delta, w_out,
           g_ffn, w_group, b_group, w_router, b_router, w_gate, w_up, w_down, g_ple, w_ple_gate, w_ple_proj,
           g_post, apply_post):
    b, s, d = x.shape
    t = b * s
    x2d = x.reshape(t, d)

    o_ff = 3 * FOX_WIDTH
    o_qkv = o_ff + FOX_HEADS
    o_da = o_qkv + 3 * GDN_WIDTH
    o_db = o_da + GDN_HEADS
    o_dz = o_db + GDN_HEADS
    o_gf = o_dz + GDN_WIDTH
    o_gd = o_gf + d
    w_small = jnp.concatenate(
        [w_in[:, o_ff:o_qkv], w_in[:, o_da:o_db], w_in[:, o_db:o_dz],
         jnp.zeros((d, LANES - FOX_HEADS - 2 * GDN_HEADS), w_in.dtype)], axis=1)
    wq = w_in[:, :o_ff].astype(BF16)
    wg = w_in[:, o_qkv:o_da].astype(BF16)
    wzs = jnp.concatenate([w_in[:, o_dz:o_gf], w_small], axis=1).astype(BF16)
    wgf = w_in[:, o_gf:o_gd].astype(BF16)
    wgd = w_in[:, o_gd:].astype(BF16)

    zq, zg, dz, small, gf, gd = _inproj(x2d, g_mix.reshape(1, d), wq, wg, wzs, wgf, wgd, _tile(t, 512))

    par = jnp.concatenate(
        [_pad_lanes(b_forget, 0) + _pad_lanes(dt_bias, FOX_HEADS), _pad_lanes(a_log, FOX_HEADS),
         jnp.zeros((6, LANES), F32)], axis=0)
    qn, kn, vv, col = _gdnprep(zg.reshape(b, s, 3 * GDN_WIDTH), conv_w.astype(F32),
                               small.reshape(b, s, LANES), par, _tile(s, 512))

    nc = s // GDN_CHUNK
    gc = col[:, :, FOX_HEADS:FOX_HEADS + GDN_HEADS].reshape(b, nc, GDN_CHUNK, GDN_HEADS)
    gcrow = jnp.transpose(gc, (0, 1, 3, 2)).reshape(b, nc, 1, GDN_HEADS * GDN_CHUNK)

    y_fox = _fox(zq.reshape(b, s, 3 * FOX_WIDTH), col, _tile(s, 1024))
    o_gdn = _gdn(qn, kn, vv, col, gcrow, _tile(s, 512))

    w_r = jnp.concatenate([w_group, w_router,
                           jnp.zeros((d, LANES - N_GROUPS - N_EXPERTS), w_group.dtype)], axis=1).astype(F32)
    wrh = w_r.astype(BF16)
    wrl = (w_r - wrh.astype(F32)).astype(BF16)
    br = _pad_lanes(b_group, 0) + _pad_lanes(b_router, EXPERT_LANE0)
    x1, tpk, rt, cnt = _combine(
        y_fox.reshape(t, FOX_WIDTH), o_gdn.reshape(t, GDN_WIDTH), dz, gf, gd, x2d,
        w_o_fox.astype(BF16), w_o_delta.astype(BF16), w_out.astype(BF16),
        g_onorm.reshape(1, GDN_HEAD_DIM).astype(F32), g_ffn.reshape(1, d).astype(F32), wrh, wrl, br,
        _tile(t, 512))

    tmx = MOE_SLOT_TILE
    n_slots = 2 * t + N_EXPERTS * tmx
    counts = cnt[0, EXPERT_LANE0:EXPERT_LANE0 + N_EXPERTS].astype(jnp.int32)
    padded = (counts + tmx - 1) // tmx * tmx
    ends = jnp.cumsum(padded)
    pos = _slots(rt, cnt, tmx, _tile(t, 2048))
    pos1, pos2 = pos[0], pos[1]
    tile_start = jnp.arange(n_slots // tmx, dtype=jnp.int32) * tmx
    tile_expert = jnp.minimum(jnp.sum(tile_start[:, None] >= ends[None, :], axis=1), N_EXPERTS - 1).astype(jnp.int32)
    n_used = (ends[-1:] // tmx).astype(jnp.int32)
    last_of_expert = jnp.any(jnp.logical_and(tile_start[:, None] + tmx == ends[None, :], padded[None, :] > 0), axis=1)
    zflag = jnp.logical_or(last_of_expert, tile_start >= ends[-1]).astype(jnp.int32)

    def tiled_pos(tile):
        n = t // tile
        return jnp.concatenate([pos1.reshape(n, 1, tile), pos2.reshape(n, 1, tile)], axis=2)

    td = _tile(t, 1024)
    nsub = d // LANES
    xs = _dispatch(zflag, tpk, tiled_pos(td), n_slots, td, tmx, nsub)
    wgu = jnp.concatenate([w_gate, w_up], axis=2).astype(BF16)
    ys = _experts(tile_expert, n_used, xs, wgu, w_down.astype(BF16), tmx)

    tf = _tile(t, 256)
    out = _final(tiled_pos(tf), x1, rt, p_l.reshape(t, -1), g_ple.reshape(1, d).astype(F32),
                 w_ple_gate.astype(BF16), w_ple_proj.astype(BF16), g_post.reshape(1, d).astype(F32), ys, tf)
    return out.reshape(b, s, d)


def kernel(x, p, g_mix, w_in, b_forget, conv_w, a_log, dt_bias, g_onorm, w_o_fox, w_o_delta, w_out,
           g_ffn, w_group, b_group, w_router, b_router, w_gate, w_up, w_down, g_ple, w_ple_gate, w_ple_proj,
           g_final):
    depth = p.shape[0]
    assert depth == 1, "the final rmsnorm is fused into the last layer's epilogue; depth 1 only"
    i = 0
    return _layer(x, p[i], g_mix[i], w_in[i], b_forget[i], conv_w[i], a_log[i], dt_bias[i], g_onorm[i],
                  w_o_fox[i], w_o_delta[i], w_out[i], g_ffn[i], w_group[i], b_group[i], w_router[i],
                  b_router[i], w_gate[i], w_up[i], w_down[i], g_ple[i], w_ple_gate[i], w_ple_proj[i],
                  g_final, True)
```

```python
import functools

import jax
import jax.numpy as jnp
from jax import lax
from jax.experimental import pallas as pl
from jax.experimental.pallas import tpu as pltpu

F32 = jnp.float32
BF16 = jnp.bfloat16
EPS = 1e-6
NEG = -1e30

FOX_HEADS = 8
FOX_HEAD_DIM = 64
FOX_WIDTH = FOX_HEADS * FOX_HEAD_DIM
GDN_HEADS = 4
GDN_HEAD_DIM = 128
GDN_CONV = 4
GDN_CHUNK = 64
GDN_WIDTH = GDN_HEADS * GDN_HEAD_DIM
N_GROUPS = 4
EXPERTS_PER_GROUP = 8
N_EXPERTS = N_GROUPS * EXPERTS_PER_GROUP
EXPERT_FF = 256
LANES = 128
EXPERT_LANE0 = N_GROUPS

VMEM_LIMIT_BYTES = 56 * 1024 * 1024


def _cparams(sem):
    return pltpu.CompilerParams(dimension_semantics=sem, vmem_limit_bytes=VMEM_LIMIT_BYTES)


def _rms(x, g):
    return x * lax.rsqrt(jnp.mean(x * x, axis=-1, keepdims=True) + EPS) * g


def _sigmoid(x):
    return 1.0 / (1.0 + jnp.exp(-x))


def _dot(a, b):
    return jnp.dot(a, b, preferred_element_type=F32)


def _dot_nt(a, b):
    return lax.dot_general(a, b, (((1,), (1,)), ((), ())), preferred_element_type=F32)


def _dot_tn(a, b):
    return lax.dot_general(a, b, (((0,), (0,)), ((), ())), preferred_element_type=F32)


def _split3(v):
    hi = v.astype(BF16)
    r = v - hi.astype(F32)
    mid = r.astype(BF16)
    lo = (r - mid.astype(F32)).astype(BF16)
    return hi, mid, lo


def _dot_exact_lhs01(mat01, v):
    hi, mid, lo = _split3(v)
    return _dot(mat01, hi) + _dot(mat01, mid) + _dot(mat01, lo)


def _inproj_body(x_ref, g_ref, wq_ref, wg_ref, wzs_ref, wgf_ref, wgd_ref,
                 oq_ref, og_ref, oz_ref, osm_ref, ogf_ref, ogd_ref):
    hb = _rms(x_ref[...], g_ref[...]).astype(BF16)
    oq_ref[...] = _dot(hb, wq_ref[...]).astype(BF16)
    og_ref[...] = _dot(hb, wg_ref[...]).astype(BF16)
    zs = _dot(hb, wzs_ref[...])
    oz_ref[...] = zs[:, :GDN_WIDTH].astype(BF16)
    osm_ref[...] = zs[:, GDN_WIDTH:]
    ogf_ref[...] = _dot(hb, wgf_ref[...]).astype(BF16)
    ogd_ref[...] = _dot(hb, wgd_ref[...]).astype(BF16)


def _inproj(x2d, g_mix, wq, wg, wzs, wgf, wgd, tm):
    t, d = x2d.shape
    row = lambda i: (i, 0)
    const = lambda i: (0, 0)
    widths = (3 * FOX_WIDTH, 3 * GDN_WIDTH, GDN_WIDTH, LANES, d, d)
    dtypes = (BF16, BF16, BF16, F32, BF16, BF16)
    return pl.pallas_call(
        _inproj_body,
        grid=(t // tm,),
        in_specs=[pl.BlockSpec((tm, d), row), pl.BlockSpec((1, d), const)]
        + [pl.BlockSpec(w.shape, const) for w in (wq, wg, wzs, wgf, wgd)],
        out_specs=[pl.BlockSpec((tm, n), row) for n in widths],
        out_shape=[jax.ShapeDtypeStruct((t, n), dt) for n, dt in zip(widths, dtypes)],
        compiler_params=_cparams(("parallel",)),
        name="inproj",
    )(x2d, g_mix, wq, wg, wzs, wgf, wgd)


def _gates_tile(sm, par_ref, carry_ref):
    tp = sm.shape[0]
    lane = lax.broadcasted_iota(jnp.int32, sm.shape, 1)
    z = sm + par_ref[0:1, :]
    soft = jnp.log1p(jnp.exp(-jnp.abs(z)))
    softplus = jnp.maximum(z, 0.0) + soft
    logf = jnp.minimum(z, 0.0) - soft
    g = -jnp.exp(par_ref[1:2, :]) * softplus
    beta = _sigmoid(sm)
    is_f = lane < FOX_HEADS
    is_g = jnp.logical_and(lane >= FOX_HEADS, lane < FOX_HEADS + GDN_HEADS)
    is_b = jnp.logical_and(lane >= FOX_HEADS + GDN_HEADS, lane < FOX_HEADS + 2 * GDN_HEADS)
    val = jnp.where(is_f, logf, jnp.where(is_g, g, 0.0))
    r = lax.broadcasted_iota(jnp.int32, (tp, tp), 0)
    c = lax.broadcasted_iota(jnp.int32, (tp, tp), 1)
    lower = r >= c
    tri = jnp.where(lower, 1.0, 0.0).astype(BF16)
    same_chunk = (r // GDN_CHUNK) == (c // GDN_CHUNK)
    tri_chunk = jnp.where(jnp.logical_and(lower, same_chunk), 1.0, 0.0).astype(BF16)
    cum = _dot_exact_lhs01(tri, val) + carry_ref[...]
    gcs = _dot_exact_lhs01(tri_chunk, val)
    carry_ref[...] = cum[tp - 1:tp, :]
    return jnp.where(is_f, cum, jnp.where(is_g, gcs, jnp.where(is_b, beta, 0.0)))


LOG2E = 1.4426950408889634


FOX_VROWS = FOX_HEAD_DIM + 16


def _split3_f32(v):
    hi = v.astype(BF16).astype(F32)
    r = v - hi
    mid = r.astype(BF16).astype(F32)
    lo = (r - mid).astype(BF16).astype(F32)
    return hi, mid, lo


def _fox_body(q_ref, k_ref, v_ref, col_ref, o_ref, kx_ref, vt_ref, sa_ref, sb_ref, acc_ref, m_ref,
              *, tq, s_len):
    hp = pl.program_id(1)
    qi = pl.program_id(2)
    hd = FOX_HEAD_DIM

    @pl.when(qi == 0)
    def _():
        def prep(ci, carry):
            r0 = pl.multiple_of(ci * tq, tq)
            kx_ref[pl.ds(r0, tq), :LANES] = k_ref[0, pl.ds(r0, tq), :]
            col = col_ref[0, pl.ds(r0, tq), :]
            lane = lax.broadcasted_iota(jnp.int32, col.shape, 1)
            bias = jnp.zeros(col.shape, F32)
            for hh in range(2):
                ck = jnp.sum(jnp.where(lane == 2 * hp + hh, col, 0.0), axis=-1, keepdims=True) * LOG2E
                for k, piece in enumerate(_split3_f32(ck)):
                    bias = jnp.where(lane == 3 * hh + k, piece, bias)
            kx_ref[pl.ds(r0, tq), LANES:] = bias.astype(BF16)
            vt = v_ref[0, pl.ds(r0, tq), :].astype(F32).T
            ones_row = jnp.where(lax.broadcasted_iota(jnp.int32, (FOX_VROWS - hd, tq), 0) == 0, 1.0, 0.0)
            for hh in range(2):
                vt_ref[hh, :hd, pl.ds(r0, tq)] = vt[hh * hd:(hh + 1) * hd].astype(BF16)
                vt_ref[hh, hd:, pl.ds(r0, tq)] = ones_row.astype(BF16)
            return carry

        lax.fori_loop(0, s_len // tq, prep, 0)

    lane = lax.broadcasted_iota(jnp.int32, (tq, LANES), 1)
    q = (q_ref[0].astype(F32) * (hd ** -0.5 * LOG2E)).astype(BF16)
    zero = jnp.zeros_like(q)
    qx = []
    for hh in range(2):
        qh = jnp.where(jnp.logical_and(lane >= hh * hd, lane < (hh + 1) * hd), q, zero)
        sel = jnp.where(jnp.logical_and(lane >= 3 * hh, lane < 3 * hh + 3), -1.0, 0.0).astype(BF16)
        qx.append(jnp.concatenate([qh, sel], axis=1))
    m_ref[...] = jnp.full(m_ref.shape, NEG, F32)
    acc_ref[...] = jnp.zeros(acc_ref.shape, F32)

    def scores(j, dst_ref):
        start = pl.multiple_of(j * tq, tq)
        kx = kx_ref[pl.ds(start, tq), :]
        for hh in range(2):
            dst_ref[hh] = _dot_nt(kx, qx[hh])

    def consume(j, src_ref, masked):
        start = pl.multiple_of(j * tq, tq)
        for hh in range(2):
            s = src_ref[hh]
            if masked:
                key = lax.broadcasted_iota(jnp.int32, (tq, tq), 0)
                qry = lax.broadcasted_iota(jnp.int32, (tq, tq), 1)
                s = jnp.where(key <= qry, s, NEG)
            m_old = m_ref[hh:hh + 1, :]
            m_new = jnp.maximum(m_old, jnp.max(s, axis=0, keepdims=True))
            alpha = jnp.exp2(m_old - m_new)
            p = jnp.exp2(s - m_new).astype(BF16)
            acc_ref[hh] = alpha * acc_ref[hh] + _dot(vt_ref[hh, :, pl.ds(start, tq)], p)
            m_ref[hh:hh + 1, :] = m_new

    scores(0, sa_ref)

    def loop_body(j, carry):
        @pl.when(j % 2 == 0)
        def _():
            scores(j + 1, sb_ref)
            consume(j, sa_ref, False)

        @pl.when(j % 2 == 1)
        def _():
            scores(j + 1, sa_ref)
            consume(j, sb_ref, False)

        return carry

    lax.fori_loop(0, qi, loop_body, 0)

    @pl.when(qi % 2 == 0)
    def _():
        consume(qi, sa_ref, True)

    @pl.when(qi % 2 == 1)
    def _():
        consume(qi, sb_ref, True)

    out_t = jnp.concatenate([acc_ref[hh, :hd] / acc_ref[hh, hd:hd + 1] for hh in range(2)], axis=0)
    o_ref[0] = out_t.T.astype(BF16)


def _fox(zq3d, col3d, tq):
    b, s, _ = zq3d.shape
    npair = FOX_HEADS // 2
    kblk = FOX_WIDTH // LANES
    return pl.pallas_call(
        functools.partial(_fox_body, tq=tq, s_len=s),
        grid=(b, npair, s // tq),
        in_specs=[pl.BlockSpec((1, tq, LANES), lambda bi, hp, qi: (bi, qi, hp)),
                  pl.BlockSpec((1, s, LANES), lambda bi, hp, qi: (bi, 0, kblk + hp)),
                  pl.BlockSpec((1, s, LANES), lambda bi, hp, qi: (bi, 0, 2 * kblk + hp)),
                  pl.BlockSpec((1, s, LANES), lambda bi, hp, qi: (bi, 0, 0))],
        out_specs=pl.BlockSpec((1, tq, LANES), lambda bi, hp, qi: (bi, qi, hp)),
        out_shape=jax.ShapeDtypeStruct((b, s, FOX_WIDTH), BF16),
        scratch_shapes=[pltpu.VMEM((s, 2 * LANES), BF16), pltpu.VMEM((2, FOX_VROWS, s), BF16),
                        pltpu.VMEM((2, tq, tq), F32), pltpu.VMEM((2, tq, tq), F32),
                        pltpu.VMEM((2, FOX_VROWS, tq), F32), pltpu.VMEM((2, tq), F32)],
        compiler_params=_cparams(("parallel", "parallel", "arbitrary")),
        name="fox",
    )(zq3d, zq3d, zq3d, col3d)


HALO = 16


def _gdnprep_body(x_ref, halo_ref, cw_ref, sm_ref, par_ref, oq_ref, ok_ref, ov_ref, ocol_ref, ext_ref, carry_ref,
                  *, tp):
    @pl.when(pl.program_id(1) == 0)
    def _():
        carry_ref[...] = jnp.zeros_like(carry_ref)

    ocol_ref[0] = _gates_tile(sm_ref[0], par_ref, carry_ref)
    prev = halo_ref[0].astype(F32)
    ext_ref[0:HALO, :] = jnp.where(pl.program_id(1) > 0, prev, 0.0)
    ext_ref[HALO:, :] = x_ref[0].astype(F32)
    acc = cw_ref[GDN_CONV - 1:GDN_CONV, :] * ext_ref[HALO:HALO + tp, :]
    for j in range(GDN_CONV - 1):
        off = HALO - (GDN_CONV - 1) + j
        acc = acc + cw_ref[j:j + 1, :] * ext_ref[off:off + tp, :]
    y = acc * _sigmoid(acc)

    def l2(v):
        return v * lax.rsqrt(jnp.sum(v * v, axis=-1, keepdims=True) + EPS)

    for h in range(GDN_HEADS):
        lo, hi = h * GDN_HEAD_DIM, (h + 1) * GDN_HEAD_DIM
        oq_ref[0, :, lo:hi] = (l2(y[:, lo:hi]) * GDN_HEAD_DIM ** -0.5).astype(BF16)
        ok_ref[0, :, lo:hi] = l2(y[:, GDN_WIDTH + lo:GDN_WIDTH + hi]).astype(BF16)
    ov_ref[0] = y[:, 2 * GDN_WIDTH:].astype(BF16)


def _gdnprep(zg3d, conv_w, small3d, par, tp):
    b, s, c = zg3d.shape
    blk = lambda bi, i: (bi, i, 0)
    return pl.pallas_call(
        functools.partial(_gdnprep_body, tp=tp),
        grid=(b, s // tp),
        in_specs=[pl.BlockSpec((1, tp, c), blk),
                  pl.BlockSpec((1, HALO, c), lambda bi, i: (bi, jnp.maximum(i * (tp // HALO) - 1, 0), 0)),
                  pl.BlockSpec(conv_w.shape, lambda bi, i: (0, 0)),
                  pl.BlockSpec((1, tp, LANES), blk), pl.BlockSpec(par.shape, lambda bi, i: (0, 0))],
        out_specs=[pl.BlockSpec((1, tp, GDN_WIDTH), blk)] * 3 + [pl.BlockSpec((1, tp, LANES), blk)],
        out_shape=[jax.ShapeDtypeStruct((b, s, GDN_WIDTH), BF16)] * 3 + [jax.ShapeDtypeStruct((b, s, LANES), F32)],
        scratch_shapes=[pltpu.VMEM((tp + HALO, c), F32), pltpu.VMEM((1, LANES), F32)],
        compiler_params=_cparams(("parallel", "arbitrary")),
        name="gdnprep",
    )(zg3d, zg3d, conv_w, small3d, par)


def _stack_heads(x):
    return jnp.concatenate([x[:, h * GDN_HEAD_DIM:(h + 1) * GDN_HEAD_DIM] for h in range(GDN_HEADS)], axis=0)


def _gdn_body(q_ref, k_ref, v_ref, col_ref, grow_ref, o_ref, state_ref, *bufs, tg):
    C = GDN_CHUNK
    R = GDN_HEADS * C
    dh = GDN_HEAD_DIM
    n_chunks = tg // C
    step = pl.program_id(1)
    buf_sets = (bufs[:len(bufs) // 2], bufs[len(bufs) // 2:])

    @pl.when(step == 0)
    def _():
        for ref in (state_ref,) + tuple(bufs):
            ref[...] = jnp.zeros_like(ref)

    r = lax.broadcasted_iota(jnp.int32, (R, R), 0)
    c = lax.broadcasted_iota(jnp.int32, (R, R), 1)
    same_head = (r // C) == (c // C)
    lower = jnp.logical_and(same_head, r >= c)
    strict = jnp.logical_and(same_head, r > c)
    gc_lane0 = FOX_HEADS
    beta_lane0 = FOX_HEADS + GDN_HEADS

    def advance(ci, rd):
        u_ref, w_ref, intra_ref, qd_ref, kd_ref, gl_ref = rd
        r0 = ci * C
        u, w, intra = u_ref[ci], w_ref[ci], intra_ref[ci]
        q_dec, k_dec = qd_ref[ci], kd_ref[ci]
        v_new = []
        o_state = []
        for h in range(GDN_HEADS):
            sl = slice(h * C, (h + 1) * C)
            st = state_ref[h].astype(BF16)
            v_new.append(u[sl] - _dot(w[sl], st))
            o_state.append(_dot(q_dec[sl], st))
        v_new = jnp.concatenate(v_new, axis=0)
        v_new_b = v_new.astype(BF16)
        o_all = jnp.concatenate(o_state, axis=0) + _dot(intra, v_new_b)
        for h in range(GDN_HEADS):
            sl = slice(h * C, (h + 1) * C)
            state_ref[h] = state_ref[h] * gl_ref[ci, h:h + 1, :] + _dot_tn(k_dec[sl], v_new_b[sl])
        o_ref[0, r0:r0 + C, :] = jnp.concatenate(
            [o_all[h * C:(h + 1) * C] for h in range(GDN_HEADS)], axis=1).astype(BF16)

    chunks = range(n_chunks)

    def prepare_all(wr):
        u_ref, w_ref, intra_ref, qd_ref, kd_ref, gl_ref = wr
        qs, ks, vs, gc_col, beta_col, gl_row, lmat, intra = [], [], [], [], [], [], [], []
        for ci in chunks:
            r0 = ci * C
            qs.append(_stack_heads(q_ref[0, r0:r0 + C, :]).astype(F32))
            ks.append(_stack_heads(k_ref[0, r0:r0 + C, :]).astype(F32))
            vs.append(_stack_heads(v_ref[0, r0:r0 + C, :]).astype(F32))
            col = col_ref[0, r0:r0 + C, :]
            gc_col.append(
                jnp.concatenate([col[:, gc_lane0 + h:gc_lane0 + h + 1] for h in range(GDN_HEADS)], axis=0))
            beta_col.append(
                jnp.concatenate([col[:, beta_lane0 + h:beta_lane0 + h + 1] for h in range(GDN_HEADS)], axis=0))
            gl_row.append(col[C - 1:C, :])
            gc_row = grow_ref[0, ci, :, :]
            decay = jnp.exp(jnp.where(lower, gc_col[ci] - gc_row, NEG))
            ksb = ks[ci].astype(BF16)
            kk = _dot_nt(ksb, ksb)
            qk = _dot_nt(qs[ci].astype(BF16), ksb)
            lmat.append(jnp.where(strict, kk * decay * beta_col[ci], 0.0))
            intra.append((qk * decay).astype(BF16))
        def side_by_side(bd):
            return functools.reduce(lambda a, b: a + b, [bd[h * C:(h + 1) * C] for h in range(GDN_HEADS)])

        def block_diag(sbs):
            return jnp.where(same_head, jnp.concatenate([sbs] * GDN_HEADS, axis=0), 0.0)

        pw_bd = [l.astype(BF16) for l in lmat]
        pw = [side_by_side(l) for l in lmat]
        n_sbs = [-p for p in pw]
        for _ in range(5):
            pw = [_dot(p.astype(BF16), b) for p, b in zip(pw, pw_bd)]
            pw_bd = [block_diag(p).astype(BF16) for p in pw]
            n_sbs = [n + p + _dot(n.astype(BF16), b) for n, p, b in zip(n_sbs, pw, pw_bd)]
        n_mat = [block_diag(n) for n in n_sbs]
        for ci in chunks:
            e_gc = jnp.exp(gc_col[ci])
            rhs = jnp.concatenate([vs[ci] * beta_col[ci], ks[ci] * (beta_col[ci] * e_gc)], axis=1)
            sol = rhs + _dot(n_mat[ci].astype(BF16), rhs.astype(BF16))
            gl_col = jnp.concatenate(
                [jnp.broadcast_to(gl_row[ci][:, gc_lane0 + h:gc_lane0 + h + 1], (C, 1))
                 for h in range(GDN_HEADS)], axis=0)
            u_ref[ci] = sol[:, :dh]
            w_ref[ci] = sol[:, dh:].astype(BF16)
            intra_ref[ci] = intra[ci]
            qd_ref[ci] = (qs[ci] * e_gc).astype(BF16)
            kd_ref[ci] = (ks[ci] * jnp.exp(gl_col - gc_col[ci])).astype(BF16)
            for h in range(GDN_HEADS):
                gl_ref[ci, h:h + 1, :] = jnp.broadcast_to(
                    jnp.exp(gl_row[ci][:, gc_lane0 + h:gc_lane0 + h + 1]), (1, LANES))

    def run(rd, wr):
        for ci in chunks:
            advance(ci, rd)
        prepare_all(wr)

    @pl.when(step % 2 == 0)
    def _():
        run(buf_sets[0], buf_sets[1])

    @pl.when(step % 2 == 1)
    def _():
        run(buf_sets[1], buf_sets[0])


def _gdn(qn, kn, vv, col3d, gcrow, tg):
    b, s, _ = qn.shape
    n = s // tg
    nck = tg // GDN_CHUNK
    rows = GDN_HEADS * GDN_CHUNK
    dh = GDN_HEAD_DIM
    blk_in = lambda bi, i: (bi, jnp.minimum(i, n - 1), 0)
    blk_out = lambda bi, i: (bi, jnp.maximum(i - 1, 0), 0)
    return pl.pallas_call(
        functools.partial(_gdn_body, tg=tg),
        grid=(b, n + 1),
        in_specs=[pl.BlockSpec((1, tg, GDN_WIDTH), blk_in)] * 3
        + [pl.BlockSpec((1, tg, LANES), blk_in),
           pl.BlockSpec((1, nck, 1, rows), lambda bi, i: (bi, jnp.minimum(i, n - 1), 0, 0))],
        out_specs=pl.BlockSpec((1, tg, GDN_WIDTH), blk_out),
        out_shape=jax.ShapeDtypeStruct((b, s, GDN_WIDTH), BF16),
        scratch_shapes=[pltpu.VMEM((GDN_HEADS, dh, dh), F32)] + 2 * [
            pltpu.VMEM((nck, rows, dh), F32), pltpu.VMEM((nck, rows, dh), BF16),
            pltpu.VMEM((nck, rows, rows), BF16), pltpu.VMEM((nck, rows, dh), BF16),
            pltpu.VMEM((nck, rows, dh), BF16), pltpu.VMEM((nck, 8, LANES), F32)],
        compiler_params=_cparams(("parallel", "arbitrary")),
        name="gdn",
    )(qn, kn, vv, col3d, gcrow)


MOE_SLOT_TILE = 512


def _rows_to_tiles(ref, v):
    m, width = v.shape
    n = width // LANES
    for s in range(n):
        ref[pl.ds(s, m, stride=n), :] = v[:, s * LANES:(s + 1) * LANES]


def _tiles_to_rows(ref, m):
    n = ref.shape[0] // m
    return jnp.concatenate([ref[pl.ds(s, m, stride=n), :] for s in range(n)], axis=1)


RT_E1, RT_E2, RT_RANK1, RT_RANK2, RT_G1, RT_G2 = range(6)


def _combine_body(yf_ref, og_ref, dz_ref, gf_ref, gd_ref, x_ref, wof_ref, wod_ref, wout_ref,
                  gon_ref, gffn_ref, wrh_ref, wrl_ref, br_ref, x1_ref, t_ref, rt_ref, cnt_ref, carry_ref,
                  *, tc):
    @pl.when(pl.program_id(0) == 0)
    def _():
        carry_ref[...] = jnp.zeros_like(carry_ref)

    on = []
    for h in range(GDN_HEADS):
        sl = slice(h * GDN_HEAD_DIM, (h + 1) * GDN_HEAD_DIM)
        dz = dz_ref[:, sl].astype(F32)
        on.append(_rms(og_ref[:, sl].astype(F32), gon_ref[...]) * (dz * _sigmoid(dz)))
    on = jnp.concatenate(on, axis=1).astype(BF16)
    y_fox = _dot(yf_ref[...], wof_ref[...])
    y_delta = _dot(on, wod_ref[...])
    merged = _sigmoid(gf_ref[...].astype(F32)) * y_fox + _sigmoid(gd_ref[...].astype(F32)) * y_delta
    x1 = x_ref[...] + _dot(merged.astype(BF16), wout_ref[...])
    x1_ref[...] = x1
    t32 = _rms(x1, gffn_ref[...])
    th = t32.astype(BF16)
    tl = (t32 - th.astype(F32)).astype(BF16)
    _rows_to_tiles(t_ref, t32)
    logits = _dot(th, wrh_ref[...]) + _dot(tl, wrh_ref[...]) + _dot(th, wrl_ref[...]) + br_ref[...]
    lane = lax.broadcasted_iota(jnp.int32, logits.shape, 1)
    gl = jnp.where(lane < N_GROUPS, logits, NEG)
    gmax = jnp.max(gl, axis=-1, keepdims=True)
    g_sel = jnp.min(jnp.where(gl == gmax, lane, LANES), axis=-1, keepdims=True)
    p_sel = 1.0 / jnp.sum(jnp.exp(gl - gmax), axis=-1, keepdims=True)
    lo = EXPERT_LANE0 + EXPERTS_PER_GROUP * g_sel
    in_grp = jnp.logical_and(lane >= lo, lane < lo + EXPERTS_PER_GROUP)
    el = jnp.where(in_grp, logits, NEG)
    emax = jnp.max(el, axis=-1, keepdims=True)
    ee = jnp.where(in_grp, jnp.exp(el - emax), 0.0)
    pe = ee / jnp.sum(ee, axis=-1, keepdims=True)
    pe = jnp.where(in_grp, pe, -1.0)
    p1 = jnp.max(pe, axis=-1, keepdims=True)
    i1 = jnp.min(jnp.where(pe == p1, lane, LANES), axis=-1, keepdims=True)
    pe2 = jnp.where(lane == i1, -1.0, pe)
    p2 = jnp.max(pe2, axis=-1, keepdims=True)
    i2 = jnp.min(jnp.where(pe2 == p2, lane, LANES), axis=-1, keepdims=True)
    den = p1 + p2
    hit1 = lane == i1
    hit2 = lane == i2
    assign = jnp.where(jnp.logical_or(hit1, hit2), 1.0, 0.0)
    r = lax.broadcasted_iota(jnp.int32, (tc, tc), 0)
    c = lax.broadcasted_iota(jnp.int32, (tc, tc), 1)
    before = jnp.where(r > c, 1.0, 0.0).astype(BF16)
    prefix = _dot(before, assign.astype(BF16)) + carry_ref[...]
    rank1 = jnp.sum(jnp.where(hit1, prefix, 0.0), axis=-1, keepdims=True)
    rank2 = jnp.sum(jnp.where(hit2, prefix, 0.0), axis=-1, keepdims=True)
    carry_ref[...] = prefix[tc - 1:tc, :] + assign[tc - 1:tc, :]
    cnt_ref[...] = carry_ref[...]
    cols = ((i1 - EXPERT_LANE0).astype(F32), (i2 - EXPERT_LANE0).astype(F32), rank1, rank2,
            p_sel * (p1 / den), p_sel * (p2 / den))
    rt = jnp.zeros(logits.shape, F32)
    for k, v in enumerate(cols):
        rt = jnp.where(lane == k, v, rt)
    rt_ref[...] = rt


def _combine(yf, og, dz, gf, gd, x2d, wof, wod, wout, g_on, g_ffn, wrh, wrl, br, tc):
    t, d = x2d.shape
    row = lambda i: (i, 0)
    const = lambda i: (0, 0)
    acts = (yf, og, dz, gf, gd, x2d)
    consts = (wof, wod, wout, g_on, g_ffn, wrh, wrl, br)
    return pl.pallas_call(
        functools.partial(_combine_body, tc=tc),
        grid=(t // tc,),
        in_specs=[pl.BlockSpec((tc, a.shape[1]), row) for a in acts]
        + [pl.BlockSpec(c.shape, const) for c in consts],
        out_specs=[pl.BlockSpec((tc, d), row), pl.BlockSpec((tc * (d // LANES), LANES), row),
                   pl.BlockSpec((tc, LANES), row), pl.BlockSpec((1, LANES), const)],
        out_shape=[jax.ShapeDtypeStruct((t, d), F32), jax.ShapeDtypeStruct((t * (d // LANES), LANES), F32),
                   jax.ShapeDtypeStruct((t, LANES), F32), jax.ShapeDtypeStruct((1, LANES), F32)],
        scratch_shapes=[pltpu.VMEM((1, LANES), F32)],
        compiler_params=_cparams(("arbitrary",)),
        name="combine",
    )(*acts, *consts)


def _slots_body(rt_ref, cnt_ref, o_ref, *, tmx):
    lane1 = lax.broadcasted_iota(jnp.int32, (1, LANES), 1)
    is_expert = jnp.logical_and(lane1 >= EXPERT_LANE0, lane1 < EXPERT_LANE0 + N_EXPERTS)
    padded = jnp.where(is_expert, jnp.floor((cnt_ref[...] + (tmx - 1)) / tmx) * tmx, 0.0)
    r = lax.broadcasted_iota(jnp.int32, (LANES, LANES), 0)
    c = lax.broadcasted_iota(jnp.int32, (LANES, LANES), 1)
    before = jnp.where(r < c, 1.0, 0.0).astype(BF16)
    hi, mid, lo = _split3(jnp.broadcast_to(padded, (8, LANES)))
    offs = (_dot(hi, before) + _dot(mid, before) + _dot(lo, before))[0:1, :]
    rt = rt_ref[...]
    lane = lax.broadcasted_iota(jnp.int32, rt.shape, 1)
    expert_of_lane = (lane - EXPERT_LANE0).astype(F32)
    out = jnp.zeros(rt.shape, F32)
    for k, (ce, cr) in enumerate(((RT_E1, RT_RANK1), (RT_E2, RT_RANK2))):
        start = jnp.sum(jnp.where(expert_of_lane == rt[:, ce:ce + 1], offs, 0.0), axis=-1, keepdims=True)
        out = jnp.where(lane == k, start + rt[:, cr:cr + 1], out)
    o_ref[...] = out.T[:8, :].astype(jnp.int32)


def _slots(rt, cnt, tmx, ts):
    t = rt.shape[0]
    return pl.pallas_call(
        functools.partial(_slots_body, tmx=tmx),
        grid=(t // ts,),
        in_specs=[pl.BlockSpec((ts, LANES), lambda i: (i, 0)), pl.BlockSpec((1, LANES), lambda i: (0, 0))],
        out_specs=pl.BlockSpec((8, ts), lambda i: (0, i)),
        out_shape=jax.ShapeDtypeStruct((8, t), jnp.int32),
        compiler_params=_cparams(("parallel",)),
        name="slots",
    )(rt, cnt)


ROW_GROUP = 8


def _issue_row_copies(pos_ref, n_rows, nsub, make_copy):
    def issue(g, carry):
        r0 = g * ROW_GROUP
        slots = [[pos_ref[0, 0, k * n_rows + r0 + u] for k in range(2)] for u in range(ROW_GROUP)]
        for u in range(ROW_GROUP):
            tok = pl.ds(pl.multiple_of((r0 + u) * nsub, nsub), nsub)
            for k in range(2):
                slot = pl.ds(pl.multiple_of(slots[u][k] * nsub, nsub), nsub)
                make_copy(k, tok, slot).start(priority=k)
        return carry

    lax.fori_loop(0, n_rows // ROW_GROUP, issue, 0)


def _dispatch_body(zflag_ref, pos_ref, t_ref, xs_ref, zbuf, sem, zsem, *, td, tmx, n_tiles, nsub):
    @pl.when(pl.program_id(0) == 0)
    def _():
        zbuf[...] = jnp.zeros_like(zbuf)
        rows = tmx * nsub

        def zero_tile(k, carry):
            @pl.when(zflag_ref[k] != 0)
            def _():
                cp = pltpu.make_async_copy(zbuf, xs_ref.at[pl.ds(pl.multiple_of(k * rows, rows), rows), :], zsem)
                cp.start()
                cp.wait()

            return carry

        lax.fori_loop(0, n_tiles, zero_tile, 0)

    _issue_row_copies(pos_ref, td, nsub,
                      lambda k, tok, slot: pltpu.make_async_copy(t_ref.at[tok, :], xs_ref.at[slot, :], sem))
    for _ in range(2):
        pltpu.make_async_copy(t_ref, xs_ref.at[pl.ds(0, td * nsub), :], sem).wait()


def _dispatch(zflag, t_tiles, pos12, n_slots, td, tmx, nsub):
    t = t_tiles.shape[0] // nsub
    n_tiles = n_slots // tmx
    grid_spec = pltpu.PrefetchScalarGridSpec(
        num_scalar_prefetch=1,
        grid=(t // td,),
        in_specs=[pl.BlockSpec((1, 1, 2 * td), lambda i, zf: (i, 0, 0), memory_space=pltpu.SMEM),
                  pl.BlockSpec((td * nsub, LANES), lambda i, zf: (i, 0))],
        out_specs=pl.BlockSpec(memory_space=pl.ANY),
        scratch_shapes=[pltpu.VMEM((tmx * nsub, LANES), F32), pltpu.SemaphoreType.DMA(()),
                        pltpu.SemaphoreType.DMA(())],
    )
    return pl.pallas_call(
        functools.partial(_dispatch_body, td=td, tmx=tmx, n_tiles=n_tiles, nsub=nsub),
        grid_spec=grid_spec,
        out_shape=jax.ShapeDtypeStruct((n_slots * nsub, LANES), F32),
        compiler_params=_cparams(("arbitrary",)),
        name="dispatch",
    )(zflag, pos12, t_tiles)


def _experts_body(te_ref, nused_ref, xs_ref, wgu_ref, wd_ref, ys_ref, *, tmx):
    @pl.when(pl.program_id(0) < nused_ref[0])
    def _():
        hgu = _dot(_tiles_to_rows(xs_ref, tmx).astype(BF16), wgu_ref[0])
        a = hgu[:, :EXPERT_FF]
        hid = a * _sigmoid(a) * hgu[:, EXPERT_FF:]
        _rows_to_tiles(ys_ref, _dot(hid.astype(BF16), wd_ref[0]))

    @pl.when(pl.program_id(0) >= nused_ref[0])
    def _():
        ys_ref[...] = jnp.zeros_like(ys_ref)


def _experts(tile_expert, n_used, xs, wgu, wd, tmx):
    d = wgu.shape[1]
    nsub = d // LANES
    n_slots = xs.shape[0] // nsub
    grid_spec = pltpu.PrefetchScalarGridSpec(
        num_scalar_prefetch=2,
        grid=(n_slots // tmx,),
        in_specs=[pl.BlockSpec((tmx * nsub, LANES), lambda i, te, nu: (i, 0)),
                  pl.BlockSpec((1, d, 2 * EXPERT_FF), lambda i, te, nu: (te[i], 0, 0)),
                  pl.BlockSpec((1, EXPERT_FF, d), lambda i, te, nu: (te[i], 0, 0))],
        out_specs=pl.BlockSpec((tmx * nsub, LANES), lambda i, te, nu: (i, 0)),
    )
    return pl.pallas_call(
        functools.partial(_experts_body, tmx=tmx),
        grid_spec=grid_spec,
        out_shape=jax.ShapeDtypeStruct(xs.shape, F32),
        compiler_params=_cparams(("arbitrary",)),
        name="experts",
    )(tile_expert, n_used, xs, wgu, wd)


def _final_body(pos_ref, posn_ref, x_ref, rt_ref, p_ref, gple_ref, wpg_ref, wpp_ref, gfin_ref, ys_ref,
                o_ref, gbuf, sems, *, tf, nsub):
    i = pl.program_id(0)
    n = pl.num_programs(0)
    slot = i % 2

    other = 1 - slot

    def wait_tile(dst_slot):
        for k in range(2):
            pltpu.make_async_copy(ys_ref.at[pl.ds(0, tf * nsub), :], gbuf.at[dst_slot, k], sems.at[dst_slot]).wait()

    @pl.when(i == 0)
    def _():
        _issue_row_copies(pos_ref, tf, nsub, lambda k, tok, src: pltpu.make_async_copy(
            ys_ref.at[src, :], gbuf.at[0, k, tok, :], sems.at[0]))

    wait_tile(slot)
    rt = rt_ref[...]
    g1 = rt[:, RT_G1:RT_G1 + 1]
    g2 = rt[:, RT_G2:RT_G2 + 1]
    x = x_ref[...] + g1 * _tiles_to_rows(gbuf.at[slot, 0], tf) + g2 * _tiles_to_rows(gbuf.at[slot, 1], tf)
    for row in range(tf):
        tok = pl.ds(row * nsub, nsub)
        for k in range(2):
            src = pl.ds(pl.multiple_of(posn_ref[0, 0, k * tf + row] * nsub, nsub), nsub)
            pltpu.make_async_copy(ys_ref.at[src, :], gbuf.at[other, k, tok, :], sems.at[other]).start(priority=k)
    r = _rms(x, gple_ref[...]).astype(BF16)
    ple_gate = _sigmoid(_dot(r, wpg_ref[...]))
    proj = _dot(p_ref[...].astype(BF16), wpp_ref[...])
    o_ref[...] = _rms(x + ple_gate * proj, gfin_ref[...])

    @pl.when(i == n - 1)
    def _():
        wait_tile(other)


def _final(pos12, x1, rt, p2d, g_ple, wpg, wpp, g_final, ys, tf):
    t, d = x1.shape
    nsub = d // LANES
    n = t // tf
    row = lambda i: (i, 0)
    const = lambda i: (0, 0)
    smem_blk = lambda imap: pl.BlockSpec((1, 1, 2 * tf), imap, memory_space=pltpu.SMEM)
    return pl.pallas_call(
        functools.partial(_final_body, tf=tf, nsub=nsub),
        grid=(n,),
        in_specs=[smem_blk(lambda i: (i, 0, 0)), smem_blk(lambda i: (jnp.minimum(i + 1, n - 1), 0, 0)),
                  pl.BlockSpec((tf, d), row), pl.BlockSpec((tf, LANES), row),
                  pl.BlockSpec((tf, p2d.shape[1]), row),
                  pl.BlockSpec((1, d), const), pl.BlockSpec(wpg.shape, const),
                  pl.BlockSpec(wpp.shape, const), pl.BlockSpec((1, d), const),
                  pl.BlockSpec(memory_space=pl.ANY)],
        out_specs=pl.BlockSpec((tf, d), row),
        out_shape=jax.ShapeDtypeStruct((t, d), F32),
        scratch_shapes=[pltpu.VMEM((2, 2, tf * nsub, LANES), F32), pltpu.SemaphoreType.DMA((2,))],
        compiler_params=_cparams(("arbitrary",)),
        name="final",
    )(pos12, pos12, x1, rt, p2d, g_ple, wpg, wpp, g_final, ys)


def _pad_lanes(v, lane0):
    return jnp.zeros((1, LANES), F32).at[0, lane0:lane0 + v.shape[0]].set(v.astype(F32))


def _tile(n, pref):
    return pref if n % pref == 0 else n


def _layer(x, p_l, g_mix, w_in, b_forget, conv_w, a_log, dt_bias, g_onorm, w_o_fox, w_o_delta, w_out,
           g_ffn, w_group, b_group, w_router, b_router, w_gate, w_up, w_down, g_ple, w_ple_gate, w_ple_proj,
           g_post):
    b, s, d = x.shape
    t = b * s
    x2d = x.reshape(t, d)

    o_ff = 3 * FOX_WIDTH
    o_qkv = o_ff + FOX_HEADS
    o_da = o_qkv + 3 * GDN_WIDTH
    o_db = o_da + GDN_HEADS
    o_dz = o_db + GDN_HEADS
    o_gf = o_dz + GDN_WIDTH
    o_gd = o_gf + d
    w_small = jnp.concatenate(
        [w_in[:, o_ff:o_qkv], w_in[:, o_da:o_db], w_in[:, o_db:o_dz],
         jnp.zeros((d, LANES - FOX_HEADS - 2 * GDN_HEADS), w_in.dtype)], axis=1)
    wq = w_in[:, :o_ff].astype(BF16)
    wg = w_in[:, o_qkv:o_da].astype(BF16)
    wzs = jnp.concatenate([w_in[:, o_dz:o_gf], w_small], axis=1).astype(BF16)
    wgf = w_in[:, o_gf:o_gd].astype(BF16)
    wgd = w_in[:, o_gd:].astype(BF16)

    zq, zg, dz, small, gf, gd = _inproj(x2d, g_mix.reshape(1, d), wq, wg, wzs, wgf, wgd, _tile(t, 512))

    par = jnp.concatenate(
        [_pad_lanes(b_forget, 0) + _pad_lanes(dt_bias, FOX_HEADS), _pad_lanes(a_log, FOX_HEADS),
         jnp.zeros((6, LANES), F32)], axis=0)
    qn, kn, vv, col = _gdnprep(zg.reshape(b, s, 3 * GDN_WIDTH), conv_w.astype(F32),
                               small.reshape(b, s, LANES), par, _tile(s, 512))

    nc = s // GDN_CHUNK
    gc = col[:, :, FOX_HEADS:FOX_HEADS + GDN_HEADS].reshape(b, nc, GDN_CHUNK, GDN_HEADS)
    gcrow = jnp.transpose(gc, (0, 1, 3, 2)).reshape(b, nc, 1, GDN_HEADS * GDN_CHUNK)

    y_fox = _fox(zq.reshape(b, s, 3 * FOX_WIDTH), col, _tile(s, 1024))
    o_gdn = _gdn(qn, kn, vv, col, gcrow, _tile(s, 512))

    w_r = jnp.concatenate([w_group, w_router,
                           jnp.zeros((d, LANES - N_GROUPS - N_EXPERTS), w_group.dtype)], axis=1).astype(F32)
    wrh = w_r.astype(BF16)
    wrl = (w_r - wrh.astype(F32)).astype(BF16)
    br = _pad_lanes(b_group, 0) + _pad_lanes(b_router, EXPERT_LANE0)
    x1, tpk, rt, cnt = _combine(
        y_fox.reshape(t, FOX_WIDTH), o_gdn.reshape(t, GDN_WIDTH), dz, gf, gd, x2d,
        w_o_fox.astype(BF16), w_o_delta.astype(BF16), w_out.astype(BF16),
        g_onorm.reshape(1, GDN_HEAD_DIM).astype(F32), g_ffn.reshape(1, d).astype(F32), wrh, wrl, br,
        _tile(t, 512))

    tmx = MOE_SLOT_TILE
    n_slots = 2 * t + N_EXPERTS * tmx
    counts = cnt[0, EXPERT_LANE0:EXPERT_LANE0 + N_EXPERTS].astype(jnp.int32)
    padded = (counts + tmx - 1) // tmx * tmx
    ends = jnp.cumsum(padded)
    pos = _slots(rt, cnt, tmx, _tile(t, 2048))
    pos1, pos2 = pos[0], pos[1]
    tile_start = jnp.arange(n_slots // tmx, dtype=jnp.int32) * tmx
    tile_expert = jnp.minimum(jnp.sum(tile_start[:, None] >= ends[None, :], axis=1), N_EXPERTS - 1).astype(jnp.int32)
    n_used = (ends[-1:] // tmx).astype(jnp.int32)
    last_of_expert = jnp.any(jnp.logical_and(tile_start[:, None] + tmx == ends[None, :], padded[None, :] > 0), axis=1)
    zflag = jnp.logical_or(last_of_expert, tile_start >= ends[-1]).astype(jnp.int32)

    def tiled_pos(tile):
        n = t // tile
        return jnp.concatenate([pos1.reshape(n, 1, tile), pos2.reshape(n, 1, tile)], axis=2)

    td = _tile(t, 1024)
    nsub = d // LANES
    xs = _dispatch(zflag, tpk, tiled_pos(td), n_slots, td, tmx, nsub)
    wgu = jnp.concatenate([w_gate, w_up], axis=2).astype(BF16)
    ys = _experts(tile_expert, n_used, xs, wgu, w_down.astype(BF16), tmx)

    tf = _tile(t, 256)
    out = _final(tiled_pos(tf), x1, rt, p_l.reshape(t, -1), g_ple.reshape(1, d).astype(F32),
                 w_ple_gate.astype(BF16), w_ple_proj.astype(BF16), g_post.reshape(1, d).astype(F32), ys, tf)
    return out.reshape(b, s, d)


def kernel(x, p, g_mix, w_in, b_forget, conv_w, a_log, dt_bias, g_onorm, w_o_fox, w_o_delta, w_out,
           g_ffn, w_group, b_group, w_router, b_router, w_gate, w_up, w_down, g_ple, w_ple_gate, w_ple_proj,
           g_final):
    depth = p.shape[0]
    assert depth == 1, "the final rmsnorm is fused into the last layer's epilogue; depth 1 only"
    i = 0
    return _layer(x, p[i], g_mix[i], w_in[i], b_forget[i], conv_w[i], a_log[i], dt_bias[i], g_onorm[i],
                  w_o_fox[i], w_o_delta[i], w_out[i], g_ffn[i], w_group[i], b_group[i], w_router[i],
                  b_router[i], w_gate[i], w_up[i], w_down[i], g_ple[i], w_ple_gate[i], w_ple_proj[i],
                  g_final)
```

```python
import functools

import jax
import jax.numpy as jnp
from jax import lax
from jax.experimental import pallas as pl
from jax.experimental.pallas import tpu as pltpu

F32 = jnp.float32
BF16 = jnp.bfloat16
EPS = 1e-6
NEG = -1e30

FOX_HEADS = 8
FOX_HEAD_DIM = 64
FOX_WIDTH = FOX_HEADS * FOX_HEAD_DIM
GDN_HEADS = 4
GDN_HEAD_DIM = 128
GDN_CONV = 4
GDN_CHUNK = 64
GDN_WIDTH = GDN_HEADS * GDN_HEAD_DIM
N_GROUPS = 4
EXPERTS_PER_GROUP = 8
N_EXPERTS = N_GROUPS * EXPERTS_PER_GROUP
EXPERT_FF = 256
LANES = 128
EXPERT_LANE0 = N_GROUPS

VMEM_LIMIT_BYTES = 56 * 1024 * 1024


def _cparams(sem):
    return pltpu.CompilerParams(dimension_semantics=sem, vmem_limit_bytes=VMEM_LIMIT_BYTES)


def _rms(x, g):
    return x * lax.rsqrt(jnp.mean(x * x, axis=-1, keepdims=True) + EPS) * g


def _sigmoid(x):
    return 1.0 / (1.0 + jnp.exp(-x))


def _dot(a, b):
    return jnp.dot(a, b, preferred_element_type=F32)


def _dot_nt(a, b):
    return lax.dot_general(a, b, (((1,), (1,)), ((), ())), preferred_element_type=F32)


def _dot_tn(a, b):
    return lax.dot_general(a, b, (((0,), (0,)), ((), ())), preferred_element_type=F32)


def _split3(v):
    hi = v.astype(BF16)
    r = v - hi.astype(F32)
    mid = r.astype(BF16)
    lo = (r - mid.astype(F32)).astype(BF16)
    return hi, mid, lo


def _dot_exact_lhs01(mat01, v):
    hi, mid, lo = _split3(v)
    return _dot(mat01, hi) + _dot(mat01, mid) + _dot(mat01, lo)


def _inproj_body(x_ref, g_ref, wq_ref, wg_ref, wzs_ref, wgf_ref, wgd_ref,
                 oq_ref, og_ref, oz_ref, osm_ref, ogf_ref, ogd_ref):
    hb = _rms(x_ref[...], g_ref[...]).astype(BF16)
    oq_ref[...] = _dot(hb, wq_ref[...]).astype(BF16)
    og_ref[...] = _dot(hb, wg_ref[...]).astype(BF16)
    zs = _dot(hb, wzs_ref[...])
    oz_ref[...] = zs[:, :GDN_WIDTH].astype(BF16)
    osm_ref[...] = zs[:, GDN_WIDTH:]
    ogf_ref[...] = _dot(hb, wgf_ref[...]).astype(BF16)
    ogd_ref[...] = _dot(hb, wgd_ref[...]).astype(BF16)


def _inproj(x2d, g_mix, wq, wg, wzs, wgf, wgd, tm):
    t, d = x2d.shape
    row = lambda i: (i, 0)
    const = lambda i: (0, 0)
    widths = (3 * FOX_WIDTH, 3 * GDN_WIDTH, GDN_WIDTH, LANES, d, d)
    dtypes = (BF16, BF16, BF16, F32, BF16, BF16)
    return pl.pallas_call(
        _inproj_body,
        grid=(t // tm,),
        in_specs=[pl.BlockSpec((tm, d), row), pl.BlockSpec((1, d), const)]
        + [pl.BlockSpec(w.shape, const) for w in (wq, wg, wzs, wgf, wgd)],
        out_specs=[pl.BlockSpec((tm, n), row) for n in widths],
        out_shape=[jax.ShapeDtypeStruct((t, n), dt) for n, dt in zip(widths, dtypes)],
        compiler_params=_cparams(("parallel",)),
        name="inproj",
    )(x2d, g_mix, wq, wg, wzs, wgf, wgd)


def _gates_tile(sm, par_ref, carry_ref):
    tp = sm.shape[0]
    lane = lax.broadcasted_iota(jnp.int32, sm.shape, 1)
    z = sm + par_ref[0:1, :]
    soft = jnp.log1p(jnp.exp(-jnp.abs(z)))
    softplus = jnp.maximum(z, 0.0) + soft
    logf = jnp.minimum(z, 0.0) - soft
    g = -jnp.exp(par_ref[1:2, :]) * softplus
    beta = _sigmoid(sm)
    is_f = lane < FOX_HEADS
    is_g = jnp.logical_and(lane >= FOX_HEADS, lane < FOX_HEADS + GDN_HEADS)
    is_b = jnp.logical_and(lane >= FOX_HEADS + GDN_HEADS, lane < FOX_HEADS + 2 * GDN_HEADS)
    val = jnp.where(is_f, logf, jnp.where(is_g, g, 0.0))
    r = lax.broadcasted_iota(jnp.int32, (tp, tp), 0)
    c = lax.broadcasted_iota(jnp.int32, (tp, tp), 1)
    lower = r >= c
    tri = jnp.where(lower, 1.0, 0.0).astype(BF16)
    same_chunk = (r // GDN_CHUNK) == (c // GDN_CHUNK)
    tri_chunk = jnp.where(jnp.logical_and(lower, same_chunk), 1.0, 0.0).astype(BF16)
    cum = _dot_exact_lhs01(tri, val) + carry_ref[...]
    gcs = _dot_exact_lhs01(tri_chunk, val)
    carry_ref[...] = cum[tp - 1:tp, :]
    return jnp.where(is_f, cum, jnp.where(is_g, gcs, jnp.where(is_b, beta, 0.0)))


LOG2E = 1.4426950408889634


FOX_VROWS = FOX_HEAD_DIM + 16


def _split3_f32(v):
    hi = v.astype(BF16).astype(F32)
    r = v - hi
    mid = r.astype(BF16).astype(F32)
    lo = (r - mid).astype(BF16).astype(F32)
    return hi, mid, lo


def _fox_body(q_ref, k_ref, v_ref, col_ref, o_ref, kx_ref, vt_ref, sa_ref, sb_ref, acc_ref, m_ref,
              *, tq, s_len):
    hp = pl.program_id(1)
    qi = pl.program_id(2)
    hd = FOX_HEAD_DIM

    @pl.when(qi == 0)
    def _():
        def prep(ci, carry):
            r0 = pl.multiple_of(ci * tq, tq)
            kx_ref[pl.ds(r0, tq), :LANES] = k_ref[0, pl.ds(r0, tq), :]
            col = col_ref[0, pl.ds(r0, tq), :]
            lane = lax.broadcasted_iota(jnp.int32, col.shape, 1)
            bias = jnp.zeros(col.shape, F32)
            for hh in range(2):
                ck = jnp.sum(jnp.where(lane == 2 * hp + hh, col, 0.0), axis=-1, keepdims=True) * LOG2E
                for k, piece in enumerate(_split3_f32(ck)):
                    bias = jnp.where(lane == 3 * hh + k, piece, bias)
            kx_ref[pl.ds(r0, tq), LANES:] = bias.astype(BF16)
            vt = v_ref[0, pl.ds(r0, tq), :].astype(F32).T
            ones_row = jnp.where(lax.broadcasted_iota(jnp.int32, (FOX_VROWS - hd, tq), 0) == 0, 1.0, 0.0)
            for hh in range(2):
                vt_ref[hh, :hd, pl.ds(r0, tq)] = vt[hh * hd:(hh + 1) * hd].astype(BF16)
                vt_ref[hh, hd:, pl.ds(r0, tq)] = ones_row.astype(BF16)
            return carry

        lax.fori_loop(0, s_len // tq, prep, 0)

    lane = lax.broadcasted_iota(jnp.int32, (tq, LANES), 1)
    q = (q_ref[0].astype(F32) * (hd ** -0.5 * LOG2E)).astype(BF16)
    zero = jnp.zeros_like(q)
    qx = []
    for hh in range(2):
        qh = jnp.where(jnp.logical_and(lane >= hh * hd, lane < (hh + 1) * hd), q, zero)
        sel = jnp.where(jnp.logical_and(lane >= 3 * hh, lane < 3 * hh + 3), -1.0, 0.0).astype(BF16)
        qx.append(jnp.concatenate([qh, sel], axis=1))
    m_ref[...] = jnp.full(m_ref.shape, NEG, F32)
    acc_ref[...] = jnp.zeros(acc_ref.shape, F32)

    def scores(j, dst_ref):
        start = pl.multiple_of(j * tq, tq)
        kx = kx_ref[pl.ds(start, tq), :]
        for hh in range(2):
            dst_ref[hh] = _dot_nt(kx, qx[hh])

    def consume(j, src_ref, masked):
        start = pl.multiple_of(j * tq, tq)
        for hh in range(2):
            s = src_ref[hh]
            if masked:
                key = lax.broadcasted_iota(jnp.int32, (tq, tq), 0)
                qry = lax.broadcasted_iota(jnp.int32, (tq, tq), 1)
                s = jnp.where(key <= qry, s, NEG)
            m_old = m_ref[hh:hh + 1, :]
            m_new = jnp.maximum(m_old, jnp.max(s, axis=0, keepdims=True))
            alpha = jnp.exp2(m_old - m_new)
            p = jnp.exp2(s - m_new).astype(BF16)
            acc_ref[hh] = alpha * acc_ref[hh] + _dot(vt_ref[hh, :, pl.ds(start, tq)], p)
            m_ref[hh:hh + 1, :] = m_new

    scores(0, sa_ref)

    def loop_body(j, carry):
        @pl.when(j % 2 == 0)
        def _():
            scores(j + 1, sb_ref)
            consume(j, sa_ref, False)

        @pl.when(j % 2 == 1)
        def _():
            scores(j + 1, sa_ref)
            consume(j, sb_ref, False)

        return carry

    lax.fori_loop(0, qi, loop_body, 0)

    @pl.when(qi % 2 == 0)
    def _():
        consume(qi, sa_ref, True)

    @pl.when(qi % 2 == 1)
    def _():
        consume(qi, sb_ref, True)

    out_t = jnp.concatenate([acc_ref[hh, :hd] / acc_ref[hh, hd:hd + 1] for hh in range(2)], axis=0)
    o_ref[0] = out_t.T.astype(BF16)


def _fox(zq3d, col3d, tq):
    b, s, _ = zq3d.shape
    npair = FOX_HEADS // 2
    kblk = FOX_WIDTH // LANES
    return pl.pallas_call(
        functools.partial(_fox_body, tq=tq, s_len=s),
        grid=(b, npair, s // tq),
        in_specs=[pl.BlockSpec((1, tq, LANES), lambda bi, hp, qi: (bi, qi, hp)),
                  pl.BlockSpec((1, s, LANES), lambda bi, hp, qi: (bi, 0, kblk + hp)),
                  pl.BlockSpec((1, s, LANES), lambda bi, hp, qi: (bi, 0, 2 * kblk + hp)),
                  pl.BlockSpec((1, s, LANES), lambda bi, hp, qi: (bi, 0, 0))],
        out_specs=pl.BlockSpec((1, tq, LANES), lambda bi, hp, qi: (bi, qi, hp)),
        out_shape=jax.ShapeDtypeStruct((b, s, FOX_WIDTH), BF16),
        scratch_shapes=[pltpu.VMEM((s, 2 * LANES), BF16), pltpu.VMEM((2, FOX_VROWS, s), BF16),
                        pltpu.VMEM((2, tq, tq), F32), pltpu.VMEM((2, tq, tq), F32),
                        pltpu.VMEM((2, FOX_VROWS, tq), F32), pltpu.VMEM((2, tq), F32)],
        compiler_params=_cparams(("parallel", "parallel", "arbitrary")),
        name="fox",
    )(zq3d, zq3d, zq3d, col3d)


HALO = 16


def _gdnprep_body(x_ref, halo_ref, cw_ref, sm_ref, par_ref, oq_ref, ok_ref, ov_ref, ocol_ref, ext_ref, carry_ref,
                  *, tp):
    @pl.when(pl.program_id(1) == 0)
    def _():
        carry_ref[...] = jnp.zeros_like(carry_ref)

    ocol_ref[0] = _gates_tile(sm_ref[0], par_ref, carry_ref)
    prev = halo_ref[0].astype(F32)
    ext_ref[0:HALO, :] = jnp.where(pl.program_id(1) > 0, prev, 0.0)
    ext_ref[HALO:, :] = x_ref[0].astype(F32)
    acc = cw_ref[GDN_CONV - 1:GDN_CONV, :] * ext_ref[HALO:HALO + tp, :]
    for j in range(GDN_CONV - 1):
        off = HALO - (GDN_CONV - 1) + j
        acc = acc + cw_ref[j:j + 1, :] * ext_ref[off:off + tp, :]
    y = acc * _sigmoid(acc)

    def l2(v):
        return v * lax.rsqrt(jnp.sum(v * v, axis=-1, keepdims=True) + EPS)

    for h in range(GDN_HEADS):
        lo, hi = h * GDN_HEAD_DIM, (h + 1) * GDN_HEAD_DIM
        oq_ref[0, :, lo:hi] = (l2(y[:, lo:hi]) * GDN_HEAD_DIM ** -0.5).astype(BF16)
        ok_ref[0, :, lo:hi] = l2(y[:, GDN_WIDTH + lo:GDN_WIDTH + hi]).astype(BF16)
    ov_ref[0] = y[:, 2 * GDN_WIDTH:].astype(BF16)


def _gdnprep(zg3d, conv_w, small3d, par, tp):
    b, s, c = zg3d.shape
    blk = lambda bi, i: (bi, i, 0)
    return pl.pallas_call(
        functools.partial(_gdnprep_body, tp=tp),
        grid=(b, s // tp),
        in_specs=[pl.BlockSpec((1, tp, c), blk),
                  pl.BlockSpec((1, HALO, c), lambda bi, i: (bi, jnp.maximum(i * (tp // HALO) - 1, 0), 0)),
                  pl.BlockSpec(conv_w.shape, lambda bi, i: (0, 0)),
                  pl.BlockSpec((1, tp, LANES), blk), pl.BlockSpec(par.shape, lambda bi, i: (0, 0))],
        out_specs=[pl.BlockSpec((1, tp, GDN_WIDTH), blk)] * 3 + [pl.BlockSpec((1, tp, LANES), blk)],
        out_shape=[jax.ShapeDtypeStruct((b, s, GDN_WIDTH), BF16)] * 3 + [jax.ShapeDtypeStruct((b, s, LANES), F32)],
        scratch_shapes=[pltpu.VMEM((tp + HALO, c), F32), pltpu.VMEM((1, LANES), F32)],
        compiler_params=_cparams(("parallel", "arbitrary")),
        name="gdnprep",
    )(zg3d, zg3d, conv_w, small3d, par)


def _stack_heads(x):
    return jnp.concatenate([x[:, h * GDN_HEAD_DIM:(h + 1) * GDN_HEAD_DIM] for h in range(GDN_HEADS)], axis=0)


def _gdn_body(q_ref, k_ref, v_ref, col_ref, grow_ref, o_ref, state_ref, *bufs, tg):
    C = GDN_CHUNK
    R = GDN_HEADS * C
    dh = GDN_HEAD_DIM
    n_chunks = tg // C
    step = pl.program_id(1)
    buf_sets = (bufs[:len(bufs) // 2], bufs[len(bufs) // 2:])

    @pl.when(step == 0)
    def _():
        for ref in (state_ref,) + tuple(bufs):
            ref[...] = jnp.zeros_like(ref)

    r = lax.broadcasted_iota(jnp.int32, (R, R), 0)
    c = lax.broadcasted_iota(jnp.int32, (R, R), 1)
    same_head = (r // C) == (c // C)
    lower = jnp.logical_and(same_head, r >= c)
    strict = jnp.logical_and(same_head, r > c)
    gc_lane0 = FOX_HEADS
    beta_lane0 = FOX_HEADS + GDN_HEADS

    def advance(ci, rd):
        u_ref, w_ref, intra_ref, qd_ref, kd_ref, gl_ref = rd
        r0 = ci * C
        u, w, intra = u_ref[ci], w_ref[ci], intra_ref[ci]
        q_dec, k_dec = qd_ref[ci], kd_ref[ci]
        v_new = []
        o_state = []
        for h in range(GDN_HEADS):
            sl = slice(h * C, (h + 1) * C)
            st = state_ref[h].astype(BF16)
            v_new.append(u[sl] - _dot(w[sl], st))
            o_state.append(_dot(q_dec[sl], st))
        v_new = jnp.concatenate(v_new, axis=0)
        v_new_b = v_new.astype(BF16)
        o_all = jnp.concatenate(o_state, axis=0) + _dot(intra, v_new_b)
        for h in range(GDN_HEADS):
            sl = slice(h * C, (h + 1) * C)
            state_ref[h] = state_ref[h] * gl_ref[ci, h:h + 1, :] + _dot_tn(k_dec[sl], v_new_b[sl])
        o_ref[0, r0:r0 + C, :] = jnp.concatenate(
            [o_all[h * C:(h + 1) * C] for h in range(GDN_HEADS)], axis=1).astype(BF16)

    chunks = range(n_chunks)

    def prepare_all(wr):
        u_ref, w_ref, intra_ref, qd_ref, kd_ref, gl_ref = wr
        qs, ks, vs, gc_col, beta_col, gl_row, lmat, intra = [], [], [], [], [], [], [], []
        for ci in chunks:
            r0 = ci * C
            qs.append(_stack_heads(q_ref[0, r0:r0 + C, :]).astype(F32))
            ks.append(_stack_heads(k_ref[0, r0:r0 + C, :]).astype(F32))
            vs.append(_stack_heads(v_ref[0, r0:r0 + C, :]).astype(F32))
            col = col_ref[0, r0:r0 + C, :]
            gc_col.append(
                jnp.concatenate([col[:, gc_lane0 + h:gc_lane0 + h + 1] for h in range(GDN_HEADS)], axis=0))
            beta_col.append(
                jnp.concatenate([col[:, beta_lane0 + h:beta_lane0 + h + 1] for h in range(GDN_HEADS)], axis=0))
            gl_row.append(col[C - 1:C, :])
            gc_row = grow_ref[0, ci, :, :]
            decay = jnp.exp(jnp.where(lower, gc_col[ci] - gc_row, NEG))
            ksb = ks[ci].astype(BF16)
            kk = _dot_nt(ksb, ksb)
            qk = _dot_nt(qs[ci].astype(BF16), ksb)
            lmat.append(jnp.where(strict, kk * decay * beta_col[ci], 0.0))
            intra.append((qk * decay).astype(BF16))
        def side_by_side(bd):
            return functools.reduce(lambda a, b: a + b, [bd[h * C:(h + 1) * C] for h in range(GDN_HEADS)])

        def block_diag(sbs):
            return jnp.where(same_head, jnp.concatenate([sbs] * GDN_HEADS, axis=0), 0.0)

        pw_bd = [l.astype(BF16) for l in lmat]
        pw = [side_by_side(l) for l in lmat]
        n_sbs = [-p for p in pw]
        for _ in range(5):
            pw = [_dot(p.astype(BF16), b) for p, b in zip(pw, pw_bd)]
            pw_bd = [block_diag(p).astype(BF16) for p in pw]
            n_sbs = [n + p + _dot(n.astype(BF16), b) for n, p, b in zip(n_sbs, pw, pw_bd)]
        n_mat = [block_diag(n) for n in n_sbs]
        for ci in chunks:
            e_gc = jnp.exp(gc_col[ci])
            rhs = jnp.concatenate([vs[ci] * beta_col[ci], ks[ci] * (beta_col[ci] * e_gc)], axis=1)
            sol = rhs + _dot(n_mat[ci].astype(BF16), rhs.astype(BF16))
            gl_col = jnp.concatenate(
                [jnp.broadcast_to(gl_row[ci][:, gc_lane0 + h:gc_lane0 + h + 1], (C, 1))
                 for h in range(GDN_HEADS)], axis=0)
            u_ref[ci] = sol[:, :dh]
            w_ref[ci] = sol[:, dh:].astype(BF16)
            intra_ref[ci] = intra[ci]
            qd_ref[ci] = (qs[ci] * e_gc).astype(BF16)
            kd_ref[ci] = (ks[ci] * jnp.exp(gl_col - gc_col[ci])).astype(BF16)
            for h in range(GDN_HEADS):
                gl_ref[ci, h:h + 1, :] = jnp.broadcast_to(
                    jnp.exp(gl_row[ci][:, gc_lane0 + h:gc_lane0 + h + 1]), (1, LANES))

    def run(rd, wr):
        for ci in chunks:
            advance(ci, rd)
        prepare_all(wr)

    @pl.when(step % 2 == 0)
    def _():
        run(buf_sets[0], buf_sets[1])

    @pl.when(step % 2 == 1)
    def _():
        run(buf_sets[1], buf_sets[0])


def _gdn(qn, kn, vv, col3d, gcrow, tg):
    b, s, _ = qn.shape
    n = s // tg
    nck = tg // GDN_CHUNK
    rows = GDN_HEADS * GDN_CHUNK
    dh = GDN_HEAD_DIM
    blk_in = lambda bi, i: (bi, jnp.minimum(i, n - 1), 0)
    blk_out = lambda bi, i: (bi, jnp.maximum(i - 1, 0), 0)
    return pl.pallas_call(
        functools.partial(_gdn_body, tg=tg),
        grid=(b, n + 1),
        in_specs=[pl.BlockSpec((1, tg, GDN_WIDTH), blk_in)] * 3
        + [pl.BlockSpec((1, tg, LANES), blk_in),
           pl.BlockSpec((1, nck, 1, rows), lambda bi, i: (bi, jnp.minimum(i, n - 1), 0, 0))],
        out_specs=pl.BlockSpec((1, tg, GDN_WIDTH), blk_out),
        out_shape=jax.ShapeDtypeStruct((b, s, GDN_WIDTH), BF16),
        scratch_shapes=[pltpu.VMEM((GDN_HEADS, dh, dh), F32)] + 2 * [
            pltpu.VMEM((nck, rows, dh), F32), pltpu.VMEM((nck, rows, dh), BF16),
            pltpu.VMEM((nck, rows, rows), BF16), pltpu.VMEM((nck, rows, dh), BF16),
            pltpu.VMEM((nck, rows, dh), BF16), pltpu.VMEM((nck, 8, LANES), F32)],
        compiler_params=_cparams(("parallel", "arbitrary")),
        name="gdn",
    )(qn, kn, vv, col3d, gcrow)


MOE_SLOT_TILE = 512


def _rows_to_tiles(ref, v):
    m, width = v.shape
    n = width // LANES
    for s in range(n):
        ref[pl.ds(s, m, stride=n), :] = v[:, s * LANES:(s + 1) * LANES]


def _tiles_to_rows(ref, m):
    n = ref.shape[0] // m
    return jnp.concatenate([ref[pl.ds(s, m, stride=n), :] for s in range(n)], axis=1)


RT_E1, RT_E2, RT_RANK1, RT_RANK2, RT_G1, RT_G2 = range(6)


def _combine_body(yf_ref, og_ref, dz_ref, gf_ref, gd_ref, x_ref, wof_ref, wod_ref, wout_ref,
                  gon_ref, gffn_ref, wrh_ref, wrl_ref, br_ref, x1_ref, t_ref, rt_ref, cnt_ref, carry_ref,
                  *, tc):
    @pl.when(pl.program_id(0) == 0)
    def _():
        carry_ref[...] = jnp.zeros_like(carry_ref)

    on = []
    for h in range(GDN_HEADS):
        sl = slice(h * GDN_HEAD_DIM, (h + 1) * GDN_HEAD_DIM)
        dz = dz_ref[:, sl].astype(F32)
        on.append(_rms(og_ref[:, sl].astype(F32), gon_ref[...]) * (dz * _sigmoid(dz)))
    on = jnp.concatenate(on, axis=1).astype(BF16)
    y_fox = _dot(yf_ref[...], wof_ref[...])
    y_delta = _dot(on, wod_ref[...])
    merged = _sigmoid(gf_ref[...].astype(F32)) * y_fox + _sigmoid(gd_ref[...].astype(F32)) * y_delta
    x1 = x_ref[...] + _dot(merged.astype(BF16), wout_ref[...])
    x1_ref[...] = x1
    t32 = _rms(x1, gffn_ref[...])
    th = t32.astype(BF16)
    tl = (t32 - th.astype(F32)).astype(BF16)
    _rows_to_tiles(t_ref, t32)
    logits = _dot(th, wrh_ref[...]) + _dot(tl, wrh_ref[...]) + _dot(th, wrl_ref[...]) + br_ref[...]
    lane = lax.broadcasted_iota(jnp.int32, logits.shape, 1)
    gl = jnp.where(lane < N_GROUPS, logits, NEG)
    gmax = jnp.max(gl, axis=-1, keepdims=True)
    g_sel = jnp.min(jnp.where(gl == gmax, lane, LANES), axis=-1, keepdims=True)
    p_sel = 1.0 / jnp.sum(jnp.exp(gl - gmax), axis=-1, keepdims=True)
    lo = EXPERT_LANE0 + EXPERTS_PER_GROUP * g_sel
    in_grp = jnp.logical_and(lane >= lo, lane < lo + EXPERTS_PER_GROUP)
    el = jnp.where(in_grp, logits, NEG)
    emax = jnp.max(el, axis=-1, keepdims=True)
    ee = jnp.where(in_grp, jnp.exp(el - emax), 0.0)
    pe = ee / jnp.sum(ee, axis=-1, keepdims=True)
    pe = jnp.where(in_grp, pe, -1.0)
    p1 = jnp.max(pe, axis=-1, keepdims=True)
    i1 = jnp.min(jnp.where(pe == p1, lane, LANES), axis=-1, keepdims=True)
    pe2 = jnp.where(lane == i1, -1.0, pe)
    p2 = jnp.max(pe2, axis=-1, keepdims=True)
    i2 = jnp.min(jnp.where(pe2 == p2, lane, LANES), axis=-1, keepdims=True)
    den = p1 + p2
    hit1 = lane == i1
    hit2 = lane == i2
    assign = jnp.where(jnp.logical_or(hit1, hit2), 1.0, 0.0)
    r = lax.broadcasted_iota(jnp.int32, (tc, tc), 0)
    c = lax.broadcasted_iota(jnp.int32, (tc, tc), 1)
    before = jnp.where(r > c, 1.0, 0.0).astype(BF16)
    prefix = _dot(before, assign.astype(BF16)) + carry_ref[...]
    rank1 = jnp.sum(jnp.where(hit1, prefix, 0.0), axis=-1, keepdims=True)
    rank2 = jnp.sum(jnp.where(hit2, prefix, 0.0), axis=-1, keepdims=True)
    carry_ref[...] = prefix[tc - 1:tc, :] + assign[tc - 1:tc, :]
    cnt_ref[...] = carry_ref[...]
    cols = ((i1 - EXPERT_LANE0).astype(F32), (i2 - EXPERT_LANE0).astype(F32), rank1, rank2,
            p_sel * (p1 / den), p_sel * (p2 / den))
    rt = jnp.zeros(logits.shape, F32)
    for k, v in enumerate(cols):
        rt = jnp.where(lane == k, v, rt)
    rt_ref[...] = rt


def _combine(yf, og, dz, gf, gd, x2d, wof, wod, wout, g_on, g_ffn, wrh, wrl, br, tc):
    t, d = x2d.shape
    row = lambda i: (i, 0)
    const = lambda i: (0, 0)
    acts = (yf, og, dz, gf, gd, x2d)
    consts = (wof, wod, wout, g_on, g_ffn, wrh, wrl, br)
    return pl.pallas_call(
        functools.partial(_combine_body, tc=tc),
        grid=(t // tc,),
        in_specs=[pl.BlockSpec((tc, a.shape[1]), row) for a in acts]
        + [pl.BlockSpec(c.shape, const) for c in consts],
        out_specs=[pl.BlockSpec((tc, d), row), pl.BlockSpec((tc * (d // LANES), LANES), row),
                   pl.BlockSpec((tc, LANES), row), pl.BlockSpec((1, LANES), const)],
        out_shape=[jax.ShapeDtypeStruct((t, d), F32), jax.ShapeDtypeStruct((t * (d // LANES), LANES), F32),
                   jax.ShapeDtypeStruct((t, LANES), F32), jax.ShapeDtypeStruct((1, LANES), F32)],
        scratch_shapes=[pltpu.VMEM((1, LANES), F32)],
        compiler_params=_cparams(("arbitrary",)),
        name="combine",
    )(*acts, *consts)


def _slots_body(rt_ref, cnt_ref, o_ref, *, tmx):
    lane1 = lax.broadcasted_iota(jnp.int32, (1, LANES), 1)
    is_expert = jnp.logical_and(lane1 >= EXPERT_LANE0, lane1 < EXPERT_LANE0 + N_EXPERTS)
    padded = jnp.where(is_expert, jnp.floor((cnt_ref[...] + (tmx - 1)) / tmx) * tmx, 0.0)
    r = lax.broadcasted_iota(jnp.int32, (LANES, LANES), 0)
    c = lax.broadcasted_iota(jnp.int32, (LANES, LANES), 1)
    before = jnp.where(r < c, 1.0, 0.0).astype(BF16)
    hi, mid, lo = _split3(jnp.broadcast_to(padded, (8, LANES)))
    offs = (_dot(hi, before) + _dot(mid, before) + _dot(lo, before))[0:1, :]
    rt = rt_ref[...]
    lane = lax.broadcasted_iota(jnp.int32, rt.shape, 1)
    expert_of_lane = (lane - EXPERT_LANE0).astype(F32)
    out = jnp.zeros(rt.shape, F32)
    for k, (ce, cr) in enumerate(((RT_E1, RT_RANK1), (RT_E2, RT_RANK2))):
        start = jnp.sum(jnp.where(expert_of_lane == rt[:, ce:ce + 1], offs, 0.0), axis=-1, keepdims=True)
        out = jnp.where(lane == k, start + rt[:, cr:cr + 1], out)
    o_ref[...] = out.T[:8, :].astype(jnp.int32)


def _slots(rt, cnt, tmx, ts):
    t = rt.shape[0]
    return pl.pallas_call(
        functools.partial(_slots_body, tmx=tmx),
        grid=(t // ts,),
        in_specs=[pl.BlockSpec((ts, LANES), lambda i: (i, 0)), pl.BlockSpec((1, LANES), lambda i: (0, 0))],
        out_specs=pl.BlockSpec((8, ts), lambda i: (0, i)),
        out_shape=jax.ShapeDtypeStruct((8, t), jnp.int32),
        compiler_params=_cparams(("parallel",)),
        name="slots",
    )(rt, cnt)


ROW_GROUP = 8


def _issue_row_copies(pos_ref, n_rows, nsub, make_copy):
    def issue(g, carry):
        r0 = g * ROW_GROUP
        slots = [[pos_ref[0, 0, k * n_rows + r0 + u] for k in range(2)] for u in range(ROW_GROUP)]
        for u in range(ROW_GROUP):
            tok = pl.ds(pl.multiple_of((r0 + u) * nsub, nsub), nsub)
            for k in range(2):
                slot = pl.ds(pl.multiple_of(slots[u][k] * nsub, nsub), nsub)
                make_copy(k, tok, slot).start(priority=k)
        return carry

    lax.fori_loop(0, n_rows // ROW_GROUP, issue, 0)


def _dispatch_body(zflag_ref, pos_ref, t_ref, xs_ref, zbuf, sem, zsem, *, td, tmx, n_tiles, nsub):
    @pl.when(pl.program_id(0) == 0)
    def _():
        zbuf[...] = jnp.zeros_like(zbuf)
        rows = tmx * nsub

        def zero_tile(k, carry):
            @pl.when(zflag_ref[k] != 0)
            def _():
                cp = pltpu.make_async_copy(zbuf, xs_ref.at[pl.ds(pl.multiple_of(k * rows, rows), rows), :], zsem)
                cp.start()
                cp.wait()

            return carry

        lax.fori_loop(0, n_tiles, zero_tile, 0)

    _issue_row_copies(pos_ref, td, nsub,
                      lambda k, tok, slot: pltpu.make_async_copy(t_ref.at[tok, :], xs_ref.at[slot, :], sem))
    for _ in range(2):
        pltpu.make_async_copy(t_ref, xs_ref.at[pl.ds(0, td * nsub), :], sem).wait()


def _dispatch(zflag, t_tiles, pos12, n_slots, td, tmx, nsub):
    t = t_tiles.shape[0] // nsub
    n_tiles = n_slots // tmx
    grid_spec = pltpu.PrefetchScalarGridSpec(
        num_scalar_prefetch=1,
        grid=(t // td,),
        in_specs=[pl.BlockSpec((1, 1, 2 * td), lambda i, zf: (i, 0, 0), memory_space=pltpu.SMEM),
                  pl.BlockSpec((td * nsub, LANES), lambda i, zf: (i, 0))],
        out_specs=pl.BlockSpec(memory_space=pl.ANY),
        scratch_shapes=[pltpu.VMEM((tmx * nsub, LANES), F32), pltpu.SemaphoreType.DMA(()),
                        pltpu.SemaphoreType.DMA(())],
    )
    return pl.pallas_call(
        functools.partial(_dispatch_body, td=td, tmx=tmx, n_tiles=n_tiles, nsub=nsub),
        grid_spec=grid_spec,
        out_shape=jax.ShapeDtypeStruct((n_slots * nsub, LANES), F32),
        compiler_params=_cparams(("arbitrary",)),
        name="dispatch",
    )(zflag, pos12, t_tiles)


def _experts_body(te_ref, nused_ref, xs_ref, wgu_ref, wd_ref, ys_ref, *, tmx):
    @pl.when(pl.program_id(0) < nused_ref[0])
    def _():
        hgu = _dot(_tiles_to_rows(xs_ref, tmx).astype(BF16), wgu_ref[0])
        a = hgu[:, :EXPERT_FF]
        hid = a * _sigmoid(a) * hgu[:, EXPERT_FF:]
        _rows_to_tiles(ys_ref, _dot(hid.astype(BF16), wd_ref[0]))

    @pl.when(pl.program_id(0) >= nused_ref[0])
    def _():
        ys_ref[...] = jnp.zeros_like(ys_ref)


def _experts(tile_expert, n_used, xs, wgu, wd, tmx):
    d = wgu.shape[1]
    nsub = d // LANES
    n_slots = xs.shape[0] // nsub
    grid_spec = pltpu.PrefetchScalarGridSpec(
        num_scalar_prefetch=2,
        grid=(n_slots // tmx,),
        in_specs=[pl.BlockSpec((tmx * nsub, LANES), lambda i, te, nu: (i, 0)),
                  pl.BlockSpec((1, d, 2 * EXPERT_FF), lambda i, te, nu: (te[i], 0, 0)),
                  pl.BlockSpec((1, EXPERT_FF, d), lambda i, te, nu: (te[i], 0, 0))],
        out_specs=pl.BlockSpec((tmx * nsub, LANES), lambda i, te, nu: (i, 0)),
    )
    return pl.pallas_call(
        functools.partial(_experts_body, tmx=tmx),
        grid_spec=grid_spec,
        out_shape=jax.ShapeDtypeStruct(xs.shape, F32),
        compiler_params=_cparams(("arbitrary",)),
        name="experts",
    )(tile_expert, n_used, xs, wgu, wd)


def _final_body(pos_ref, posn_ref, x_ref, rt_ref, p_ref, gple_ref, wpg_ref, wpp_ref, gfin_ref, ys_ref,
                o_ref, gbuf, sems, *, tf, nsub):
    i = pl.program_id(0)
    n = pl.num_programs(0)
    slot = i % 2

    def gather(pref, dst_slot):
        _issue_row_copies(pref, tf, nsub, lambda k, tok, src: pltpu.make_async_copy(
            ys_ref.at[src, :], gbuf.at[dst_slot, k, tok, :], sems.at[dst_slot]))

    @pl.when(i == 0)
    def _():
        gather(pos_ref, 0)

    @pl.when(i + 1 < n)
    def _():
        gather(posn_ref, 1 - slot)

    for k in range(2):
        pltpu.make_async_copy(ys_ref.at[pl.ds(0, tf * nsub), :], gbuf.at[slot, k], sems.at[slot]).wait()

    rt = rt_ref[...]
    g1 = rt[:, RT_G1:RT_G1 + 1]
    g2 = rt[:, RT_G2:RT_G2 + 1]
    x = x_ref[...] + g1 * _tiles_to_rows(gbuf.at[slot, 0], tf) + g2 * _tiles_to_rows(gbuf.at[slot, 1], tf)
    r = _rms(x, gple_ref[...]).astype(BF16)
    ple_gate = _sigmoid(_dot(r, wpg_ref[...]))
    proj = _dot(p_ref[...].astype(BF16), wpp_ref[...])
    o_ref[...] = _rms(x + ple_gate * proj, gfin_ref[...])


def _final(pos12, x1, rt, p2d, g_ple, wpg, wpp, g_final, ys, tf):
    t, d = x1.shape
    nsub = d // LANES
    n = t // tf
    row = lambda i: (i, 0)
    const = lambda i: (0, 0)
    smem_blk = lambda imap: pl.BlockSpec((1, 1, 2 * tf), imap, memory_space=pltpu.SMEM)
    return pl.pallas_call(
        functools.partial(_final_body, tf=tf, nsub=nsub),
        grid=(n,),
        in_specs=[smem_blk(lambda i: (i, 0, 0)), smem_blk(lambda i: (jnp.minimum(i + 1, n - 1), 0, 0)),
                  pl.BlockSpec((tf, d), row), pl.BlockSpec((tf, LANES), row),
                  pl.BlockSpec((tf, p2d.shape[1]), row),
                  pl.BlockSpec((1, d), const), pl.BlockSpec(wpg.shape, const),
                  pl.BlockSpec(wpp.shape, const), pl.BlockSpec((1, d), const),
                  pl.BlockSpec(memory_space=pl.ANY)],
        out_specs=pl.BlockSpec((tf, d), row),
        out_shape=jax.ShapeDtypeStruct((t, d), F32),
        scratch_shapes=[pltpu.VMEM((2, 2, tf * nsub, LANES), F32), pltpu.SemaphoreType.DMA((2,))],
        compiler_params=_cparams(("arbitrary",)),
        name="final",
    )(pos12, pos12, x1, rt, p2d, g_ple, wpg, wpp, g_final, ys)


def _pad_lanes(v, lane0):
    return jnp.zeros((1, LANES), F32).at[0, lane0:lane0 + v.shape[0]].set(v.astype(F32))


def _tile(n, pref):
    return pref if n % pref == 0 else n


def _layer(x, p_l, g_mix, w_in, b_forget, conv_w, a_log, dt_bias, g_onorm, w_o_fox, w_o_delta, w_out,
           g_ffn, w_group, b_group, w_router, b_router, w_gate, w_up, w_down, g_ple, w_ple_gate, w_ple_proj,
           g_post, apply_post):
    b, s, d = x.shape
    t = b * s
    x2d = x.reshape(t, d)

    o_ff = 3 * FOX_WIDTH
    o_qkv = o_ff + FOX_HEADS
    o_da = o_qkv + 3 * GDN_WIDTH
    o_db = o_da + GDN_HEADS
    o_dz = o_db + GDN_HEADS
    o_gf = o_dz + GDN_WIDTH
    o_gd = o_gf + d
    w_small = jnp.concatenate(
        [w_in[:, o_ff:o_qkv], w_in[:, o_da:o_db], w_in[:, o_db:o_dz],
         jnp.zeros((d, LANES - FOX_HEADS - 2 * GDN_HEADS), w_in.dtype)], axis=1)
    wq = w_in[:, :o_ff].astype(BF16)
    wg = w_in[:, o_qkv:o_da].astype(BF16)
    wzs = jnp.concatenate([w_in[:, o_dz:o_gf], w_small], axis=1).astype(BF16)
    wgf = w_in[:, o_gf:o_gd].astype(BF16)
    wgd = w_in[:, o_gd:].astype(BF16)

    zq, zg, dz, small, gf, gd = _inproj(x2d, g_mix.reshape(1, d), wq, wg, wzs, wgf, wgd, _tile(t, 512))

    par = jnp.concatenate(
        [_pad_lanes(b_forget, 0) + _pad_lanes(dt_bias, FOX_HEADS), _pad_lanes(a_log, FOX_HEADS),
         jnp.zeros((6, LANES), F32)], axis=0)
    qn, kn, vv, col = _gdnprep(zg.reshape(b, s, 3 * GDN_WIDTH), conv_w.astype(F32),
                               small.reshape(b, s, LANES), par, _tile(s, 512))

    nc = s // GDN_CHUNK
    gc = col[:, :, FOX_HEADS:FOX_HEADS + GDN_HEADS].reshape(b, nc, GDN_CHUNK, GDN_HEADS)
    gcrow = jnp.transpose(gc, (0, 1, 3, 2)).reshape(b, nc, 1, GDN_HEADS * GDN_CHUNK)

    y_fox = _fox(zq.reshape(b, s, 3 * FOX_WIDTH), col, _tile(s, 1024))
    o_gdn = _gdn(qn, kn, vv, col, gcrow, _tile(s, 512))

    w_r = jnp.concatenate([w_group, w_router,
                           jnp.zeros((d, LANES - N_GROUPS - N_EXPERTS), w_group.dtype)], axis=1).astype(F32)
    wrh = w_r.astype(BF16)
    wrl = (w_r - wrh.astype(F32)).astype(BF16)
    br = _pad_lanes(b_group, 0) + _pad_lanes(b_router, EXPERT_LANE0)
    x1, tpk, rt, cnt = _combine(
        y_fox.reshape(t, FOX_WIDTH), o_gdn.reshape(t, GDN_WIDTH), dz, gf, gd, x2d,
        w_o_fox.astype(BF16), w_o_delta.astype(BF16), w_out.astype(BF16),
        g_onorm.reshape(1, GDN_HEAD_DIM).astype(F32), g_ffn.reshape(1, d).astype(F32), wrh, wrl, br,
        _tile(t, 512))

    tmx = MOE_SLOT_TILE
    n_slots = 2 * t + N_EXPERTS * tmx
    counts = cnt[0, EXPERT_LANE0:EXPERT_LANE0 + N_EXPERTS].astype(jnp.int32)
    padded = (counts + tmx - 1) // tmx * tmx
    ends = jnp.cumsum(padded)
    pos = _slots(rt, cnt, tmx, _tile(t, 2048))
    pos1, pos2 = pos[0], pos[1]
    tile_start = jnp.arange(n_slots // tmx, dtype=jnp.int32) * tmx
    tile_expert = jnp.minimum(jnp.sum(tile_start[:, None] >= ends[None, :], axis=1), N_EXPERTS - 1).astype(jnp.int32)
    n_used = (ends[-1:] // tmx).astype(jnp.int32)
    last_of_expert = jnp.any(jnp.logical_and(tile_start[:, None] + tmx == ends[None, :], padded[None, :] > 0), axis=1)
    zflag = jnp.logical_or(last_of_expert, tile_start >= ends[-1]).astype(jnp.int32)

    def tiled_pos(tile):
        n = t // tile
        return jnp.concatenate([pos1.reshape(n, 1, tile), pos2.reshape(n, 1, tile)], axis=2)

    td = _tile(t, 2048)
    nsub = d // LANES
    xs = _dispatch(zflag, tpk, tiled_pos(td), n_slots, td, tmx, nsub)
    wgu = jnp.concatenate([w_gate, w_up], axis=2).astype(BF16)
    ys = _experts(tile_expert, n_used, xs, wgu, w_down.astype(BF16), tmx)

    tf = _tile(t, 512)
    out = _final(tiled_pos(tf), x1, rt, p_l.reshape(t, -1), g_ple.reshape(1, d).astype(F32),
                 w_ple_gate.astype(BF16), w_ple_proj.astype(BF16), g_post.reshape(1, d).astype(F32), ys, tf)
    return out.reshape(b, s, d)


def kernel(x, p, g_mix, w_in, b_forget, conv_w, a_log, dt_bias, g_onorm, w_o_fox, w_o_delta, w_out,
           g_ffn, w_group, b_group, w_router, b_router, w_gate, w_up, w_down, g_ple, w_ple_gate, w_ple_proj,
           g_final):
    depth = p.shape[0]
    assert depth == 1, "the final rmsnorm is fused into the last layer's epilogue; depth 1 only"
    i = 0
    return _layer(x, p[i], g_mix[i], w_in[i], b_forget[i], conv_w[i], a_log[i], dt_bias[i], g_onorm[i],
                  w_o_fox[i], w_o_delta[i], w_out[i], g_ffn[i], w_group[i], b_group[i], w_router[i],
                  b_router[i], w_gate[i], w_up[i], w_down[i], g_ple[i], w_ple_gate[i], w_ple_proj[i],
                  g_final, True)
```

```python
import functools

import jax
import jax.numpy as jnp
from jax import lax
from jax.experimental import pallas as pl
from jax.experimental.pallas import tpu as pltpu

F32 = jnp.float32
BF16 = jnp.bfloat16
EPS = 1e-6
NEG = -1e30

FOX_HEADS = 8
FOX_HEAD_DIM = 64
FOX_WIDTH = FOX_HEADS * FOX_HEAD_DIM
GDN_HEADS = 4
GDN_HEAD_DIM = 128
GDN_CONV = 4
GDN_CHUNK = 64
GDN_WIDTH = GDN_HEADS * GDN_HEAD_DIM
N_GROUPS = 4
EXPERTS_PER_GROUP = 8
N_EXPERTS = N_GROUPS * EXPERTS_PER_GROUP
EXPERT_FF = 256
LANES = 128
EXPERT_LANE0 = N_GROUPS

VMEM_LIMIT_BYTES = 56 * 1024 * 1024


def _cparams(sem):
    return pltpu.CompilerParams(dimension_semantics=sem, vmem_limit_bytes=VMEM_LIMIT_BYTES)


def _rms(x, g):
    return x * lax.rsqrt(jnp.mean(x * x, axis=-1, keepdims=True) + EPS) * g


def _sigmoid(x):
    return 1.0 / (1.0 + jnp.exp(-x))


def _dot(a, b):
    return jnp.dot(a, b, preferred_element_type=F32)


def _dot_nt(a, b):
    return lax.dot_general(a, b, (((1,), (1,)), ((), ())), preferred_element_type=F32)


def _dot_tn(a, b):
    return lax.dot_general(a, b, (((0,), (0,)), ((), ())), preferred_element_type=F32)


def _split3(v):
    hi = v.astype(BF16)
    r = v - hi.astype(F32)
    mid = r.astype(BF16)
    lo = (r - mid.astype(F32)).astype(BF16)
    return hi, mid, lo


def _dot_exact_lhs01(mat01, v):
    hi, mid, lo = _split3(v)
    return _dot(mat01, hi) + _dot(mat01, mid) + _dot(mat01, lo)


def _inproj_body(x_ref, g_ref, wq_ref, wg_ref, wzs_ref, wgf_ref, wgd_ref,
                 oq_ref, og_ref, oz_ref, osm_ref, ogf_ref, ogd_ref):
    hb = _rms(x_ref[...], g_ref[...]).astype(BF16)
    oq_ref[...] = _dot(hb, wq_ref[...]).astype(BF16)
    og_ref[...] = _dot(hb, wg_ref[...]).astype(BF16)
    zs = _dot(hb, wzs_ref[...])
    oz_ref[...] = zs[:, :GDN_WIDTH].astype(BF16)
    osm_ref[...] = zs[:, GDN_WIDTH:]
    ogf_ref[...] = _dot(hb, wgf_ref[...]).astype(BF16)
    ogd_ref[...] = _dot(hb, wgd_ref[...]).astype(BF16)


def _inproj(x2d, g_mix, wq, wg, wzs, wgf, wgd, tm):
    t, d = x2d.shape
    row = lambda i: (i, 0)
    const = lambda i: (0, 0)
    widths = (3 * FOX_WIDTH, 3 * GDN_WIDTH, GDN_WIDTH, LANES, d, d)
    dtypes = (BF16, BF16, BF16, F32, BF16, BF16)
    return pl.pallas_call(
        _inproj_body,
        grid=(t // tm,),
        in_specs=[pl.BlockSpec((tm, d), row), pl.BlockSpec((1, d), const)]
        + [pl.BlockSpec(w.shape, const) for w in (wq, wg, wzs, wgf, wgd)],
        out_specs=[pl.BlockSpec((tm, n), row) for n in widths],
        out_shape=[jax.ShapeDtypeStruct((t, n), dt) for n, dt in zip(widths, dtypes)],
        compiler_params=_cparams(("parallel",)),
        name="inproj",
    )(x2d, g_mix, wq, wg, wzs, wgf, wgd)


def _gates_tile(sm, par_ref, carry_ref):
    tp = sm.shape[0]
    lane = lax.broadcasted_iota(jnp.int32, sm.shape, 1)
    z = sm + par_ref[0:1, :]
    soft = jnp.log1p(jnp.exp(-jnp.abs(z)))
    softplus = jnp.maximum(z, 0.0) + soft
    logf = jnp.minimum(z, 0.0) - soft
    g = -jnp.exp(par_ref[1:2, :]) * softplus
    beta = _sigmoid(sm)
    is_f = lane < FOX_HEADS
    is_g = jnp.logical_and(lane >= FOX_HEADS, lane < FOX_HEADS + GDN_HEADS)
    is_b = jnp.logical_and(lane >= FOX_HEADS + GDN_HEADS, lane < FOX_HEADS + 2 * GDN_HEADS)
    val = jnp.where(is_f, logf, jnp.where(is_g, g, 0.0))
    r = lax.broadcasted_iota(jnp.int32, (tp, tp), 0)
    c = lax.broadcasted_iota(jnp.int32, (tp, tp), 1)
    lower = r >= c
    tri = jnp.where(lower, 1.0, 0.0).astype(BF16)
    same_chunk = (r // GDN_CHUNK) == (c // GDN_CHUNK)
    tri_chunk = jnp.where(jnp.logical_and(lower, same_chunk), 1.0, 0.0).astype(BF16)
    cum = _dot_exact_lhs01(tri, val) + carry_ref[...]
    gcs = _dot_exact_lhs01(tri_chunk, val)
    carry_ref[...] = cum[tp - 1:tp, :]
    return jnp.where(is_f, cum, jnp.where(is_g, gcs, jnp.where(is_b, beta, 0.0)))


LOG2E = 1.4426950408889634


FOX_VROWS = FOX_HEAD_DIM + 16


def _split3_f32(v):
    hi = v.astype(BF16).astype(F32)
    r = v - hi
    mid = r.astype(BF16).astype(F32)
    lo = (r - mid).astype(BF16).astype(F32)
    return hi, mid, lo


def _fox_body(q_ref, k_ref, v_ref, col_ref, o_ref, kx_ref, vt_ref, sa_ref, sb_ref, acc_ref, m_ref,
              *, tq, s_len):
    hp = pl.program_id(1)
    qi = pl.program_id(2)
    hd = FOX_HEAD_DIM

    @pl.when(qi == 0)
    def _():
        def prep(ci, carry):
            r0 = pl.multiple_of(ci * tq, tq)
            kx_ref[pl.ds(r0, tq), :LANES] = k_ref[0, pl.ds(r0, tq), :]
            col = col_ref[0, pl.ds(r0, tq), :]
            lane = lax.broadcasted_iota(jnp.int32, col.shape, 1)
            bias = jnp.zeros(col.shape, F32)
            for hh in range(2):
                ck = jnp.sum(jnp.where(lane == 2 * hp + hh, col, 0.0), axis=-1, keepdims=True) * LOG2E
                for k, piece in enumerate(_split3_f32(ck)):
                    bias = jnp.where(lane == 3 * hh + k, piece, bias)
            kx_ref[pl.ds(r0, tq), LANES:] = bias.astype(BF16)
            vt = v_ref[0, pl.ds(r0, tq), :].astype(F32).T
            ones_row = jnp.where(lax.broadcasted_iota(jnp.int32, (FOX_VROWS - hd, tq), 0) == 0, 1.0, 0.0)
            for hh in range(2):
                vt_ref[hh, :hd, pl.ds(r0, tq)] = vt[hh * hd:(hh + 1) * hd].astype(BF16)
                vt_ref[hh, hd:, pl.ds(r0, tq)] = ones_row.astype(BF16)
            return carry

        lax.fori_loop(0, s_len // tq, prep, 0)

    lane = lax.broadcasted_iota(jnp.int32, (tq, LANES), 1)
    q = (q_ref[0].astype(F32) * (hd ** -0.5 * LOG2E)).astype(BF16)
    zero = jnp.zeros_like(q)
    qx = []
    for hh in range(2):
        qh = jnp.where(jnp.logical_and(lane >= hh * hd, lane < (hh + 1) * hd), q, zero)
        sel = jnp.where(jnp.logical_and(lane >= 3 * hh, lane < 3 * hh + 3), -1.0, 0.0).astype(BF16)
        qx.append(jnp.concatenate([qh, sel], axis=1))
    m_ref[...] = jnp.full(m_ref.shape, NEG, F32)
    acc_ref[...] = jnp.zeros(acc_ref.shape, F32)

    def scores(j, dst_ref):
        start = pl.multiple_of(j * tq, tq)
        kx = kx_ref[pl.ds(start, tq), :]
        for hh in range(2):
            dst_ref[hh] = _dot_nt(kx, qx[hh])

    def consume(j, src_ref, masked):
        half = tq // 2
        pieces = ((0, tq, 0, tq),) if not masked else ((0, half, 0, tq), (half, tq, half, tq))
        for k0, k1, q0, q1 in pieces:
            start = pl.multiple_of(j * tq + k0, half)
            for hh in range(2):
                s = src_ref[hh, k0:k1, q0:q1]
                if masked:
                    key = lax.broadcasted_iota(jnp.int32, s.shape, 0) + k0
                    qry = lax.broadcasted_iota(jnp.int32, s.shape, 1) + q0
                    s = jnp.where(key <= qry, s, NEG)
                m_old = m_ref[hh:hh + 1, q0:q1]
                m_new = jnp.maximum(m_old, jnp.max(s, axis=0, keepdims=True))
                alpha = jnp.exp2(m_old - m_new)
                p = jnp.exp2(s - m_new).astype(BF16)
                acc_ref[hh, :, q0:q1] = alpha * acc_ref[hh, :, q0:q1] + _dot(vt_ref[hh, :, pl.ds(start, k1 - k0)], p)
                m_ref[hh:hh + 1, q0:q1] = m_new

    scores(0, sa_ref)

    def loop_body(j, carry):
        @pl.when(j % 2 == 0)
        def _():
            scores(j + 1, sb_ref)
            consume(j, sa_ref, False)

        @pl.when(j % 2 == 1)
        def _():
            scores(j + 1, sa_ref)
            consume(j, sb_ref, False)

        return carry

    lax.fori_loop(0, qi, loop_body, 0)

    @pl.when(qi % 2 == 0)
    def _():
        consume(qi, sa_ref, True)

    @pl.when(qi % 2 == 1)
    def _():
        consume(qi, sb_ref, True)

    out_t = jnp.concatenate([acc_ref[hh, :hd] / acc_ref[hh, hd:hd + 1] for hh in range(2)], axis=0)
    o_ref[0] = out_t.T.astype(BF16)


def _fox(zq3d, col3d, tq):
    b, s, _ = zq3d.shape
    npair = FOX_HEADS // 2
    kblk = FOX_WIDTH // LANES
    return pl.pallas_call(
        functools.partial(_fox_body, tq=tq, s_len=s),
        grid=(b, npair, s // tq),
        in_specs=[pl.BlockSpec((1, tq, LANES), lambda bi, hp, qi: (bi, qi, hp)),
                  pl.BlockSpec((1, s, LANES), lambda bi, hp, qi: (bi, 0, kblk + hp)),
                  pl.BlockSpec((1, s, LANES), lambda bi, hp, qi: (bi, 0, 2 * kblk + hp)),
                  pl.BlockSpec((1, s, LANES), lambda bi, hp, qi: (bi, 0, 0))],
        out_specs=pl.BlockSpec((1, tq, LANES), lambda bi, hp, qi: (bi, qi, hp)),
        out_shape=jax.ShapeDtypeStruct((b, s, FOX_WIDTH), BF16),
        scratch_shapes=[pltpu.VMEM((s, 2 * LANES), BF16), pltpu.VMEM((2, FOX_VROWS, s), BF16),
                        pltpu.VMEM((2, tq, tq), F32), pltpu.VMEM((2, tq, tq), F32),
                        pltpu.VMEM((2, FOX_VROWS, tq), F32), pltpu.VMEM((2, tq), F32)],
        compiler_params=_cparams(("parallel", "parallel", "arbitrary")),
        name="fox",
    )(zq3d, zq3d, zq3d, col3d)


HALO = 16


def _gdnprep_body(x_ref, halo_ref, cw_ref, sm_ref, par_ref, oq_ref, ok_ref, ov_ref, ocol_ref, ext_ref, carry_ref,
                  *, tp):
    @pl.when(pl.program_id(1) == 0)
    def _():
        carry_ref[...] = jnp.zeros_like(carry_ref)

    ocol_ref[0] = _gates_tile(sm_ref[0], par_ref, carry_ref)
    prev = halo_ref[0].astype(F32)
    ext_ref[0:HALO, :] = jnp.where(pl.program_id(1) > 0, prev, 0.0)
    ext_ref[HALO:, :] = x_ref[0].astype(F32)
    acc = cw_ref[GDN_CONV - 1:GDN_CONV, :] * ext_ref[HALO:HALO + tp, :]
    for j in range(GDN_CONV - 1):
        off = HALO - (GDN_CONV - 1) + j
        acc = acc + cw_ref[j:j + 1, :] * ext_ref[off:off + tp, :]
    y = acc * _sigmoid(acc)

    def l2(v):
        return v * lax.rsqrt(jnp.sum(v * v, axis=-1, keepdims=True) + EPS)

    for h in range(GDN_HEADS):
        lo, hi = h * GDN_HEAD_DIM, (h + 1) * GDN_HEAD_DIM
        oq_ref[0, :, lo:hi] = (l2(y[:, lo:hi]) * GDN_HEAD_DIM ** -0.5).astype(BF16)
        ok_ref[0, :, lo:hi] = l2(y[:, GDN_WIDTH + lo:GDN_WIDTH + hi]).astype(BF16)
    ov_ref[0] = y[:, 2 * GDN_WIDTH:].astype(BF16)


def _gdnprep(zg3d, conv_w, small3d, par, tp):
    b, s, c = zg3d.shape
    blk = lambda bi, i: (bi, i, 0)
    return pl.pallas_call(
        functools.partial(_gdnprep_body, tp=tp),
        grid=(b, s // tp),
        in_specs=[pl.BlockSpec((1, tp, c), blk),
                  pl.BlockSpec((1, HALO, c), lambda bi, i: (bi, jnp.maximum(i * (tp // HALO) - 1, 0), 0)),
                  pl.BlockSpec(conv_w.shape, lambda bi, i: (0, 0)),
                  pl.BlockSpec((1, tp, LANES), blk), pl.BlockSpec(par.shape, lambda bi, i: (0, 0))],
        out_specs=[pl.BlockSpec((1, tp, GDN_WIDTH), blk)] * 3 + [pl.BlockSpec((1, tp, LANES), blk)],
        out_shape=[jax.ShapeDtypeStruct((b, s, GDN_WIDTH), BF16)] * 3 + [jax.ShapeDtypeStruct((b, s, LANES), F32)],
        scratch_shapes=[pltpu.VMEM((tp + HALO, c), F32), pltpu.VMEM((1, LANES), F32)],
        compiler_params=_cparams(("parallel", "arbitrary")),
        name="gdnprep",
    )(zg3d, zg3d, conv_w, small3d, par)


def _stack_heads(x):
    return jnp.concatenate([x[:, h * GDN_HEAD_DIM:(h + 1) * GDN_HEAD_DIM] for h in range(GDN_HEADS)], axis=0)


def _gdn_body(q_ref, k_ref, v_ref, col_ref, grow_ref, o_ref, state_ref, *bufs, tg):
    C = GDN_CHUNK
    R = GDN_HEADS * C
    dh = GDN_HEAD_DIM
    n_chunks = tg // C
    step = pl.program_id(1)
    buf_sets = (bufs[:len(bufs) // 2], bufs[len(bufs) // 2:])

    @pl.when(step == 0)
    def _():
        for ref in (state_ref,) + tuple(bufs):
            ref[...] = jnp.zeros_like(ref)

    r = lax.broadcasted_iota(jnp.int32, (R, R), 0)
    c = lax.broadcasted_iota(jnp.int32, (R, R), 1)
    same_head = (r // C) == (c // C)
    lower = jnp.logical_and(same_head, r >= c)
    strict = jnp.logical_and(same_head, r > c)
    gc_lane0 = FOX_HEADS
    beta_lane0 = FOX_HEADS + GDN_HEADS

    def advance(ci, rd):
        u_ref, w_ref, intra_ref, qd_ref, kd_ref, gl_ref = rd
        r0 = ci * C
        u, w, intra = u_ref[ci], w_ref[ci], intra_ref[ci]
        q_dec, k_dec = qd_ref[ci], kd_ref[ci]
        v_new = []
        o_state = []
        for h in range(GDN_HEADS):
            sl = slice(h * C, (h + 1) * C)
            st = state_ref[h].astype(BF16)
            v_new.append(u[sl] - _dot(w[sl], st))
            o_state.append(_dot(q_dec[sl], st))
        v_new = jnp.concatenate(v_new, axis=0)
        v_new_b = v_new.astype(BF16)
        o_all = jnp.concatenate(o_state, axis=0) + _dot(intra, v_new_b)
        for h in range(GDN_HEADS):
            sl = slice(h * C, (h + 1) * C)
            state_ref[h] = state_ref[h] * gl_ref[ci, h:h + 1, :] + _dot_tn(k_dec[sl], v_new_b[sl])
        o_ref[0, r0:r0 + C, :] = jnp.concatenate(
            [o_all[h * C:(h + 1) * C] for h in range(GDN_HEADS)], axis=1).astype(BF16)

    chunks = range(n_chunks)

    def prepare_all(wr):
        u_ref, w_ref, intra_ref, qd_ref, kd_ref, gl_ref = wr
        qs, ks, vs, gc_col, beta_col, gl_row, lmat, intra = [], [], [], [], [], [], [], []
        for ci in chunks:
            r0 = ci * C
            qs.append(_stack_heads(q_ref[0, r0:r0 + C, :]).astype(F32))
            ks.append(_stack_heads(k_ref[0, r0:r0 + C, :]).astype(F32))
            vs.append(_stack_heads(v_ref[0, r0:r0 + C, :]).astype(F32))
            col = col_ref[0, r0:r0 + C, :]
            gc_col.append(
                jnp.concatenate([col[:, gc_lane0 + h:gc_lane0 + h + 1] for h in range(GDN_HEADS)], axis=0))
            beta_col.append(
                jnp.concatenate([col[:, beta_lane0 + h:beta_lane0 + h + 1] for h in range(GDN_HEADS)], axis=0))
            gl_row.append(col[C - 1:C, :])
            gc_row = grow_ref[0, ci, :, :]
            decay = jnp.exp(jnp.where(lower, gc_col[ci] - gc_row, NEG))
            ksb = ks[ci].astype(BF16)
            kk = _dot_nt(ksb, ksb)
            qk = _dot_nt(qs[ci].astype(BF16), ksb)
            lmat.append(jnp.where(strict, kk * decay * beta_col[ci], 0.0))
            intra.append((qk * decay).astype(BF16))
        def side_by_side(bd):
            return functools.reduce(lambda a, b: a + b, [bd[h * C:(h + 1) * C] for h in range(GDN_HEADS)])

        def block_diag(sbs):
            return jnp.where(same_head, jnp.concatenate([sbs] * GDN_HEADS, axis=0), 0.0)

        pw_bd = [l.astype(BF16) for l in lmat]
        pw = [side_by_side(l) for l in lmat]
        n_sbs = [-p for p in pw]
        for _ in range(5):
            pw = [_dot(p.astype(BF16), b) for p, b in zip(pw, pw_bd)]
            pw_bd = [block_diag(p).astype(BF16) for p in pw]
            n_sbs = [n + p + _dot(n.astype(BF16), b) for n, p, b in zip(n_sbs, pw, pw_bd)]
        n_mat = [block_diag(n) for n in n_sbs]
        for ci in chunks:
            e_gc = jnp.exp(gc_col[ci])
            rhs = jnp.concatenate([vs[ci] * beta_col[ci], ks[ci] * (beta_col[ci] * e_gc)], axis=1)
            sol = rhs + _dot(n_mat[ci].astype(BF16), rhs.astype(BF16))
            gl_col = jnp.concatenate(
                [jnp.broadcast_to(gl_row[ci][:, gc_lane0 + h:gc_lane0 + h + 1], (C, 1))
                 for h in range(GDN_HEADS)], axis=0)
            u_ref[ci] = sol[:, :dh]
            w_ref[ci] = sol[:, dh:].astype(BF16)
            intra_ref[ci] = intra[ci]
            qd_ref[ci] = (qs[ci] * e_gc).astype(BF16)
            kd_ref[ci] = (ks[ci] * jnp.exp(gl_col - gc_col[ci])).astype(BF16)
            for h in range(GDN_HEADS):
                gl_ref[ci, h:h + 1, :] = jnp.broadcast_to(
                    jnp.exp(gl_row[ci][:, gc_lane0 + h:gc_lane0 + h + 1]), (1, LANES))

    def run(rd, wr):
        for ci in chunks:
            advance(ci, rd)
        prepare_all(wr)

    @pl.when(step % 2 == 0)
    def _():
        run(buf_sets[0], buf_sets[1])

    @pl.when(step % 2 == 1)
    def _():
        run(buf_sets[1], buf_sets[0])


def _gdn(qn, kn, vv, col3d, gcrow, tg):
    b, s, _ = qn.shape
    n = s // tg
    nck = tg // GDN_CHUNK
    rows = GDN_HEADS * GDN_CHUNK
    dh = GDN_HEAD_DIM
    blk_in = lambda bi, i: (bi, jnp.minimum(i, n - 1), 0)
    blk_out = lambda bi, i: (bi, jnp.maximum(i - 1, 0), 0)
    return pl.pallas_call(
        functools.partial(_gdn_body, tg=tg),
        grid=(b, n + 1),
        in_specs=[pl.BlockSpec((1, tg, GDN_WIDTH), blk_in)] * 3
        + [pl.BlockSpec((1, tg, LANES), blk_in),
           pl.BlockSpec((1, nck, 1, rows), lambda bi, i: (bi, jnp.minimum(i, n - 1), 0, 0))],
        out_specs=pl.BlockSpec((1, tg, GDN_WIDTH), blk_out),
        out_shape=jax.ShapeDtypeStruct((b, s, GDN_WIDTH), BF16),
        scratch_shapes=[pltpu.VMEM((GDN_HEADS, dh, dh), F32)] + 2 * [
            pltpu.VMEM((nck, rows, dh), F32), pltpu.VMEM((nck, rows, dh), BF16),
            pltpu.VMEM((nck, rows, rows), BF16), pltpu.VMEM((nck, rows, dh), BF16),
            pltpu.VMEM((nck, rows, dh), BF16), pltpu.VMEM((nck, 8, LANES), F32)],
        compiler_params=_cparams(("parallel", "arbitrary")),
        name="gdn",
    )(qn, kn, vv, col3d, gcrow)


MOE_SLOT_TILE = 512


def _rows_to_tiles(ref, v):
    m, width = v.shape
    n = width // LANES
    for s in range(n):
        ref[pl.ds(s, m, stride=n), :] = v[:, s * LANES:(s + 1) * LANES]


def _tiles_to_rows(ref, m):
    n = ref.shape[0] // m
    return jnp.concatenate([ref[pl.ds(s, m, stride=n), :] for s in range(n)], axis=1)


RT_E1, RT_E2, RT_RANK1, RT_RANK2, RT_G1, RT_G2 = range(6)


def _combine_body(yf_ref, og_ref, dz_ref, gf_ref, gd_ref, x_ref, wof_ref, wod_ref, wout_ref,
                  gon_ref, gffn_ref, wrh_ref, wrl_ref, br_ref, x1_ref, t_ref, rt_ref, cnt_ref, carry_ref,
                  *, tc):
    @pl.when(pl.program_id(0) == 0)
    def _():
        carry_ref[...] = jnp.zeros_like(carry_ref)

    on = []
    for h in range(GDN_HEADS):
        sl = slice(h * GDN_HEAD_DIM, (h + 1) * GDN_HEAD_DIM)
        dz = dz_ref[:, sl].astype(F32)
        on.append(_rms(og_ref[:, sl].astype(F32), gon_ref[...]) * (dz * _sigmoid(dz)))
    on = jnp.concatenate(on, axis=1).astype(BF16)
    y_fox = _dot(yf_ref[...], wof_ref[...])
    y_delta = _dot(on, wod_ref[...])
    merged = _sigmoid(gf_ref[...].astype(F32)) * y_fox + _sigmoid(gd_ref[...].astype(F32)) * y_delta
    x1 = x_ref[...] + _dot(merged.astype(BF16), wout_ref[...])
    x1_ref[...] = x1
    t32 = _rms(x1, gffn_ref[...])
    th = t32.astype(BF16)
    tl = (t32 - th.astype(F32)).astype(BF16)
    _rows_to_tiles(t_ref, t32)
    logits = _dot(th, wrh_ref[...]) + _dot(tl, wrh_ref[...]) + _dot(th, wrl_ref[...]) + br_ref[...]
    lane = lax.broadcasted_iota(jnp.int32, logits.shape, 1)
    gl = jnp.where(lane < N_GROUPS, logits, NEG)
    gmax = jnp.max(gl, axis=-1, keepdims=True)
    g_sel = jnp.min(jnp.where(gl == gmax, lane, LANES), axis=-1, keepdims=True)
    p_sel = 1.0 / jnp.sum(jnp.exp(gl - gmax), axis=-1, keepdims=True)
    lo = EXPERT_LANE0 + EXPERTS_PER_GROUP * g_sel
    in_grp = jnp.logical_and(lane >= lo, lane < lo + EXPERTS_PER_GROUP)
    el = jnp.where(in_grp, logits, NEG)
    emax = jnp.max(el, axis=-1, keepdims=True)
    ee = jnp.where(in_grp, jnp.exp(el - emax), 0.0)
    pe = ee / jnp.sum(ee, axis=-1, keepdims=True)
    pe = jnp.where(in_grp, pe, -1.0)
    p1 = jnp.max(pe, axis=-1, keepdims=True)
    i1 = jnp.min(jnp.where(pe == p1, lane, LANES), axis=-1, keepdims=True)
    pe2 = jnp.where(lane == i1, -1.0, pe)
    p2 = jnp.max(pe2, axis=-1, keepdims=True)
    i2 = jnp.min(jnp.where(pe2 == p2, lane, LANES), axis=-1, keepdims=True)
    den = p1 + p2
    hit1 = lane == i1
    hit2 = lane == i2
    assign = jnp.where(jnp.logical_or(hit1, hit2), 1.0, 0.0)
    r = lax.broadcasted_iota(jnp.int32, (tc, tc), 0)
    c = lax.broadcasted_iota(jnp.int32, (tc, tc), 1)
    before = jnp.where(r > c, 1.0, 0.0).astype(BF16)
    prefix = _dot(before, assign.astype(BF16)) + carry_ref[...]
    rank1 = jnp.sum(jnp.where(hit1, prefix, 0.0), axis=-1, keepdims=True)
    rank2 = jnp.sum(jnp.where(hit2, prefix, 0.0), axis=-1, keepdims=True)
    carry_ref[...] = prefix[tc - 1:tc, :] + assign[tc - 1:tc, :]
    cnt_ref[...] = carry_ref[...]
    cols = ((i1 - EXPERT_LANE0).astype(F32), (i2 - EXPERT_LANE0).astype(F32), rank1, rank2,
            p_sel * (p1 / den), p_sel * (p2 / den))
    rt = jnp.zeros(logits.shape, F32)
    for k, v in enumerate(cols):
        rt = jnp.where(lane == k, v, rt)
    rt_ref[...] = rt


def _combine(yf, og, dz, gf, gd, x2d, wof, wod, wout, g_on, g_ffn, wrh, wrl, br, tc):
    t, d = x2d.shape
    row = lambda i: (i, 0)
    const = lambda i: (0, 0)
    acts = (yf, og, dz, gf, gd, x2d)
    consts = (wof, wod, wout, g_on, g_ffn, wrh, wrl, br)
    return pl.pallas_call(
        functools.partial(_combine_body, tc=tc),
        grid=(t // tc,),
        in_specs=[pl.BlockSpec((tc, a.shape[1]), row) for a in acts]
        + [pl.BlockSpec(c.shape, const) for c in consts],
        out_specs=[pl.BlockSpec((tc, d), row), pl.BlockSpec((tc * (d // LANES), LANES), row),
                   pl.BlockSpec((tc, LANES), row), pl.BlockSpec((1, LANES), const)],
        out_shape=[jax.ShapeDtypeStruct((t, d), F32), jax.ShapeDtypeStruct((t * (d // LANES), LANES), F32),
                   jax.ShapeDtypeStruct((t, LANES), F32), jax.ShapeDtypeStruct((1, LANES), F32)],
        scratch_shapes=[pltpu.VMEM((1, LANES), F32)],
        compiler_params=_cparams(("arbitrary",)),
        name="combine",
    )(*acts, *consts)


def _slots_body(rt_ref, cnt_ref, o_ref, *, tmx):
    lane1 = lax.broadcasted_iota(jnp.int32, (1, LANES), 1)
    is_expert = jnp.logical_and(lane1 >= EXPERT_LANE0, lane1 < EXPERT_LANE0 + N_EXPERTS)
    padded = jnp.where(is_expert, jnp.floor((cnt_ref[...] + (tmx - 1)) / tmx) * tmx, 0.0)
    r = lax.broadcasted_iota(jnp.int32, (LANES, LANES), 0)
    c = lax.broadcasted_iota(jnp.int32, (LANES, LANES), 1)
    before = jnp.where(r < c, 1.0, 0.0).astype(BF16)
    hi, mid, lo = _split3(jnp.broadcast_to(padded, (8, LANES)))
    offs = (_dot(hi, before) + _dot(mid, before) + _dot(lo, before))[0:1, :]
    rt = rt_ref[...]
    lane = lax.broadcasted_iota(jnp.int32, rt.shape, 1)
    expert_of_lane = (lane - EXPERT_LANE0).astype(F32)
    out = jnp.zeros(rt.shape, F32)
    for k, (ce, cr) in enumerate(((RT_E1, RT_RANK1), (RT_E2, RT_RANK2))):
        start = jnp.sum(jnp.where(expert_of_lane == rt[:, ce:ce + 1], offs, 0.0), axis=-1, keepdims=True)
        out = jnp.where(lane == k, start + rt[:, cr:cr + 1], out)
    o_ref[...] = out.T[:8, :].astype(jnp.int32)


def _slots(rt, cnt, tmx, ts):
    t = rt.shape[0]
    return pl.pallas_call(
        functools.partial(_slots_body, tmx=tmx),
        grid=(t // ts,),
        in_specs=[pl.BlockSpec((ts, LANES), lambda i: (i, 0)), pl.BlockSpec((1, LANES), lambda i: (0, 0))],
        out_specs=pl.BlockSpec((8, ts), lambda i: (0, i)),
        out_shape=jax.ShapeDtypeStruct((8, t), jnp.int32),
        compiler_params=_cparams(("parallel",)),
        name="slots",
    )(rt, cnt)


ROW_GROUP = 8


def _issue_row_copies(pos_ref, n_rows, nsub, make_copy):
    def issue(g, carry):
        r0 = g * ROW_GROUP
        slots = [[pos_ref[0, 0, k * n_rows + r0 + u] for k in range(2)] for u in range(ROW_GROUP)]
        for u in range(ROW_GROUP):
            tok = pl.ds(pl.multiple_of((r0 + u) * nsub, nsub), nsub)
            for k in range(2):
                slot = pl.ds(pl.multiple_of(slots[u][k] * nsub, nsub), nsub)
                make_copy(k, tok, slot).start(priority=k)
        return carry

    lax.fori_loop(0, n_rows // ROW_GROUP, issue, 0)


def _dispatch_body(zflag_ref, pos_ref, t_ref, xs_ref, zbuf, sem, zsem, *, td, tmx, n_tiles, nsub):
    @pl.when(pl.program_id(0) == 0)
    def _():
        zbuf[...] = jnp.zeros_like(zbuf)
        rows = tmx * nsub

        def zero_tile(k, carry):
            @pl.when(zflag_ref[k] != 0)
            def _():
                cp = pltpu.make_async_copy(zbuf, xs_ref.at[pl.ds(pl.multiple_of(k * rows, rows), rows), :], zsem)
                cp.start()
                cp.wait()

            return carry

        lax.fori_loop(0, n_tiles, zero_tile, 0)

    _issue_row_copies(pos_ref, td, nsub,
                      lambda k, tok, slot: pltpu.make_async_copy(t_ref.at[tok, :], xs_ref.at[slot, :], sem))
    for _ in range(2):
        pltpu.make_async_copy(t_ref, xs_ref.at[pl.ds(0, td * nsub), :], sem).wait()


def _dispatch(zflag, t_tiles, pos12, n_slots, td, tmx, nsub):
    t = t_tiles.shape[0] // nsub
    n_tiles = n_slots // tmx
    grid_spec = pltpu.PrefetchScalarGridSpec(
        num_scalar_prefetch=1,
        grid=(t // td,),
        in_specs=[pl.BlockSpec((1, 1, 2 * td), lambda i, zf: (i, 0, 0), memory_space=pltpu.SMEM),
                  pl.BlockSpec((td * nsub, LANES), lambda i, zf: (i, 0))],
        out_specs=pl.BlockSpec(memory_space=pl.ANY),
        scratch_shapes=[pltpu.VMEM((tmx * nsub, LANES), F32), pltpu.SemaphoreType.DMA(()),
                        pltpu.SemaphoreType.DMA(())],
    )
    return pl.pallas_call(
        functools.partial(_dispatch_body, td=td, tmx=tmx, n_tiles=n_tiles, nsub=nsub),
        grid_spec=grid_spec,
        out_shape=jax.ShapeDtypeStruct((n_slots * nsub, LANES), F32),
        compiler_params=_cparams(("arbitrary",)),
        name="dispatch",
    )(zflag, pos12, t_tiles)


def _experts_body(te_ref, nused_ref, xs_ref, wgu_ref, wd_ref, ys_ref, *, tmx):
    @pl.when(pl.program_id(0) < nused_ref[0])
    def _():
        hgu = _dot(_tiles_to_rows(xs_ref, tmx).astype(BF16), wgu_ref[0])
        a = hgu[:, :EXPERT_FF]
        hid = a * _sigmoid(a) * hgu[:, EXPERT_FF:]
        _rows_to_tiles(ys_ref, _dot(hid.astype(BF16), wd_ref[0]))

    @pl.when(pl.program_id(0) >= nused_ref[0])
    def _():
        ys_ref[...] = jnp.zeros_like(ys_ref)


def _experts(tile_expert, n_used, xs, wgu, wd, tmx):
    d = wgu.shape[1]
    nsub = d // LANES
    n_slots = xs.shape[0] // nsub
    grid_spec = pltpu.PrefetchScalarGridSpec(
        num_scalar_prefetch=2,
        grid=(n_slots // tmx,),
        in_specs=[pl.BlockSpec((tmx * nsub, LANES), lambda i, te, nu: (i, 0)),
                  pl.BlockSpec((1, d, 2 * EXPERT_FF), lambda i, te, nu: (te[i], 0, 0)),
                  pl.BlockSpec((1, EXPERT_FF, d), lambda i, te, nu: (te[i], 0, 0))],
        out_specs=pl.BlockSpec((tmx * nsub, LANES), lambda i, te, nu: (i, 0)),
    )
    return pl.pallas_call(
        functools.partial(_experts_body, tmx=tmx),
        grid_spec=grid_spec,
        out_shape=jax.ShapeDtypeStruct(xs.shape, F32),
        compiler_params=_cparams(("arbitrary",)),
        name="experts",
    )(tile_expert, n_used, xs, wgu, wd)


def _final_body(pos_ref, posn_ref, x_ref, rt_ref, p_ref, gple_ref, wpg_ref, wpp_ref, gfin_ref, ys_ref,
                o_ref, gbuf, sems, *, tf, nsub):
    i = pl.program_id(0)
    n = pl.num_programs(0)
    slot = i % 2

    def gather(pref, dst_slot):
        _issue_row_copies(pref, tf, nsub, lambda k, tok, src: pltpu.make_async_copy(
            ys_ref.at[src, :], gbuf.at[dst_slot, k, tok, :], sems.at[dst_slot]))

    @pl.when(i == 0)
    def _():
        gather(pos_ref, 0)

    @pl.when(i + 1 < n)
    def _():
        gather(posn_ref, 1 - slot)

    for k in range(2):
        pltpu.make_async_copy(ys_ref.at[pl.ds(0, tf * nsub), :], gbuf.at[slot, k], sems.at[slot]).wait()

    rt = rt_ref[...]
    g1 = rt[:, RT_G1:RT_G1 + 1]
    g2 = rt[:, RT_G2:RT_G2 + 1]
    x = x_ref[...] + g1 * _tiles_to_rows(gbuf.at[slot, 0], tf) + g2 * _tiles_to_rows(gbuf.at[slot, 1], tf)
    r = _rms(x, gple_ref[...]).astype(BF16)
    ple_gate = _sigmoid(_dot(r, wpg_ref[...]))
    proj = _dot(p_ref[...].astype(BF16), wpp_ref[...])
    o_ref[...] = _rms(x + ple_gate * proj, gfin_ref[...])


def _final(pos12, x1, rt, p2d, g_ple, wpg, wpp, g_final, ys, tf):
    t, d = x1.shape
    nsub = d // LANES
    n = t // tf
    row = lambda i: (i, 0)
    const = lambda i: (0, 0)
    smem_blk = lambda imap: pl.BlockSpec((1, 1, 2 * tf), imap, memory_space=pltpu.SMEM)
    return pl.pallas_call(
        functools.partial(_final_body, tf=tf, nsub=nsub),
        grid=(n,),
        in_specs=[smem_blk(lambda i: (i, 0, 0)), smem_blk(lambda i: (jnp.minimum(i + 1, n - 1), 0, 0)),
                  pl.BlockSpec((tf, d), row), pl.BlockSpec((tf, LANES), row),
                  pl.BlockSpec((tf, p2d.shape[1]), row),
                  pl.BlockSpec((1, d), const), pl.BlockSpec(wpg.shape, const),
                  pl.BlockSpec(wpp.shape, const), pl.BlockSpec((1, d), const),
                  pl.BlockSpec(memory_space=pl.ANY)],
        out_specs=pl.BlockSpec((tf, d), row),
        out_shape=jax.ShapeDtypeStruct((t, d), F32),
        scratch_shapes=[pltpu.VMEM((2, 2, tf * nsub, LANES), F32), pltpu.SemaphoreType.DMA((2,))],
        compiler_params=_cparams(("arbitrary",)),
        name="final",
    )(pos12, pos12, x1, rt, p2d, g_ple, wpg, wpp, g_final, ys)


def _pad_lanes(v, lane0):
    return jnp.zeros((1, LANES), F32).at[0, lane0:lane0 + v.shape[0]].set(v.astype(F32))


def _tile(n, pref):
    return pref if n % pref == 0 else n


def _layer(x, p_l, g_mix, w_in, b_forget, conv_w, a_log, dt_bias, g_onorm, w_o_fox, w_o_delta, w_out,
           g_ffn, w_group, b_group, w_router, b_router, w_gate, w_up, w_down, g_ple, w_ple_gate, w_ple_proj,
           g_post, apply_post):
    b, s, d = x.shape
    t = b * s
    x2d = x.reshape(t, d)

    o_ff = 3 * FOX_WIDTH
    o_qkv = o_ff + FOX_HEADS
    o_da = o_qkv + 3 * GDN_WIDTH
    o_db = o_da + GDN_HEADS
    o_dz = o_db + GDN_HEADS
    o_gf = o_dz + GDN_WIDTH
    o_gd = o_gf + d
    w_small = jnp.concatenate(
        [w_in[:, o_ff:o_qkv], w_in[:, o_da:o_db], w_in[:, o_db:o_dz],
         jnp.zeros((d, LANES - FOX_HEADS - 2 * GDN_HEADS), w_in.dtype)], axis=1)
    wq = w_in[:, :o_ff].astype(BF16)
    wg = w_in[:, o_qkv:o_da].astype(BF16)
    wzs = jnp.concatenate([w_in[:, o_dz:o_gf], w_small], axis=1).astype(BF16)
    wgf = w_in[:, o_gf:o_gd].astype(BF16)
    wgd = w_in[:, o_gd:].astype(BF16)

    zq, zg, dz, small, gf, gd = _inproj(x2d, g_mix.reshape(1, d), wq, wg, wzs, wgf, wgd, _tile(t, 512))

    par = jnp.concatenate(
        [_pad_lanes(b_forget, 0) + _pad_lanes(dt_bias, FOX_HEADS), _pad_lanes(a_log, FOX_HEADS),
         jnp.zeros((6, LANES), F32)], axis=0)
    qn, kn, vv, col = _gdnprep(zg.reshape(b, s, 3 * GDN_WIDTH), conv_w.astype(F32),
                               small.reshape(b, s, LANES), par, _tile(s, 512))

    nc = s // GDN_CHUNK
    gc = col[:, :, FOX_HEADS:FOX_HEADS + GDN_HEADS].reshape(b, nc, GDN_CHUNK, GDN_HEADS)
    gcrow = jnp.transpose(gc, (0, 1, 3, 2)).reshape(b, nc, 1, GDN_HEADS * GDN_CHUNK)

    y_fox = _fox(zq.reshape(b, s, 3 * FOX_WIDTH), col, _tile(s, 1024))
    o_gdn = _gdn(qn, kn, vv, col, gcrow, _tile(s, 512))

    w_r = jnp.concatenate([w_group, w_router,
                           jnp.zeros((d, LANES - N_GROUPS - N_EXPERTS), w_group.dtype)], axis=1).astype(F32)
    wrh = w_r.astype(BF16)
    wrl = (w_r - wrh.astype(F32)).astype(BF16)
    br = _pad_lanes(b_group, 0) + _pad_lanes(b_router, EXPERT_LANE0)
    x1, tpk, rt, cnt = _combine(
        y_fox.reshape(t, FOX_WIDTH), o_gdn.reshape(t, GDN_WIDTH), dz, gf, gd, x2d,
        w_o_fox.astype(BF16), w_o_delta.astype(BF16), w_out.astype(BF16),
        g_onorm.reshape(1, GDN_HEAD_DIM).astype(F32), g_ffn.reshape(1, d).astype(F32), wrh, wrl, br,
        _tile(t, 512))

    tmx = MOE_SLOT_TILE
    n_slots = 2 * t + N_EXPERTS * tmx
    counts = cnt[0, EXPERT_LANE0:EXPERT_LANE0 + N_EXPERTS].astype(jnp.int32)
    padded = (counts + tmx - 1) // tmx * tmx
    ends = jnp.cumsum(padded)
    pos = _slots(rt, cnt, tmx, _tile(t, 2048))
    pos1, pos2 = pos[0], pos[1]
    tile_start = jnp.arange(n_slots // tmx, dtype=jnp.int32) * tmx
    tile_expert = jnp.minimum(jnp.sum(tile_start[:, None] >= ends[None, :], axis=1), N_EXPERTS - 1).astype(jnp.int32)
    n_used = (ends[-1:] // tmx).astype(jnp.int32)
    last_of_expert = jnp.any(jnp.logical_and(tile_start[:, None] + tmx == ends[None, :], padded[None, :] > 0), axis=1)
    zflag = jnp.logical_or(last_of_expert, tile_start >= ends[-1]).astype(jnp.int32)

    def tiled_pos(tile):
        n = t // tile
        return jnp.concatenate([pos1.reshape(n, 1, tile), pos2.reshape(n, 1, tile)], axis=2)

    td = _tile(t, 2048)
    nsub = d // LANES
    xs = _dispatch(zflag, tpk, tiled_pos(td), n_slots, td, tmx, nsub)
    wgu = jnp.concatenate([w_gate, w_up], axis=2).astype(BF16)
    ys = _experts(tile_expert, n_used, xs, wgu, w_down.astype(BF16), tmx)

    tf = _tile(t, 256)
    out = _final(tiled_pos(tf), x1, rt, p_l.reshape(t, -1), g_ple.reshape(1, d).astype(F32),
                 w_ple_gate.astype(BF16), w_ple_proj.astype(BF16), g_post.reshape(1, d).astype(F32), ys, tf)
    return out.reshape(b, s, d)


def kernel(x, p, g_mix, w_in, b_forget, conv_w, a_log, dt_bias, g_onorm, w_o_fox, w_o_delta, w_out,
           g_ffn, w_group, b_group, w_router, b_router, w_gate, w_up, w_down, g_ple, w_ple_gate, w_ple_proj,
           g_final):
    depth = p.shape[0]
    assert depth == 1, "the final rmsnorm is fused into the last layer's epilogue; depth 1 only"
    i = 0
    return _layer(x, p[i], g_mix[i], w_in[i], b_forget[i], conv_w[i], a_log[i], dt_bias[i], g_onorm[i],
                  w_o_fox[i], w_o_delta[i], w_out[i], g_ffn[i], w_group[i], b_group[i], w_router[i],
                  b_router[i], w_gate[i], w_up[i], w_down[i], g_ple[i], w_ple_gate[i], w_ple_proj[i],
                  g_final, True)
```

```python
import functools

import jax
import jax.numpy as jnp
from jax import lax
from jax.experimental import pallas as pl
from jax.experimental.pallas import tpu as pltpu

F32 = jnp.float32
BF16 = jnp.bfloat16
EPS = 1e-6
NEG = -1e30

FOX_HEADS = 8
FOX_HEAD_DIM = 64
FOX_WIDTH = FOX_HEADS * FOX_HEAD_DIM
GDN_HEADS = 4
GDN_HEAD_DIM = 128
GDN_CONV = 4
GDN_CHUNK = 64
GDN_WIDTH = GDN_HEADS * GDN_HEAD_DIM
N_GROUPS = 4
EXPERTS_PER_GROUP = 8
N_EXPERTS = N_GROUPS * EXPERTS_PER_GROUP
EXPERT_FF = 256
LANES = 128
EXPERT_LANE0 = N_GROUPS

VMEM_LIMIT_BYTES = 56 * 1024 * 1024


def _cparams(sem):
    return pltpu.CompilerParams(dimension_semantics=sem, vmem_limit_bytes=VMEM_LIMIT_BYTES)


def _rms(x, g):
    return x * lax.rsqrt(jnp.mean(x * x, axis=-1, keepdims=True) + EPS) * g


def _sigmoid(x):
    return 1.0 / (1.0 + jnp.exp(-x))


def _dot(a, b):
    return jnp.dot(a, b, preferred_element_type=F32)


def _dot_nt(a, b):
    return lax.dot_general(a, b, (((1,), (1,)), ((), ())), preferred_element_type=F32)


def _dot_tn(a, b):
    return lax.dot_general(a, b, (((0,), (0,)), ((), ())), preferred_element_type=F32)


def _split3(v):
    hi = v.astype(BF16)
    r = v - hi.astype(F32)
    mid = r.astype(BF16)
    lo = (r - mid.astype(F32)).astype(BF16)
    return hi, mid, lo


def _dot_exact_lhs01(mat01, v):
    hi, mid, lo = _split3(v)
    return _dot(mat01, hi) + _dot(mat01, mid) + _dot(mat01, lo)


def _inproj_body(x_ref, g_ref, wq_ref, wg_ref, wzs_ref, wgf_ref, wgd_ref,
                 oq_ref, og_ref, oz_ref, osm_ref, ogf_ref, ogd_ref):
    hb = _rms(x_ref[...], g_ref[...]).astype(BF16)
    oq_ref[...] = _dot(hb, wq_ref[...]).astype(BF16)
    og_ref[...] = _dot(hb, wg_ref[...]).astype(BF16)
    zs = _dot(hb, wzs_ref[...])
    oz_ref[...] = zs[:, :GDN_WIDTH].astype(BF16)
    osm_ref[...] = zs[:, GDN_WIDTH:]
    ogf_ref[...] = _dot(hb, wgf_ref[...]).astype(BF16)
    ogd_ref[...] = _dot(hb, wgd_ref[...]).astype(BF16)


def _inproj(x2d, g_mix, wq, wg, wzs, wgf, wgd, tm):
    t, d = x2d.shape
    row = lambda i: (i, 0)
    const = lambda i: (0, 0)
    widths = (3 * FOX_WIDTH, 3 * GDN_WIDTH, GDN_WIDTH, LANES, d, d)
    dtypes = (BF16, BF16, BF16, F32, BF16, BF16)
    return pl.pallas_call(
        _inproj_body,
        grid=(t // tm,),
        in_specs=[pl.BlockSpec((tm, d), row), pl.BlockSpec((1, d), const)]
        + [pl.BlockSpec(w.shape, const) for w in (wq, wg, wzs, wgf, wgd)],
        out_specs=[pl.BlockSpec((tm, n), row) for n in widths],
        out_shape=[jax.ShapeDtypeStruct((t, n), dt) for n, dt in zip(widths, dtypes)],
        compiler_params=_cparams(("parallel",)),
        name="inproj",
    )(x2d, g_mix, wq, wg, wzs, wgf, wgd)


def _gates_tile(sm, par_ref, carry_ref):
    tp = sm.shape[0]
    lane = lax.broadcasted_iota(jnp.int32, sm.shape, 1)
    z = sm + par_ref[0:1, :]
    soft = jnp.log1p(jnp.exp(-jnp.abs(z)))
    softplus = jnp.maximum(z, 0.0) + soft
    logf = jnp.minimum(z, 0.0) - soft
    g = -jnp.exp(par_ref[1:2, :]) * softplus
    beta = _sigmoid(sm)
    is_f = lane < FOX_HEADS
    is_g = jnp.logical_and(lane >= FOX_HEADS, lane < FOX_HEADS + GDN_HEADS)
    is_b = jnp.logical_and(lane >= FOX_HEADS + GDN_HEADS, lane < FOX_HEADS + 2 * GDN_HEADS)
    val = jnp.where(is_f, logf, jnp.where(is_g, g, 0.0))
    r = lax.broadcasted_iota(jnp.int32, (tp, tp), 0)
    c = lax.broadcasted_iota(jnp.int32, (tp, tp), 1)
    lower = r >= c
    tri = jnp.where(lower, 1.0, 0.0).astype(BF16)
    same_chunk = (r // GDN_CHUNK) == (c // GDN_CHUNK)
    tri_chunk = jnp.where(jnp.logical_and(lower, same_chunk), 1.0, 0.0).astype(BF16)
    cum = _dot_exact_lhs01(tri, val) + carry_ref[...]
    gcs = _dot_exact_lhs01(tri_chunk, val)
    carry_ref[...] = cum[tp - 1:tp, :]
    return jnp.where(is_f, cum, jnp.where(is_g, gcs, jnp.where(is_b, beta, 0.0)))


LOG2E = 1.4426950408889634


FOX_VROWS = FOX_HEAD_DIM + 16


def _split3_f32(v):
    hi = v.astype(BF16).astype(F32)
    r = v - hi
    mid = r.astype(BF16).astype(F32)
    lo = (r - mid).astype(BF16).astype(F32)
    return hi, mid, lo


def _fox_body(q_ref, k_ref, v_ref, col_ref, o_ref, kx_ref, vt_ref, sa_ref, sb_ref, acc_ref, m_ref,
              *, tq, s_len):
    hp = pl.program_id(1)
    qi = pl.program_id(2)
    hd = FOX_HEAD_DIM

    @pl.when(qi == 0)
    def _():
        def prep(ci, carry):
            r0 = pl.multiple_of(ci * tq, tq)
            kx_ref[pl.ds(r0, tq), :LANES] = k_ref[0, pl.ds(r0, tq), :]
            col = col_ref[0, pl.ds(r0, tq), :]
            lane = lax.broadcasted_iota(jnp.int32, col.shape, 1)
            bias = jnp.zeros(col.shape, F32)
            for hh in range(2):
                ck = jnp.sum(jnp.where(lane == 2 * hp + hh, col, 0.0), axis=-1, keepdims=True) * LOG2E
                for k, piece in enumerate(_split3_f32(ck)):
                    bias = jnp.where(lane == 3 * hh + k, piece, bias)
            kx_ref[pl.ds(r0, tq), LANES:] = bias.astype(BF16)
            vt = v_ref[0, pl.ds(r0, tq), :].astype(F32).T
            ones_row = jnp.where(lax.broadcasted_iota(jnp.int32, (FOX_VROWS - hd, tq), 0) == 0, 1.0, 0.0)
            for hh in range(2):
                vt_ref[hh, :hd, pl.ds(r0, tq)] = vt[hh * hd:(hh + 1) * hd].astype(BF16)
                vt_ref[hh, hd:, pl.ds(r0, tq)] = ones_row.astype(BF16)
            return carry

        lax.fori_loop(0, s_len // tq, prep, 0)

    lane = lax.broadcasted_iota(jnp.int32, (tq, LANES), 1)
    q = (q_ref[0].astype(F32) * (hd ** -0.5 * LOG2E)).astype(BF16)
    zero = jnp.zeros_like(q)
    qx = []
    for hh in range(2):
        qh = jnp.where(jnp.logical_and(lane >= hh * hd, lane < (hh + 1) * hd), q, zero)
        sel = jnp.where(jnp.logical_and(lane >= 3 * hh, lane < 3 * hh + 3), -1.0, 0.0).astype(BF16)
        qx.append(jnp.concatenate([qh, sel], axis=1))
    m_ref[...] = jnp.full(m_ref.shape, NEG, F32)
    acc_ref[...] = jnp.zeros(acc_ref.shape, F32)

    def scores(j, dst_ref):
        start = pl.multiple_of(j * tq, tq)
        kx = kx_ref[pl.ds(start, tq), :]
        for hh in range(2):
            dst_ref[hh] = _dot_nt(kx, qx[hh])

    def consume(j, src_ref, masked):
        half = tq // 2
        pieces = ((0, tq, 0, tq),) if not masked else ((0, half, 0, tq), (half, tq, half, tq))
        for k0, k1, q0, q1 in pieces:
            start = pl.multiple_of(j * tq + k0, half)
            for hh in range(2):
                s = src_ref[hh, k0:k1, q0:q1]
                if masked:
                    key = lax.broadcasted_iota(jnp.int32, s.shape, 0) + k0
                    qry = lax.broadcasted_iota(jnp.int32, s.shape, 1) + q0
                    s = jnp.where(key <= qry, s, NEG)
                m_old = m_ref[hh:hh + 1, q0:q1]
                m_new = jnp.maximum(m_old, jnp.max(s, axis=0, keepdims=True))
                alpha = jnp.exp2(m_old - m_new)
                p = jnp.exp2(s - m_new).astype(BF16)
                acc_ref[hh, :, q0:q1] = alpha * acc_ref[hh, :, q0:q1] + _dot(vt_ref[hh, :, pl.ds(start, k1 - k0)], p)
                m_ref[hh:hh + 1, q0:q1] = m_new

    scores(0, sa_ref)

    def loop_body(j, carry):
        @pl.when(j % 2 == 0)
        def _():
            scores(j + 1, sb_ref)
            consume(j, sa_ref, False)

        @pl.when(j % 2 == 1)
        def _():
            scores(j + 1, sa_ref)
            consume(j, sb_ref, False)

        return carry

    lax.fori_loop(0, qi, loop_body, 0)

    @pl.when(qi % 2 == 0)
    def _():
        consume(qi, sa_ref, True)

    @pl.when(qi % 2 == 1)
    def _():
        consume(qi, sb_ref, True)

    out_t = jnp.concatenate([acc_ref[hh, :hd] / acc_ref[hh, hd:hd + 1] for hh in range(2)], axis=0)
    o_ref[0] = out_t.T.astype(BF16)


def _fox(zq3d, col3d, tq):
    b, s, _ = zq3d.shape
    npair = FOX_HEADS // 2
    kblk = FOX_WIDTH // LANES
    return pl.pallas_call(
        functools.partial(_fox_body, tq=tq, s_len=s),
        grid=(b, npair, s // tq),
        in_specs=[pl.BlockSpec((1, tq, LANES), lambda bi, hp, qi: (bi, qi, hp)),
                  pl.BlockSpec((1, s, LANES), lambda bi, hp, qi: (bi, 0, kblk + hp)),
                  pl.BlockSpec((1, s, LANES), lambda bi, hp, qi: (bi, 0, 2 * kblk + hp)),
                  pl.BlockSpec((1, s, LANES), lambda bi, hp, qi: (bi, 0, 0))],
        out_specs=pl.BlockSpec((1, tq, LANES), lambda bi, hp, qi: (bi, qi, hp)),
        out_shape=jax.ShapeDtypeStruct((b, s, FOX_WIDTH), BF16),
        scratch_shapes=[pltpu.VMEM((s, 2 * LANES), BF16), pltpu.VMEM((2, FOX_VROWS, s), BF16),
                        pltpu.VMEM((2, tq, tq), F32), pltpu.VMEM((2, tq, tq), F32),
                        pltpu.VMEM((2, FOX_VROWS, tq), F32), pltpu.VMEM((2, tq), F32)],
        compiler_params=_cparams(("parallel", "parallel", "arbitrary")),
        name="fox",
    )(zq3d, zq3d, zq3d, col3d)


HALO = 16


def _gdnprep_body(x_ref, halo_ref, cw_ref, sm_ref, par_ref, oq_ref, ok_ref, ov_ref, ocol_ref, ext_ref, carry_ref,
                  *, tp):
    @pl.when(pl.program_id(1) == 0)
    def _():
        carry_ref[...] = jnp.zeros_like(carry_ref)

    ocol_ref[0] = _gates_tile(sm_ref[0], par_ref, carry_ref)
    prev = halo_ref[0].astype(F32)
    ext_ref[0:HALO, :] = jnp.where(pl.program_id(1) > 0, prev, 0.0)
    ext_ref[HALO:, :] = x_ref[0].astype(F32)
    acc = cw_ref[GDN_CONV - 1:GDN_CONV, :] * ext_ref[HALO:HALO + tp, :]
    for j in range(GDN_CONV - 1):
        off = HALO - (GDN_CONV - 1) + j
        acc = acc + cw_ref[j:j + 1, :] * ext_ref[off:off + tp, :]
    y = acc * _sigmoid(acc)

    def l2(v):
        return v * lax.rsqrt(jnp.sum(v * v, axis=-1, keepdims=True) + EPS)

    for h in range(GDN_HEADS):
        lo, hi = h * GDN_HEAD_DIM, (h + 1) * GDN_HEAD_DIM
        oq_ref[0, :, lo:hi] = (l2(y[:, lo:hi]) * GDN_HEAD_DIM ** -0.5).astype(BF16)
        ok_ref[0, :, lo:hi] = l2(y[:, GDN_WIDTH + lo:GDN_WIDTH + hi]).astype(BF16)
    ov_ref[0] = y[:, 2 * GDN_WIDTH:].astype(BF16)


def _gdnprep(zg3d, conv_w, small3d, par, tp):
    b, s, c = zg3d.shape
    blk = lambda bi, i: (bi, i, 0)
    return pl.pallas_call(
        functools.partial(_gdnprep_body, tp=tp),
        grid=(b, s // tp),
        in_specs=[pl.BlockSpec((1, tp, c), blk),
                  pl.BlockSpec((1, HALO, c), lambda bi, i: (bi, jnp.maximum(i * (tp // HALO) - 1, 0), 0)),
                  pl.BlockSpec(conv_w.shape, lambda bi, i: (0, 0)),
                  pl.BlockSpec((1, tp, LANES), blk), pl.BlockSpec(par.shape, lambda bi, i: (0, 0))],
        out_specs=[pl.BlockSpec((1, tp, GDN_WIDTH), blk)] * 3 + [pl.BlockSpec((1, tp, LANES), blk)],
        out_shape=[jax.ShapeDtypeStruct((b, s, GDN_WIDTH), BF16)] * 3 + [jax.ShapeDtypeStruct((b, s, LANES), F32)],
        scratch_shapes=[pltpu.VMEM((tp + HALO, c), F32), pltpu.VMEM((1, LANES), F32)],
        compiler_params=_cparams(("parallel", "arbitrary")),
        name="gdnprep",
    )(zg3d, zg3d, conv_w, small3d, par)


def _stack_heads(x):
    return jnp.concatenate([x[:, h * GDN_HEAD_DIM:(h + 1) * GDN_HEAD_DIM] for h in range(GDN_HEADS)], axis=0)


def _gdn_body(q_ref, k_ref, v_ref, col_ref, grow_ref, o_ref, state_ref, *bufs, tg):
    C = GDN_CHUNK
    R = GDN_HEADS * C
    dh = GDN_HEAD_DIM
    n_chunks = tg // C
    step = pl.program_id(1)
    buf_sets = (bufs[:len(bufs) // 2], bufs[len(bufs) // 2:])

    @pl.when(step == 0)
    def _():
        for ref in (state_ref,) + tuple(bufs):
            ref[...] = jnp.zeros_like(ref)

    r = lax.broadcasted_iota(jnp.int32, (R, R), 0)
    c = lax.broadcasted_iota(jnp.int32, (R, R), 1)
    same_head = (r // C) == (c // C)
    lower = jnp.logical_and(same_head, r >= c)
    strict = jnp.logical_and(same_head, r > c)
    gc_lane0 = FOX_HEADS
    beta_lane0 = FOX_HEADS + GDN_HEADS

    def advance(ci, rd):
        u_ref, w_ref, intra_ref, qd_ref, kd_ref, gl_ref = rd
        r0 = ci * C
        u, w, intra = u_ref[ci], w_ref[ci], intra_ref[ci]
        q_dec, k_dec = qd_ref[ci], kd_ref[ci]
        v_new = []
        o_state = []
        for h in range(GDN_HEADS):
            sl = slice(h * C, (h + 1) * C)
            st = state_ref[h].astype(BF16)
            v_new.append(u[sl] - _dot(w[sl], st))
            o_state.append(_dot(q_dec[sl], st))
        v_new = jnp.concatenate(v_new, axis=0)
        v_new_b = v_new.astype(BF16)
        o_all = jnp.concatenate(o_state, axis=0) + _dot(intra, v_new_b)
        for h in range(GDN_HEADS):
            sl = slice(h * C, (h + 1) * C)
            state_ref[h] = state_ref[h] * gl_ref[ci, h:h + 1, :] + _dot_tn(k_dec[sl], v_new_b[sl])
        o_ref[0, r0:r0 + C, :] = jnp.concatenate(
            [o_all[h * C:(h + 1) * C] for h in range(GDN_HEADS)], axis=1).astype(BF16)

    chunks = range(n_chunks)

    def prepare_all(wr):
        u_ref, w_ref, intra_ref, qd_ref, kd_ref, gl_ref = wr
        qs, ks, vs, gc_col, beta_col, gl_row, lmat, intra = [], [], [], [], [], [], [], []
        for ci in chunks:
            r0 = ci * C
            qs.append(_stack_heads(q_ref[0, r0:r0 + C, :]).astype(F32))
            ks.append(_stack_heads(k_ref[0, r0:r0 + C, :]).astype(F32))
            vs.append(_stack_heads(v_ref[0, r0:r0 + C, :]).astype(F32))
            col = col_ref[0, r0:r0 + C, :]
            gc_col.append(
                jnp.concatenate([col[:, gc_lane0 + h:gc_lane0 + h + 1] for h in range(GDN_HEADS)], axis=0))
            beta_col.append(
                jnp.concatenate([col[:, beta_lane0 + h:beta_lane0 + h + 1] for h in range(GDN_HEADS)], axis=0))
            gl_row.append(col[C - 1:C, :])
            gc_row = grow_ref[0, ci, :, :]
            decay = jnp.exp(jnp.where(lower, gc_col[ci] - gc_row, NEG))
            ksb = ks[ci].astype(BF16)
            kk = _dot_nt(ksb, ksb)
            qk = _dot_nt(qs[ci].astype(BF16), ksb)
            lmat.append(jnp.where(strict, kk * decay * beta_col[ci], 0.0))
            intra.append((qk * decay).astype(BF16))
        def side_by_side(bd):
            return functools.reduce(lambda a, b: a + b, [bd[h * C:(h + 1) * C] for h in range(GDN_HEADS)])

        def block_diag(sbs):
            return jnp.where(same_head, jnp.concatenate([sbs] * GDN_HEADS, axis=0), 0.0)

        pw_bd = [l.astype(BF16) for l in lmat]
        pw = [side_by_side(l) for l in lmat]
        n_sbs = [-p for p in pw]
        for _ in range(5):
            pw = [_dot(p.astype(BF16), b) for p, b in zip(pw, pw_bd)]
            pw_bd = [block_diag(p).astype(BF16) for p in pw]
            n_sbs = [n + p + _dot(n.astype(BF16), b) for n, p, b in zip(n_sbs, pw, pw_bd)]
        n_mat = [block_diag(n) for n in n_sbs]
        for ci in chunks:
            e_gc = jnp.exp(gc_col[ci])
            rhs = jnp.concatenate([vs[ci] * beta_col[ci], ks[ci] * (beta_col[ci] * e_gc)], axis=1)
            sol = rhs + _dot(n_mat[ci].astype(BF16), rhs.astype(BF16))
            gl_col = jnp.concatenate(
                [jnp.broadcast_to(gl_row[ci][:, gc_lane0 + h:gc_lane0 + h + 1], (C, 1))
                 for h in range(GDN_HEADS)], axis=0)
            u_ref[ci] = sol[:, :dh]
            w_ref[ci] = sol[:, dh:].astype(BF16)
            intra_ref[ci] = intra[ci]
            qd_ref[ci] = (qs[ci] * e_gc).astype(BF16)
            kd_ref[ci] = (ks[ci] * jnp.exp(gl_col - gc_col[ci])).astype(BF16)
            for h in range(GDN_HEADS):
                gl_ref[ci, h:h + 1, :] = jnp.broadcast_to(
                    jnp.exp(gl_row[ci][:, gc_lane0 + h:gc_lane0 + h + 1]), (1, LANES))

    def run(rd, wr):
        for ci in chunks:
            advance(ci, rd)
        prepare_all(wr)

    @pl.when(step % 2 == 0)
    def _():
        run(buf_sets[0], buf_sets[1])

    @pl.when(step % 2 == 1)
    def _():
        run(buf_sets[1], buf_sets[0])


def _gdn(qn, kn, vv, col3d, gcrow, tg):
    b, s, _ = qn.shape
    n = s // tg
    nck = tg // GDN_CHUNK
    rows = GDN_HEADS * GDN_CHUNK
    dh = GDN_HEAD_DIM
    blk_in = lambda bi, i: (bi, jnp.minimum(i, n - 1), 0)
    blk_out = lambda bi, i: (bi, jnp.maximum(i - 1, 0), 0)
    return pl.pallas_call(
        functools.partial(_gdn_body, tg=tg),
        grid=(b, n + 1),
        in_specs=[pl.BlockSpec((1, tg, GDN_WIDTH), blk_in)] * 3
        + [pl.BlockSpec((1, tg, LANES), blk_in),
           pl.BlockSpec((1, nck, 1, rows), lambda bi, i: (bi, jnp.minimum(i, n - 1), 0, 0))],
        out_specs=pl.BlockSpec((1, tg, GDN_WIDTH), blk_out),
        out_shape=jax.ShapeDtypeStruct((b, s, GDN_WIDTH), BF16),
        scratch_shapes=[pltpu.VMEM((GDN_HEADS, dh, dh), F32)] + 2 * [
            pltpu.VMEM((nck, rows, dh), F32), pltpu.VMEM((nck, rows, dh), BF16),
            pltpu.VMEM((nck, rows, rows), BF16), pltpu.VMEM((nck, rows, dh), BF16),
            pltpu.VMEM((nck, rows, dh), BF16), pltpu.VMEM((nck, 8, LANES), F32)],
        compiler_params=_cparams(("parallel", "arbitrary")),
        name="gdn",
    )(qn, kn, vv, col3d, gcrow)


MOE_SLOT_TILE = 512


def _rows_to_tiles(ref, v):
    m, width = v.shape
    n = width // LANES
    for s in range(n):
        ref[pl.ds(s, m, stride=n), :] = v[:, s * LANES:(s + 1) * LANES]


def _tiles_to_rows(ref, m):
    n = ref.shape[0] // m
    return jnp.concatenate([ref[pl.ds(s, m, stride=n), :] for s in range(n)], axis=1)


RT_E1, RT_E2, RT_RANK1, RT_RANK2, RT_G1, RT_G2 = range(6)


def _combine_body(yf_ref, og_ref, dz_ref, gf_ref, gd_ref, x_ref, wof_ref, wod_ref, wout_ref,
                  gon_ref, gffn_ref, wrh_ref, wrl_ref, br_ref, x1_ref, t_ref, rt_ref, cnt_ref, carry_ref,
                  *, tc):
    @pl.when(pl.program_id(0) == 0)
    def _():
        carry_ref[...] = jnp.zeros_like(carry_ref)

    on = []
    for h in range(GDN_HEADS):
        sl = slice(h * GDN_HEAD_DIM, (h + 1) * GDN_HEAD_DIM)
        dz = dz_ref[:, sl].astype(F32)
        on.append(_rms(og_ref[:, sl].astype(F32), gon_ref[...]) * (dz * _sigmoid(dz)))
    on = jnp.concatenate(on, axis=1).astype(BF16)
    y_fox = _dot(yf_ref[...], wof_ref[...])
    y_delta = _dot(on, wod_ref[...])
    merged = _sigmoid(gf_ref[...].astype(F32)) * y_fox + _sigmoid(gd_ref[...].astype(F32)) * y_delta
    x1 = x_ref[...] + _dot(merged.astype(BF16), wout_ref[...])
    x1_ref[...] = x1
    t32 = _rms(x1, gffn_ref[...])
    th = t32.astype(BF16)
    tl = (t32 - th.astype(F32)).astype(BF16)
    _rows_to_tiles(t_ref, t32)
    logits = _dot(th, wrh_ref[...]) + _dot(tl, wrh_ref[...]) + _dot(th, wrl_ref[...]) + br_ref[...]
    lane = lax.broadcasted_iota(jnp.int32, logits.shape, 1)
    gl = jnp.where(lane < N_GROUPS, logits, NEG)
    gmax = jnp.max(gl, axis=-1, keepdims=True)
    g_sel = jnp.min(jnp.where(gl == gmax, lane, LANES), axis=-1, keepdims=True)
    p_sel = 1.0 / jnp.sum(jnp.exp(gl - gmax), axis=-1, keepdims=True)
    lo = EXPERT_LANE0 + EXPERTS_PER_GROUP * g_sel
    in_grp = jnp.logical_and(lane >= lo, lane < lo + EXPERTS_PER_GROUP)
    el = jnp.where(in_grp, logits, NEG)
    emax = jnp.max(el, axis=-1, keepdims=True)
    ee = jnp.where(in_grp, jnp.exp(el - emax), 0.0)
    pe = ee / jnp.sum(ee, axis=-1, keepdims=True)
    pe = jnp.where(in_grp, pe, -1.0)
    p1 = jnp.max(pe, axis=-1, keepdims=True)
    i1 = jnp.min(jnp.where(pe == p1, lane, LANES), axis=-1, keepdims=True)
    pe2 = jnp.where(lane == i1, -1.0, pe)
    p2 = jnp.max(pe2, axis=-1, keepdims=True)
    i2 = jnp.min(jnp.where(pe2 == p2, lane, LANES), axis=-1, keepdims=True)
    den = p1 + p2
    hit1 = lane == i1
    hit2 = lane == i2
    assign = jnp.where(jnp.logical_or(hit1, hit2), 1.0, 0.0)
    r = lax.broadcasted_iota(jnp.int32, (tc, tc), 0)
    c = lax.broadcasted_iota(jnp.int32, (tc, tc), 1)
    before = jnp.where(r > c, 1.0, 0.0).astype(BF16)
    prefix = _dot(before, assign.astype(BF16)) + carry_ref[...]
    rank1 = jnp.sum(jnp.where(hit1, prefix, 0.0), axis=-1, keepdims=True)
    rank2 = jnp.sum(jnp.where(hit2, prefix, 0.0), axis=-1, keepdims=True)
    carry_ref[...] = prefix[tc - 1:tc, :] + assign[tc - 1:tc, :]
    cnt_ref[...] = carry_ref[...]
    cols = ((i1 - EXPERT_LANE0).astype(F32), (i2 - EXPERT_LANE0).astype(F32), rank1, rank2,
            p_sel * (p1 / den), p_sel * (p2 / den))
    rt = jnp.zeros(logits.shape, F32)
    for k, v in enumerate(cols):
        rt = jnp.where(lane == k, v, rt)
    rt_ref[...] = rt


def _combine(yf, og, dz, gf, gd, x2d, wof, wod, wout, g_on, g_ffn, wrh, wrl, br, tc):
    t, d = x2d.shape
    row = lambda i: (i, 0)
    const = lambda i: (0, 0)
    acts = (yf, og, dz, gf, gd, x2d)
    consts = (wof, wod, wout, g_on, g_ffn, wrh, wrl, br)
    return pl.pallas_call(
        functools.partial(_combine_body, tc=tc),
        grid=(t // tc,),
        in_specs=[pl.BlockSpec((tc, a.shape[1]), row) for a in acts]
        + [pl.BlockSpec(c.shape, const) for c in consts],
        out_specs=[pl.BlockSpec((tc, d), row), pl.BlockSpec((tc * (d // LANES), LANES), row),
                   pl.BlockSpec((tc, LANES), row), pl.BlockSpec((1, LANES), const)],
        out_shape=[jax.ShapeDtypeStruct((t, d), F32), jax.ShapeDtypeStruct((t * (d // LANES), LANES), F32),
                   jax.ShapeDtypeStruct((t, LANES), F32), jax.ShapeDtypeStruct((1, LANES), F32)],
        scratch_shapes=[pltpu.VMEM((1, LANES), F32)],
        compiler_params=_cparams(("arbitrary",)),
        name="combine",
    )(*acts, *consts)


def _slots_body(rt_ref, cnt_ref, o_ref, *, tmx):
    lane1 = lax.broadcasted_iota(jnp.int32, (1, LANES), 1)
    is_expert = jnp.logical_and(lane1 >= EXPERT_LANE0, lane1 < EXPERT_LANE0 + N_EXPERTS)
    padded = jnp.where(is_expert, jnp.floor((cnt_ref[...] + (tmx - 1)) / tmx) * tmx, 0.0)
    r = lax.broadcasted_iota(jnp.int32, (LANES, LANES), 0)
    c = lax.broadcasted_iota(jnp.int32, (LANES, LANES), 1)
    before = jnp.where(r < c, 1.0, 0.0).astype(BF16)
    hi, mid, lo = _split3(jnp.broadcast_to(padded, (8, LANES)))
    offs = (_dot(hi, before) + _dot(mid, before) + _dot(lo, before))[0:1, :]
    rt = rt_ref[...]
    lane = lax.broadcasted_iota(jnp.int32, rt.shape, 1)
    expert_of_lane = (lane - EXPERT_LANE0).astype(F32)
    out = jnp.zeros(rt.shape, F32)
    for k, (ce, cr) in enumerate(((RT_E1, RT_RANK1), (RT_E2, RT_RANK2))):
        start = jnp.sum(jnp.where(expert_of_lane == rt[:, ce:ce + 1], offs, 0.0), axis=-1, keepdims=True)
        out = jnp.where(lane == k, start + rt[:, cr:cr + 1], out)
    o_ref[...] = out.T[:8, :].astype(jnp.int32)


def _slots(rt, cnt, tmx, ts):
    t = rt.shape[0]
    return pl.pallas_call(
        functools.partial(_slots_body, tmx=tmx),
        grid=(t // ts,),
        in_specs=[pl.BlockSpec((ts, LANES), lambda i: (i, 0)), pl.BlockSpec((1, LANES), lambda i: (0, 0))],
        out_specs=pl.BlockSpec((8, ts), lambda i: (0, i)),
        out_shape=jax.ShapeDtypeStruct((8, t), jnp.int32),
        compiler_params=_cparams(("parallel",)),
        name="slots",
    )(rt, cnt)


ROW_GROUP = 8


def _issue_row_copies(pos_ref, n_rows, nsub, make_copy):
    def issue(g, carry):
        r0 = g * ROW_GROUP
        slots = [[pos_ref[0, 0, k * n_rows + r0 + u] for k in range(2)] for u in range(ROW_GROUP)]
        for u in range(ROW_GROUP):
            tok = pl.ds(pl.multiple_of((r0 + u) * nsub, nsub), nsub)
            for k in range(2):
                slot = pl.ds(pl.multiple_of(slots[u][k] * nsub, nsub), nsub)
                make_copy(k, tok, slot).start(priority=k)
        return carry

    lax.fori_loop(0, n_rows // ROW_GROUP, issue, 0)


def _dispatch_body(zflag_ref, pos_ref, t_ref, xs_ref, zbuf, sem, zsem, *, td, tmx, n_tiles, nsub):
    @pl.when(pl.program_id(0) == 0)
    def _():
        zbuf[...] = jnp.zeros_like(zbuf)
        rows = tmx * nsub

        def zero_tile(k, carry):
            @pl.when(zflag_ref[k] != 0)
            def _():
                cp = pltpu.make_async_copy(zbuf, xs_ref.at[pl.ds(pl.multiple_of(k * rows, rows), rows), :], zsem)
                cp.start()
                cp.wait()

            return carry

        lax.fori_loop(0, n_tiles, zero_tile, 0)

    _issue_row_copies(pos_ref, td, nsub,
                      lambda k, tok, slot: pltpu.make_async_copy(t_ref.at[tok, :], xs_ref.at[slot, :], sem))
    for _ in range(2):
        pltpu.make_async_copy(t_ref, xs_ref.at[pl.ds(0, td * nsub), :], sem).wait()


def _dispatch(zflag, t_tiles, pos12, n_slots, td, tmx, nsub):
    t = t_tiles.shape[0] // nsub
    n_tiles = n_slots // tmx
    grid_spec = pltpu.PrefetchScalarGridSpec(
        num_scalar_prefetch=1,
        grid=(t // td,),
        in_specs=[pl.BlockSpec((1, 1, 2 * td), lambda i, zf: (i, 0, 0), memory_space=pltpu.SMEM),
                  pl.BlockSpec((td * nsub, LANES), lambda i, zf: (i, 0))],
        out_specs=pl.BlockSpec(memory_space=pl.ANY),
        scratch_shapes=[pltpu.VMEM((tmx * nsub, LANES), F32), pltpu.SemaphoreType.DMA(()),
                        pltpu.SemaphoreType.DMA(())],
    )
    return pl.pallas_call(
        functools.partial(_dispatch_body, td=td, tmx=tmx, n_tiles=n_tiles, nsub=nsub),
        grid_spec=grid_spec,
        out_shape=jax.ShapeDtypeStruct((n_slots * nsub, LANES), F32),
        compiler_params=_cparams(("arbitrary",)),
        name="dispatch",
    )(zflag, pos12, t_tiles)


def _experts_body(te_ref, nused_ref, xs_ref, wgu_ref, wd_ref, ys_ref, *, tmx):
    @pl.when(pl.program_id(0) < nused_ref[0])
    def _():
        hgu = _dot(_tiles_to_rows(xs_ref, tmx).astype(BF16), wgu_ref[0])
        a = hgu[:, :EXPERT_FF]
        hid = a * _sigmoid(a) * hgu[:, EXPERT_FF:]
        _rows_to_tiles(ys_ref, _dot(hid.astype(BF16), wd_ref[0]))

    @pl.when(pl.program_id(0) >= nused_ref[0])
    def _():
        ys_ref[...] = jnp.zeros_like(ys_ref)


def _experts(tile_expert, n_used, xs, wgu, wd, tmx):
    d = wgu.shape[1]
    nsub = d // LANES
    n_slots = xs.shape[0] // nsub
    grid_spec = pltpu.PrefetchScalarGridSpec(
        num_scalar_prefetch=2,
        grid=(n_slots // tmx,),
        in_specs=[pl.BlockSpec((tmx * nsub, LANES), lambda i, te, nu: (i, 0)),
                  pl.BlockSpec((1, d, 2 * EXPERT_FF), lambda i, te, nu: (te[i], 0, 0)),
                  pl.BlockSpec((1, EXPERT_FF, d), lambda i, te, nu: (te[i], 0, 0))],
        out_specs=pl.BlockSpec((tmx * nsub, LANES), lambda i, te, nu: (i, 0)),
    )
    return pl.pallas_call(
        functools.partial(_experts_body, tmx=tmx),
        grid_spec=grid_spec,
        out_shape=jax.ShapeDtypeStruct(xs.shape, F32),
        compiler_params=_cparams(("arbitrary",)),
        name="experts",
    )(tile_expert, n_used, xs, wgu, wd)


def _final_body(pos_ref, posn_ref, x_ref, rt_ref, p_ref, gple_ref, wpg_ref, wpp_ref, gfin_ref, ys_ref,
                o_ref, gbuf, sems, *, tf, nsub):
    i = pl.program_id(0)
    n = pl.num_programs(0)
    slot = i % 2

    def gather(pref, dst_slot):
        _issue_row_copies(pref, tf, nsub, lambda k, tok, src: pltpu.make_async_copy(
            ys_ref.at[src, :], gbuf.at[dst_slot, k, tok, :], sems.at[dst_slot]))

    @pl.when(i == 0)
    def _():
        gather(pos_ref, 0)

    @pl.when(i + 1 < n)
    def _():
        gather(posn_ref, 1 - slot)

    for k in range(2):
        pltpu.make_async_copy(ys_ref.at[pl.ds(0, tf * nsub), :], gbuf.at[slot, k], sems.at[slot]).wait()

    rt = rt_ref[...]
    g1 = rt[:, RT_G1:RT_G1 + 1]
    g2 = rt[:, RT_G2:RT_G2 + 1]
    x = x_ref[...] + g1 * _tiles_to_rows(gbuf.at[slot, 0], tf) + g2 * _tiles_to_rows(gbuf.at[slot, 1], tf)
    r = _rms(x, gple_ref[...]).astype(BF16)
    ple_gate = _sigmoid(_dot(r, wpg_ref[...]))
    proj = _dot(p_ref[...].astype(BF16), wpp_ref[...])
    o_ref[...] = _rms(x + ple_gate * proj, gfin_ref[...])


def _final(pos12, x1, rt, p2d, g_ple, wpg, wpp, g_final, ys, tf):
    t, d = x1.shape
    nsub = d // LANES
    n = t // tf
    row = lambda i: (i, 0)
    const = lambda i: (0, 0)
    smem_blk = lambda imap: pl.BlockSpec((1, 1, 2 * tf), imap, memory_space=pltpu.SMEM)
    return pl.pallas_call(
        functools.partial(_final_body, tf=tf, nsub=nsub),
        grid=(n,),
        in_specs=[smem_blk(lambda i: (i, 0, 0)), smem_blk(lambda i: (jnp.minimum(i + 1, n - 1), 0, 0)),
                  pl.BlockSpec((tf, d), row), pl.BlockSpec((tf, LANES), row),
                  pl.BlockSpec((tf, p2d.shape[1]), row),
                  pl.BlockSpec((1, d), const), pl.BlockSpec(wpg.shape, const),
                  pl.BlockSpec(wpp.shape, const), pl.BlockSpec((1, d), const),
                  pl.BlockSpec(memory_space=pl.ANY)],
        out_specs=pl.BlockSpec((tf, d), row),
        out_shape=jax.ShapeDtypeStruct((t, d), F32),
        scratch_shapes=[pltpu.VMEM((2, 2, tf * nsub, LANES), F32), pltpu.SemaphoreType.DMA((2,))],
        compiler_params=_cparams(("arbitrary",)),
        name="final",
    )(pos12, pos12, x1, rt, p2d, g_ple, wpg, wpp, g_final, ys)


def _pad_lanes(v, lane0):
    return jnp.zeros((1, LANES), F32).at[0, lane0:lane0 + v.shape[0]].set(v.astype(F32))


def _tile(n, pref):
    return pref if n % pref == 0 else n


def _layer(x, p_l, g_mix, w_in, b_forget, conv_w, a_log, dt_bias, g_onorm, w_o_fox, w_o_delta, w_out,
           g_ffn, w_group, b_group, w_router, b_router, w_gate, w_up, w_down, g_ple, w_ple_gate, w_ple_proj,
           g_post, apply_post):
    b, s, d = x.shape
    t = b * s
    x2d = x.reshape(t, d)

    o_ff = 3 * FOX_WIDTH
    o_qkv = o_ff + FOX_HEADS
    o_da = o_qkv + 3 * GDN_WIDTH
    o_db = o_da + GDN_HEADS
    o_dz = o_db + GDN_HEADS
    o_gf = o_dz + GDN_WIDTH
    o_gd = o_gf + d
    w_small = jnp.concatenate(
        [w_in[:, o_ff:o_qkv], w_in[:, o_da:o_db], w_in[:, o_db:o_dz],
         jnp.zeros((d, LANES - FOX_HEADS - 2 * GDN_HEADS), w_in.dtype)], axis=1)
    wq = w_in[:, :o_ff].astype(BF16)
    wg = w_in[:, o_qkv:o_da].astype(BF16)
    wzs = jnp.concatenate([w_in[:, o_dz:o_gf], w_small], axis=1).astype(BF16)
    wgf = w_in[:, o_gf:o_gd].astype(BF16)
    wgd = w_in[:, o_gd:].astype(BF16)

    zq, zg, dz, small, gf, gd = _inproj(x2d, g_mix.reshape(1, d), wq, wg, wzs, wgf, wgd, _tile(t, 512))

    par = jnp.concatenate(
        [_pad_lanes(b_forget, 0) + _pad_lanes(dt_bias, FOX_HEADS), _pad_lanes(a_log, FOX_HEADS),
         jnp.zeros((6, LANES), F32)], axis=0)
    qn, kn, vv, col = _gdnprep(zg.reshape(b, s, 3 * GDN_WIDTH), conv_w.astype(F32),
                               small.reshape(b, s, LANES), par, _tile(s, 512))

    nc = s // GDN_CHUNK
    gc = col[:, :, FOX_HEADS:FOX_HEADS + GDN_HEADS].reshape(b, nc, GDN_CHUNK, GDN_HEADS)
    gcrow = jnp.transpose(gc, (0, 1, 3, 2)).reshape(b, nc, 1, GDN_HEADS * GDN_CHUNK)

    y_fox = _fox(zq.reshape(b, s, 3 * FOX_WIDTH), col, _tile(s, 1024))
    o_gdn = _gdn(qn, kn, vv, col, gcrow, _tile(s, 512))

    w_r = jnp.concatenate([w_group, w_router,
                           jnp.zeros((d, LANES - N_GROUPS - N_EXPERTS), w_group.dtype)], axis=1).astype(F32)
    wrh = w_r.astype(BF16)
    wrl = (w_r - wrh.astype(F32)).astype(BF16)
    br = _pad_lanes(b_group, 0) + _pad_lanes(b_router, EXPERT_LANE0)
    x1, tpk, rt, cnt = _combine(
        y_fox.reshape(t, FOX_WIDTH), o_gdn.reshape(t, GDN_WIDTH), dz, gf, gd, x2d,
        w_o_fox.astype(BF16), w_o_delta.astype(BF16), w_out.astype(BF16),
        g_onorm.reshape(1, GDN_HEAD_DIM).astype(F32), g_ffn.reshape(1, d).astype(F32), wrh, wrl, br,
        _tile(t, 512))

    tmx = MOE_SLOT_TILE
    n_slots = 2 * t + N_EXPERTS * tmx
    counts = cnt[0, EXPERT_LANE0:EXPERT_LANE0 + N_EXPERTS].astype(jnp.int32)
    padded = (counts + tmx - 1) // tmx * tmx
    ends = jnp.cumsum(padded)
    pos = _slots(rt, cnt, tmx, _tile(t, 2048))
    pos1, pos2 = pos[0], pos[1]
    tile_start = jnp.arange(n_slots // tmx, dtype=jnp.int32) * tmx
    tile_expert = jnp.minimum(jnp.sum(tile_start[:, None] >= ends[None, :], axis=1), N_EXPERTS - 1).astype(jnp.int32)
    n_used = (ends[-1:] // tmx).astype(jnp.int32)
    last_of_expert = jnp.any(jnp.logical_and(tile_start[:, None] + tmx == ends[None, :], padded[None, :] > 0), axis=1)
    zflag = jnp.logical_or(last_of_expert, tile_start >= ends[-1]).astype(jnp.int32)

    def tiled_pos(tile):
        n = t // tile
        return jnp.concatenate([pos1.reshape(n, 1, tile), pos2.reshape(n, 1, tile)], axis=2)

    td = _tile(t, 2048)
    nsub = d // LANES
    xs = _dispatch(zflag, tpk, tiled_pos(td), n_slots, td, tmx, nsub)
    wgu = jnp.concatenate([w_gate, w_up], axis=2).astype(BF16)
    ys = _experts(tile_expert, n_used, xs, wgu, w_down.astype(BF16), tmx)

    tf = _tile(t, 128)
    out = _final(tiled_pos(tf), x1, rt, p_l.reshape(t, -1), g_ple.reshape(1, d).astype(F32),
                 w_ple_gate.astype(BF16), w_ple_proj.astype(BF16), g_post.reshape(1, d).astype(F32), ys, tf)
    return out.reshape(b, s, d)


def kernel(x, p, g_mix, w_in, b_forget, conv_w, a_log, dt_bias, g_onorm, w_o_fox, w_o_delta, w_out,
           g_ffn, w_group, b_group, w_router, b_router, w_gate, w_up, w_down, g_ple, w_ple_gate, w_ple_proj,
           g_final):
    depth = p.shape[0]
    assert depth == 1, "the final rmsnorm is fused into the last layer's epilogue; depth 1 only"
    i = 0
    return _layer(x, p[i], g_mix[i], w_in[i], b_forget[i], conv_w[i], a_log[i], dt_bias[i], g_onorm[i],
                  w_o_fox[i], w_o_delta[i], w_out[i], g_ffn[i], w_group[i], b_group[i], w_router[i],
                  b_router[i], w_gate[i], w_up[i], w_down[i], g_ple[i], w_ple_gate[i], w_ple_proj[i],
                  g_final, True)
```

```python
import functools

import jax
import jax.numpy as jnp
from jax import lax
from jax.experimental import pallas as pl
from jax.experimental.pallas import tpu as pltpu

F32 = jnp.float32
BF16 = jnp.bfloat16
EPS = 1e-6
NEG = -1e30

FOX_HEADS = 8
FOX_HEAD_DIM = 64
FOX_WIDTH = FOX_HEADS * FOX_HEAD_DIM
GDN_HEADS = 4
GDN_HEAD_DIM = 128
GDN_CONV = 4
GDN_CHUNK = 64
GDN_WIDTH = GDN_HEADS * GDN_HEAD_DIM
N_GROUPS = 4
EXPERTS_PER_GROUP = 8
N_EXPERTS = N_GROUPS * EXPERTS_PER_GROUP
EXPERT_FF = 256
LANES = 128
EXPERT_LANE0 = N_GROUPS

VMEM_LIMIT_BYTES = 56 * 1024 * 1024


def _cparams(sem):
    return pltpu.CompilerParams(dimension_semantics=sem, vmem_limit_bytes=VMEM_LIMIT_BYTES)


def _rms(x, g):
    return x * lax.rsqrt(jnp.mean(x * x, axis=-1, keepdims=True) + EPS) * g


def _sigmoid(x):
    return 1.0 / (1.0 + jnp.exp(-x))


def _dot(a, b):
    return jnp.dot(a, b, preferred_element_type=F32)


def _dot_nt(a, b):
    return lax.dot_general(a, b, (((1,), (1,)), ((), ())), preferred_element_type=F32)


def _dot_tn(a, b):
    return lax.dot_general(a, b, (((0,), (0,)), ((), ())), preferred_element_type=F32)


def _split3(v):
    hi = v.astype(BF16)
    r = v - hi.astype(F32)
    mid = r.astype(BF16)
    lo = (r - mid.astype(F32)).astype(BF16)
    return hi, mid, lo


def _dot_exact_lhs01(mat01, v):
    hi, mid, lo = _split3(v)
    return _dot(mat01, hi) + _dot(mat01, mid) + _dot(mat01, lo)


def _inproj_body(x_ref, g_ref, wq_ref, wg_ref, wzs_ref, wgf_ref, wgd_ref,
                 oq_ref, og_ref, oz_ref, osm_ref, ogf_ref, ogd_ref):
    hb = _rms(x_ref[...], g_ref[...]).astype(BF16)
    oq_ref[...] = _dot(hb, wq_ref[...]).astype(BF16)
    og_ref[...] = _dot(hb, wg_ref[...]).astype(BF16)
    zs = _dot(hb, wzs_ref[...])
    oz_ref[...] = zs[:, :GDN_WIDTH].astype(BF16)
    osm_ref[...] = zs[:, GDN_WIDTH:]
    ogf_ref[...] = _dot(hb, wgf_ref[...]).astype(BF16)
    ogd_ref[...] = _dot(hb, wgd_ref[...]).astype(BF16)


def _inproj(x2d, g_mix, wq, wg, wzs, wgf, wgd, tm):
    t, d = x2d.shape
    row = lambda i: (i, 0)
    const = lambda i: (0, 0)
    widths = (3 * FOX_WIDTH, 3 * GDN_WIDTH, GDN_WIDTH, LANES, d, d)
    dtypes = (BF16, BF16, BF16, F32, BF16, BF16)
    return pl.pallas_call(
        _inproj_body,
        grid=(t // tm,),
        in_specs=[pl.BlockSpec((tm, d), row), pl.BlockSpec((1, d), const)]
        + [pl.BlockSpec(w.shape, const) for w in (wq, wg, wzs, wgf, wgd)],
        out_specs=[pl.BlockSpec((tm, n), row) for n in widths],
        out_shape=[jax.ShapeDtypeStruct((t, n), dt) for n, dt in zip(widths, dtypes)],
        compiler_params=_cparams(("parallel",)),
        name="inproj",
    )(x2d, g_mix, wq, wg, wzs, wgf, wgd)


def _gates_tile(sm, par_ref, carry_ref):
    tp = sm.shape[0]
    lane = lax.broadcasted_iota(jnp.int32, sm.shape, 1)
    z = sm + par_ref[0:1, :]
    soft = jnp.log1p(jnp.exp(-jnp.abs(z)))
    softplus = jnp.maximum(z, 0.0) + soft
    logf = jnp.minimum(z, 0.0) - soft
    g = -jnp.exp(par_ref[1:2, :]) * softplus
    beta = _sigmoid(sm)
    is_f = lane < FOX_HEADS
    is_g = jnp.logical_and(lane >= FOX_HEADS, lane < FOX_HEADS + GDN_HEADS)
    is_b = jnp.logical_and(lane >= FOX_HEADS + GDN_HEADS, lane < FOX_HEADS + 2 * GDN_HEADS)
    val = jnp.where(is_f, logf, jnp.where(is_g, g, 0.0))
    r = lax.broadcasted_iota(jnp.int32, (tp, tp), 0)
    c = lax.broadcasted_iota(jnp.int32, (tp, tp), 1)
    lower = r >= c
    tri = jnp.where(lower, 1.0, 0.0).astype(BF16)
    same_chunk = (r // GDN_CHUNK) == (c // GDN_CHUNK)
    tri_chunk = jnp.where(jnp.logical_and(lower, same_chunk), 1.0, 0.0).astype(BF16)
    cum = _dot_exact_lhs01(tri, val) + carry_ref[...]
    gcs = _dot_exact_lhs01(tri_chunk, val)
    carry_ref[...] = cum[tp - 1:tp, :]
    return jnp.where(is_f, cum, jnp.where(is_g, gcs, jnp.where(is_b, beta, 0.0)))


LOG2E = 1.4426950408889634


FOX_VROWS = FOX_HEAD_DIM + 16


def _split3_f32(v):
    hi = v.astype(BF16).astype(F32)
    r = v - hi
    mid = r.astype(BF16).astype(F32)
    lo = (r - mid).astype(BF16).astype(F32)
    return hi, mid, lo


def _fox_body(q_ref, k_ref, v_ref, col_ref, o_ref, kx_ref, vt_ref, sa_ref, sb_ref, acc_ref, m_ref,
              *, tq, s_len):
    hp = pl.program_id(1)
    qi = pl.program_id(2)
    hd = FOX_HEAD_DIM

    @pl.when(qi == 0)
    def _():
        def prep(ci, carry):
            r0 = pl.multiple_of(ci * tq, tq)
            kx_ref[pl.ds(r0, tq), :LANES] = k_ref[0, pl.ds(r0, tq), :]
            col = col_ref[0, pl.ds(r0, tq), :]
            lane = lax.broadcasted_iota(jnp.int32, col.shape, 1)
            bias = jnp.zeros(col.shape, F32)
            for hh in range(2):
                ck = jnp.sum(jnp.where(lane == 2 * hp + hh, col, 0.0), axis=-1, keepdims=True) * LOG2E
                for k, piece in enumerate(_split3_f32(ck)):
                    bias = jnp.where(lane == 3 * hh + k, piece, bias)
            kx_ref[pl.ds(r0, tq), LANES:] = bias.astype(BF16)
            vt = v_ref[0, pl.ds(r0, tq), :].astype(F32).T
            ones_row = jnp.where(lax.broadcasted_iota(jnp.int32, (FOX_VROWS - hd, tq), 0) == 0, 1.0, 0.0)
            for hh in range(2):
                vt_ref[hh, :hd, pl.ds(r0, tq)] = vt[hh * hd:(hh + 1) * hd].astype(BF16)
                vt_ref[hh, hd:, pl.ds(r0, tq)] = ones_row.astype(BF16)
            return carry

        lax.fori_loop(0, s_len // tq, prep, 0)

    lane = lax.broadcasted_iota(jnp.int32, (tq, LANES), 1)
    q = (q_ref[0].astype(F32) * (hd ** -0.5 * LOG2E)).astype(BF16)
    zero = jnp.zeros_like(q)
    qx = []
    for hh in range(2):
        qh = jnp.where(jnp.logical_and(lane >= hh * hd, lane < (hh + 1) * hd), q, zero)
        sel = jnp.where(jnp.logical_and(lane >= 3 * hh, lane < 3 * hh + 3), -1.0, 0.0).astype(BF16)
        qx.append(jnp.concatenate([qh, sel], axis=1))
    m_ref[...] = jnp.full(m_ref.shape, NEG, F32)
    acc_ref[...] = jnp.zeros(acc_ref.shape, F32)

    def scores(j, dst_ref):
        start = pl.multiple_of(j * tq, tq)
        kx = kx_ref[pl.ds(start, tq), :]
        for hh in range(2):
            dst_ref[hh] = _dot_nt(kx, qx[hh])

    def consume(j, src_ref, masked):
        half = tq // 2
        pieces = ((0, tq, 0, tq),) if not masked else ((0, half, 0, tq), (half, tq, half, tq))
        for k0, k1, q0, q1 in pieces:
            start = pl.multiple_of(j * tq + k0, half)
            for hh in range(2):
                s = src_ref[hh, k0:k1, q0:q1]
                if masked:
                    key = lax.broadcasted_iota(jnp.int32, s.shape, 0) + k0
                    qry = lax.broadcasted_iota(jnp.int32, s.shape, 1) + q0
                    s = jnp.where(key <= qry, s, NEG)
                m_old = m_ref[hh:hh + 1, q0:q1]
                m_new = jnp.maximum(m_old, jnp.max(s, axis=0, keepdims=True))
                alpha = jnp.exp2(m_old - m_new)
                p = jnp.exp2(s - m_new).astype(BF16)
                acc_ref[hh, :, q0:q1] = alpha * acc_ref[hh, :, q0:q1] + _dot(vt_ref[hh, :, pl.ds(start, k1 - k0)], p)
                m_ref[hh:hh + 1, q0:q1] = m_new

    scores(0, sa_ref)

    def loop_body(j, carry):
        @pl.when(j % 2 == 0)
        def _():
            scores(j + 1, sb_ref)
            consume(j, sa_ref, False)

        @pl.when(j % 2 == 1)
        def _():
            scores(j + 1, sa_ref)
            consume(j, sb_ref, False)

        return carry

    lax.fori_loop(0, qi, loop_body, 0)

    @pl.when(qi % 2 == 0)
    def _():
        consume(qi, sa_ref, True)

    @pl.when(qi % 2 == 1)
    def _():
        consume(qi, sb_ref, True)

    out_t = jnp.concatenate([acc_ref[hh, :hd] / acc_ref[hh, hd:hd + 1] for hh in range(2)], axis=0)
    o_ref[0] = out_t.T.astype(BF16)


def _fox(zq3d, col3d, tq):
    b, s, _ = zq3d.shape
    npair = FOX_HEADS // 2
    kblk = FOX_WIDTH // LANES
    return pl.pallas_call(
        functools.partial(_fox_body, tq=tq, s_len=s),
        grid=(b, npair, s // tq),
        in_specs=[pl.BlockSpec((1, tq, LANES), lambda bi, hp, qi: (bi, qi, hp)),
                  pl.BlockSpec((1, s, LANES), lambda bi, hp, qi: (bi, 0, kblk + hp)),
                  pl.BlockSpec((1, s, LANES), lambda bi, hp, qi: (bi, 0, 2 * kblk + hp)),
                  pl.BlockSpec((1, s, LANES), lambda bi, hp, qi: (bi, 0, 0))],
        out_specs=pl.BlockSpec((1, tq, LANES), lambda bi, hp, qi: (bi, qi, hp)),
        out_shape=jax.ShapeDtypeStruct((b, s, FOX_WIDTH), BF16),
        scratch_shapes=[pltpu.VMEM((s, 2 * LANES), BF16), pltpu.VMEM((2, FOX_VROWS, s), BF16),
                        pltpu.VMEM((2, tq, tq), F32), pltpu.VMEM((2, tq, tq), F32),
                        pltpu.VMEM((2, FOX_VROWS, tq), F32), pltpu.VMEM((2, tq), F32)],
        compiler_params=_cparams(("parallel", "parallel", "arbitrary")),
        name="fox",
    )(zq3d, zq3d, zq3d, col3d)


HALO = 16


def _gdnprep_body(x_ref, halo_ref, cw_ref, sm_ref, par_ref, oq_ref, ok_ref, ov_ref, ocol_ref, ext_ref, carry_ref,
                  *, tp):
    @pl.when(pl.program_id(1) == 0)
    def _():
        carry_ref[...] = jnp.zeros_like(carry_ref)

    ocol_ref[0] = _gates_tile(sm_ref[0], par_ref, carry_ref)
    prev = halo_ref[0].astype(F32)
    ext_ref[0:HALO, :] = jnp.where(pl.program_id(1) > 0, prev, 0.0)
    ext_ref[HALO:, :] = x_ref[0].astype(F32)
    acc = cw_ref[GDN_CONV - 1:GDN_CONV, :] * ext_ref[HALO:HALO + tp, :]
    for j in range(GDN_CONV - 1):
        off = HALO - (GDN_CONV - 1) + j
        acc = acc + cw_ref[j:j + 1, :] * ext_ref[off:off + tp, :]
    y = acc * _sigmoid(acc)

    def l2(v):
        return v * lax.rsqrt(jnp.sum(v * v, axis=-1, keepdims=True) + EPS)

    for h in range(GDN_HEADS):
        lo, hi = h * GDN_HEAD_DIM, (h + 1) * GDN_HEAD_DIM
        oq_ref[0, :, lo:hi] = (l2(y[:, lo:hi]) * GDN_HEAD_DIM ** -0.5).astype(BF16)
        ok_ref[0, :, lo:hi] = l2(y[:, GDN_WIDTH + lo:GDN_WIDTH + hi]).astype(BF16)
    ov_ref[0] = y[:, 2 * GDN_WIDTH:].astype(BF16)


def _gdnprep(zg3d, conv_w, small3d, par, tp):
    b, s, c = zg3d.shape
    blk = lambda bi, i: (bi, i, 0)
    return pl.pallas_call(
        functools.partial(_gdnprep_body, tp=tp),
        grid=(b, s // tp),
        in_specs=[pl.BlockSpec((1, tp, c), blk),
                  pl.BlockSpec((1, HALO, c), lambda bi, i: (bi, jnp.maximum(i * (tp // HALO) - 1, 0), 0)),
                  pl.BlockSpec(conv_w.shape, lambda bi, i: (0, 0)),
                  pl.BlockSpec((1, tp, LANES), blk), pl.BlockSpec(par.shape, lambda bi, i: (0, 0))],
        out_specs=[pl.BlockSpec((1, tp, GDN_WIDTH), blk)] * 3 + [pl.BlockSpec((1, tp, LANES), blk)],
        out_shape=[jax.ShapeDtypeStruct((b, s, GDN_WIDTH), BF16)] * 3 + [jax.ShapeDtypeStruct((b, s, LANES), F32)],
        scratch_shapes=[pltpu.VMEM((tp + HALO, c), F32), pltpu.VMEM((1, LANES), F32)],
        compiler_params=_cparams(("parallel", "arbitrary")),
        name="gdnprep",
    )(zg3d, zg3d, conv_w, small3d, par)


def _stack_heads(x):
    return jnp.concatenate([x[:, h * GDN_HEAD_DIM:(h + 1) * GDN_HEAD_DIM] for h in range(GDN_HEADS)], axis=0)


def _gdn_body(q_ref, k_ref, v_ref, col_ref, grow_ref, o_ref, state_ref, *bufs, tg):
    C = GDN_CHUNK
    R = GDN_HEADS * C
    dh = GDN_HEAD_DIM
    n_chunks = tg // C
    step = pl.program_id(1)
    buf_sets = (bufs[:len(bufs) // 2], bufs[len(bufs) // 2:])

    @pl.when(step == 0)
    def _():
        for ref in (state_ref,) + tuple(bufs):
            ref[...] = jnp.zeros_like(ref)

    r = lax.broadcasted_iota(jnp.int32, (R, R), 0)
    c = lax.broadcasted_iota(jnp.int32, (R, R), 1)
    same_head = (r // C) == (c // C)
    lower = jnp.logical_and(same_head, r >= c)
    strict = jnp.logical_and(same_head, r > c)
    gc_lane0 = FOX_HEADS
    beta_lane0 = FOX_HEADS + GDN_HEADS

    def advance(ci, rd):
        u_ref, w_ref, intra_ref, qd_ref, kd_ref, gl_ref = rd
        r0 = ci * C
        u, w, intra = u_ref[ci], w_ref[ci], intra_ref[ci]
        q_dec, k_dec = qd_ref[ci], kd_ref[ci]
        v_new = []
        o_state = []
        for h in range(GDN_HEADS):
            sl = slice(h * C, (h + 1) * C)
            st = state_ref[h].astype(BF16)
            v_new.append(u[sl] - _dot(w[sl], st))
            o_state.append(_dot(q_dec[sl], st))
        v_new = jnp.concatenate(v_new, axis=0)
        v_new_b = v_new.astype(BF16)
        o_all = jnp.concatenate(o_state, axis=0) + _dot(intra, v_new_b)
        for h in range(GDN_HEADS):
            sl = slice(h * C, (h + 1) * C)
            state_ref[h] = state_ref[h] * gl_ref[ci, h:h + 1, :] + _dot_tn(k_dec[sl], v_new_b[sl])
        o_ref[0, r0:r0 + C, :] = jnp.concatenate(
            [o_all[h * C:(h + 1) * C] for h in range(GDN_HEADS)], axis=1).astype(BF16)

    chunks = range(n_chunks)

    def prepare_all(wr):
        u_ref, w_ref, intra_ref, qd_ref, kd_ref, gl_ref = wr
        qs, ks, vs, gc_col, beta_col, gl_row, lmat, intra = [], [], [], [], [], [], [], []
        for ci in chunks:
            r0 = ci * C
            qs.append(_stack_heads(q_ref[0, r0:r0 + C, :]).astype(F32))
            ks.append(_stack_heads(k_ref[0, r0:r0 + C, :]).astype(F32))
            vs.append(_stack_heads(v_ref[0, r0:r0 + C, :]).astype(F32))
            col = col_ref[0, r0:r0 + C, :]
            gc_col.append(
                jnp.concatenate([col[:, gc_lane0 + h:gc_lane0 + h + 1] for h in range(GDN_HEADS)], axis=0))
            beta_col.append(
                jnp.concatenate([col[:, beta_lane0 + h:beta_lane0 + h + 1] for h in range(GDN_HEADS)], axis=0))
            gl_row.append(col[C - 1:C, :])
            gc_row = grow_ref[0, ci, :, :]
            decay = jnp.exp(jnp.where(lower, gc_col[ci] - gc_row, NEG))
            ksb = ks[ci].astype(BF16)
            kk = _dot_nt(ksb, ksb)
            qk = _dot_nt(qs[ci].astype(BF16), ksb)
            lmat.append(jnp.where(strict, kk * decay * beta_col[ci], 0.0))
            intra.append((qk * decay).astype(BF16))
        def side_by_side(bd):
            return functools.reduce(lambda a, b: a + b, [bd[h * C:(h + 1) * C] for h in range(GDN_HEADS)])

        def block_diag(sbs):
            return jnp.where(same_head, jnp.concatenate([sbs] * GDN_HEADS, axis=0), 0.0)

        pw_bd = [l.astype(BF16) for l in lmat]
        pw = [side_by_side(l) for l in lmat]
        n_sbs = [-p for p in pw]
        for _ in range(5):
            pw = [_dot(p.astype(BF16), b) for p, b in zip(pw, pw_bd)]
            pw_bd = [block_diag(p).astype(BF16) for p in pw]
            n_sbs = [n + p + _dot(n.astype(BF16), b) for n, p, b in zip(n_sbs, pw, pw_bd)]
        n_mat = [block_diag(n) for n in n_sbs]
        for ci in chunks:
            e_gc = jnp.exp(gc_col[ci])
            rhs = jnp.concatenate([vs[ci] * beta_col[ci], ks[ci] * (beta_col[ci] * e_gc)], axis=1)
            sol = rhs + _dot(n_mat[ci].astype(BF16), rhs.astype(BF16))
            gl_col = jnp.concatenate(
                [jnp.broadcast_to(gl_row[ci][:, gc_lane0 + h:gc_lane0 + h + 1], (C, 1))
                 for h in range(GDN_HEADS)], axis=0)
            u_ref[ci] = sol[:, :dh]
            w_ref[ci] = sol[:, dh:].astype(BF16)
            intra_ref[ci] = intra[ci]
            qd_ref[ci] = (qs[ci] * e_gc).astype(BF16)
            kd_ref[ci] = (ks[ci] * jnp.exp(gl_col - gc_col[ci])).astype(BF16)
            for h in range(GDN_HEADS):
                gl_ref[ci, h:h + 1, :] = jnp.broadcast_to(
                    jnp.exp(gl_row[ci][:, gc_lane0 + h:gc_lane0 + h + 1]), (1, LANES))

    def run(rd, wr):
        for ci in chunks:
            advance(ci, rd)
        prepare_all(wr)

    @pl.when(step % 2 == 0)
    def _():
        run(buf_sets[0], buf_sets[1])

    @pl.when(step % 2 == 1)
    def _():
        run(buf_sets[1], buf_sets[0])


def _gdn(qn, kn, vv, col3d, gcrow, tg):
    b, s, _ = qn.shape
    n = s // tg
    nck = tg // GDN_CHUNK
    rows = GDN_HEADS * GDN_CHUNK
    dh = GDN_HEAD_DIM
    blk_in = lambda bi, i: (bi, jnp.minimum(i, n - 1), 0)
    blk_out = lambda bi, i: (bi, jnp.maximum(i - 1, 0), 0)
    return pl.pallas_call(
        functools.partial(_gdn_body, tg=tg),
        grid=(b, n + 1),
        in_specs=[pl.BlockSpec((1, tg, GDN_WIDTH), blk_in)] * 3
        + [pl.BlockSpec((1, tg, LANES), blk_in),
           pl.BlockSpec((1, nck, 1, rows), lambda bi, i: (bi, jnp.minimum(i, n - 1), 0, 0))],
        out_specs=pl.BlockSpec((1, tg, GDN_WIDTH), blk_out),
        out_shape=jax.ShapeDtypeStruct((b, s, GDN_WIDTH), BF16),
        scratch_shapes=[pltpu.VMEM((GDN_HEADS, dh, dh), F32)] + 2 * [
            pltpu.VMEM((nck, rows, dh), F32), pltpu.VMEM((nck, rows, dh), BF16),
            pltpu.VMEM((nck, rows, rows), BF16), pltpu.VMEM((nck, rows, dh), BF16),
            pltpu.VMEM((nck, rows, dh), BF16), pltpu.VMEM((nck, 8, LANES), F32)],
        compiler_params=_cparams(("parallel", "arbitrary")),
        name="gdn",
    )(qn, kn, vv, col3d, gcrow)


MOE_SLOT_TILE = 512


def _rows_to_tiles(ref, v):
    m, width = v.shape
    n = width // LANES
    for s in range(n):
        ref[pl.ds(s, m, stride=n), :] = v[:, s * LANES:(s + 1) * LANES]


def _tiles_to_rows(ref, m):
    n = ref.shape[0] // m
    return jnp.concatenate([ref[pl.ds(s, m, stride=n), :] for s in range(n)], axis=1)


RT_E1, RT_E2, RT_RANK1, RT_RANK2, RT_G1, RT_G2 = range(6)


def _combine_body(yf_ref, og_ref, dz_ref, gf_ref, gd_ref, x_ref, wof_ref, wod_ref, wout_ref,
                  gon_ref, gffn_ref, wrh_ref, wrl_ref, br_ref, x1_ref, t_ref, rt_ref, cnt_ref, carry_ref,
                  *, tc):
    @pl.when(pl.program_id(0) == 0)
    def _():
        carry_ref[...] = jnp.zeros_like(carry_ref)

    on = []
    for h in range(GDN_HEADS):
        sl = slice(h * GDN_HEAD_DIM, (h + 1) * GDN_HEAD_DIM)
        dz = dz_ref[:, sl].astype(F32)
        on.append(_rms(og_ref[:, sl].astype(F32), gon_ref[...]) * (dz * _sigmoid(dz)))
    on = jnp.concatenate(on, axis=1).astype(BF16)
    y_fox = _dot(yf_ref[...], wof_ref[...])
    y_delta = _dot(on, wod_ref[...])
    merged = _sigmoid(gf_ref[...].astype(F32)) * y_fox + _sigmoid(gd_ref[...].astype(F32)) * y_delta
    x1 = x_ref[...] + _dot(merged.astype(BF16), wout_ref[...])
    x1_ref[...] = x1
    t32 = _rms(x1, gffn_ref[...])
    th = t32.astype(BF16)
    tl = (t32 - th.astype(F32)).astype(BF16)
    _rows_to_tiles(t_ref, t32)
    logits = _dot(th, wrh_ref[...]) + _dot(tl, wrh_ref[...]) + _dot(th, wrl_ref[...]) + br_ref[...]
    lane = lax.broadcasted_iota(jnp.int32, logits.shape, 1)
    gl = jnp.where(lane < N_GROUPS, logits, NEG)
    gmax = jnp.max(gl, axis=-1, keepdims=True)
    g_sel = jnp.min(jnp.where(gl == gmax, lane, LANES), axis=-1, keepdims=True)
    p_sel = 1.0 / jnp.sum(jnp.exp(gl - gmax), axis=-1, keepdims=True)
    lo = EXPERT_LANE0 + EXPERTS_PER_GROUP * g_sel
    in_grp = jnp.logical_and(lane >= lo, lane < lo + EXPERTS_PER_GROUP)
    el = jnp.where(in_grp, logits, NEG)
    emax = jnp.max(el, axis=-1, keepdims=True)
    ee = jnp.where(in_grp, jnp.exp(el - emax), 0.0)
    pe = ee / jnp.sum(ee, axis=-1, keepdims=True)
    pe = jnp.where(in_grp, pe, -1.0)
    p1 = jnp.max(pe, axis=-1, keepdims=True)
    i1 = jnp.min(jnp.where(pe == p1, lane, LANES), axis=-1, keepdims=True)
    pe2 = jnp.where(lane == i1, -1.0, pe)
    p2 = jnp.max(pe2, axis=-1, keepdims=True)
    i2 = jnp.min(jnp.where(pe2 == p2, lane, LANES), axis=-1, keepdims=True)
    den = p1 + p2
    hit1 = lane == i1
    hit2 = lane == i2
    assign = jnp.where(jnp.logical_or(hit1, hit2), 1.0, 0.0)
    r = lax.broadcasted_iota(jnp.int32, (tc, tc), 0)
    c = lax.broadcasted_iota(jnp.int32, (tc, tc), 1)
    before = jnp.where(r > c, 1.0, 0.0).astype(BF16)
    prefix = _dot(before, assign.astype(BF16)) + carry_ref[...]
    rank1 = jnp.sum(jnp.where(hit1, prefix, 0.0), axis=-1, keepdims=True)
    rank2 = jnp.sum(jnp.where(hit2, prefix, 0.0), axis=-1, keepdims=True)
    carry_ref[...] = prefix[tc - 1:tc, :] + assign[tc - 1:tc, :]
    cnt_ref[...] = carry_ref[...]
    cols = ((i1 - EXPERT_LANE0).astype(F32), (i2 - EXPERT_LANE0).astype(F32), rank1, rank2,
            p_sel * (p1 / den), p_sel * (p2 / den))
    rt = jnp.zeros(logits.shape, F32)
    for k, v in enumerate(cols):
        rt = jnp.where(lane == k, v, rt)
    rt_ref[...] = rt


def _combine(yf, og, dz, gf, gd, x2d, wof, wod, wout, g_on, g_ffn, wrh, wrl, br, tc):
    t, d = x2d.shape
    row = lambda i: (i, 0)
    const = lambda i: (0, 0)
    acts = (yf, og, dz, gf, gd, x2d)
    consts = (wof, wod, wout, g_on, g_ffn, wrh, wrl, br)
    return pl.pallas_call(
        functools.partial(_combine_body, tc=tc),
        grid=(t // tc,),
        in_specs=[pl.BlockSpec((tc, a.shape[1]), row) for a in acts]
        + [pl.BlockSpec(c.shape, const) for c in consts],
        out_specs=[pl.BlockSpec((tc, d), row), pl.BlockSpec((tc * (d // LANES), LANES), row),
                   pl.BlockSpec((tc, LANES), row), pl.BlockSpec((1, LANES), const)],
        out_shape=[jax.ShapeDtypeStruct((t, d), F32), jax.ShapeDtypeStruct((t * (d // LANES), LANES), F32),
                   jax.ShapeDtypeStruct((t, LANES), F32), jax.ShapeDtypeStruct((1, LANES), F32)],
        scratch_shapes=[pltpu.VMEM((1, LANES), F32)],
        compiler_params=_cparams(("arbitrary",)),
        name="combine",
    )(*acts, *consts)


def _slots_body(rt_ref, cnt_ref, o_ref, *, tmx):
    lane1 = lax.broadcasted_iota(jnp.int32, (1, LANES), 1)
    is_expert = jnp.logical_and(lane1 >= EXPERT_LANE0, lane1 < EXPERT_LANE0 + N_EXPERTS)
    padded = jnp.where(is_expert, jnp.floor((cnt_ref[...] + (tmx - 1)) / tmx) * tmx, 0.0)
    r = lax.broadcasted_iota(jnp.int32, (LANES, LANES), 0)
    c = lax.broadcasted_iota(jnp.int32, (LANES, LANES), 1)
    before = jnp.where(r < c, 1.0, 0.0).astype(BF16)
    hi, mid, lo = _split3(jnp.broadcast_to(padded, (8, LANES)))
    offs = (_dot(hi, before) + _dot(mid, before) + _dot(lo, before))[0:1, :]
    rt = rt_ref[...]
    lane = lax.broadcasted_iota(jnp.int32, rt.shape, 1)
    expert_of_lane = (lane - EXPERT_LANE0).astype(F32)
    out = jnp.zeros(rt.shape, F32)
    for k, (ce, cr) in enumerate(((RT_E1, RT_RANK1), (RT_E2, RT_RANK2))):
        start = jnp.sum(jnp.where(expert_of_lane == rt[:, ce:ce + 1], offs, 0.0), axis=-1, keepdims=True)
        out = jnp.where(lane == k, start + rt[:, cr:cr + 1], out)
    o_ref[...] = out.T[:8, :].astype(jnp.int32)


def _slots(rt, cnt, tmx, ts):
    t = rt.shape[0]
    return pl.pallas_call(
        functools.partial(_slots_body, tmx=tmx),
        grid=(t // ts,),
        in_specs=[pl.BlockSpec((ts, LANES), lambda i: (i, 0)), pl.BlockSpec((1, LANES), lambda i: (0, 0))],
        out_specs=pl.BlockSpec((8, ts), lambda i: (0, i)),
        out_shape=jax.ShapeDtypeStruct((8, t), jnp.int32),
        compiler_params=_cparams(("parallel",)),
        name="slots",
    )(rt, cnt)


ROW_GROUP = 8


def _issue_row_copies(pos_ref, n_rows, nsub, make_copy):
    def issue(g, carry):
        r0 = g * ROW_GROUP
        slots = [[pos_ref[0, 0, k * n_rows + r0 + u] for k in range(2)] for u in range(ROW_GROUP)]
        for u in range(ROW_GROUP):
            tok = pl.ds(pl.multiple_of((r0 + u) * nsub, nsub), nsub)
            for k in range(2):
                slot = pl.ds(pl.multiple_of(slots[u][k] * nsub, nsub), nsub)
                make_copy(k, tok, slot).start(priority=k)
        return carry

    lax.fori_loop(0, n_rows // ROW_GROUP, issue, 0)


def _dispatch_body(zflag_ref, pos_ref, t_ref, xs_ref, zbuf, sem, zsem, *, td, tmx, n_tiles, nsub):
    @pl.when(pl.program_id(0) == 0)
    def _():
        zbuf[...] = jnp.zeros_like(zbuf)
        rows = tmx * nsub

        def zero_tile(k, carry):
            @pl.when(zflag_ref[k] != 0)
            def _():
                cp = pltpu.make_async_copy(zbuf, xs_ref.at[pl.ds(pl.multiple_of(k * rows, rows), rows), :], zsem)
                cp.start()
                cp.wait()

            return carry

        lax.fori_loop(0, n_tiles, zero_tile, 0)

    _issue_row_copies(pos_ref, td, nsub,
                      lambda k, tok, slot: pltpu.make_async_copy(t_ref.at[tok, :], xs_ref.at[slot, :], sem))
    for _ in range(2):
        pltpu.make_async_copy(t_ref, xs_ref.at[pl.ds(0, td * nsub), :], sem).wait()


def _dispatch(zflag, t_tiles, pos12, n_slots, td, tmx, nsub):
    t = t_tiles.shape[0] // nsub
    n_tiles = n_slots // tmx
    grid_spec = pltpu.PrefetchScalarGridSpec(
        num_scalar_prefetch=1,
        grid=(t // td,),
        in_specs=[pl.BlockSpec((1, 1, 2 * td), lambda i, zf: (i, 0, 0), memory_space=pltpu.SMEM),
                  pl.BlockSpec((td * nsub, LANES), lambda i, zf: (i, 0))],
        out_specs=pl.BlockSpec(memory_space=pl.ANY),
        scratch_shapes=[pltpu.VMEM((tmx * nsub, LANES), F32), pltpu.SemaphoreType.DMA(()),
                        pltpu.SemaphoreType.DMA(())],
    )
    return pl.pallas_call(
        functools.partial(_dispatch_body, td=td, tmx=tmx, n_tiles=n_tiles, nsub=nsub),
        grid_spec=grid_spec,
        out_shape=jax.ShapeDtypeStruct((n_slots * nsub, LANES), F32),
        compiler_params=_cparams(("arbitrary",)),
        name="dispatch",
    )(zflag, pos12, t_tiles)


XS_RING = 3


def _experts_body(te_ref, nused_ref, xs_hbm, wgu_ref, wd_ref, ys_ref, xbuf, sems, *, tmx):
    i = pl.program_id(0)
    n = pl.num_programs(0)
    rows = xbuf.shape[1]

    def fetch(tile):
        slot = tile % XS_RING
        return pltpu.make_async_copy(xs_hbm.at[pl.ds(pl.multiple_of(tile * rows, rows), rows), :],
                                     xbuf.at[slot], sems.at[slot])

    @pl.when(i == 0)
    def _():
        for k in range(XS_RING - 1):
            @pl.when(k < n)
            def _():
                fetch(k).start()

    @pl.when(i + XS_RING - 1 < n)
    def _():
        fetch(i + XS_RING - 1).start()

    fetch(i).wait()
    xs_ref = xbuf.at[i % XS_RING]

    @pl.when(i < nused_ref[0])
    def _():
        hgu = _dot(_tiles_to_rows(xs_ref, tmx).astype(BF16), wgu_ref[0])
        a = hgu[:, :EXPERT_FF]
        hid = a * _sigmoid(a) * hgu[:, EXPERT_FF:]
        _rows_to_tiles(ys_ref, _dot(hid.astype(BF16), wd_ref[0]))

    @pl.when(pl.program_id(0) >= nused_ref[0])
    def _():
        ys_ref[...] = jnp.zeros_like(ys_ref)


def _experts(tile_expert, n_used, xs, wgu, wd, tmx):
    d = wgu.shape[1]
    nsub = d // LANES
    n_slots = xs.shape[0] // nsub
    grid_spec = pltpu.PrefetchScalarGridSpec(
        num_scalar_prefetch=2,
        grid=(n_slots // tmx,),
        in_specs=[pl.BlockSpec(memory_space=pl.ANY),
                  pl.BlockSpec((1, d, 2 * EXPERT_FF), lambda i, te, nu: (te[i], 0, 0)),
                  pl.BlockSpec((1, EXPERT_FF, d), lambda i, te, nu: (te[i], 0, 0))],
        out_specs=pl.BlockSpec((tmx * nsub, LANES), lambda i, te, nu: (i, 0)),
        scratch_shapes=[pltpu.VMEM((XS_RING, tmx * nsub, LANES), F32), pltpu.SemaphoreType.DMA((XS_RING,))],
    )
    return pl.pallas_call(
        functools.partial(_experts_body, tmx=tmx),
        grid_spec=grid_spec,
        out_shape=jax.ShapeDtypeStruct(xs.shape, F32),
        compiler_params=_cparams(("arbitrary",)),
        name="experts",
    )(tile_expert, n_used, xs, wgu, wd)


def _final_body(pos_ref, posn_ref, x_ref, rt_ref, p_ref, gple_ref, wpg_ref, wpp_ref, gfin_ref, ys_ref,
                o_ref, gbuf, sems, *, tf, nsub):
    i = pl.program_id(0)
    n = pl.num_programs(0)
    slot = i % 2

    def gather(pref, dst_slot):
        _issue_row_copies(pref, tf, nsub, lambda k, tok, src: pltpu.make_async_copy(
            ys_ref.at[src, :], gbuf.at[dst_slot, k, tok, :], sems.at[dst_slot]))

    @pl.when(i == 0)
    def _():
        gather(pos_ref, 0)

    @pl.when(i + 1 < n)
    def _():
        gather(posn_ref, 1 - slot)

    for k in range(2):
        pltpu.make_async_copy(ys_ref.at[pl.ds(0, tf * nsub), :], gbuf.at[slot, k], sems.at[slot]).wait()

    rt = rt_ref[...]
    g1 = rt[:, RT_G1:RT_G1 + 1]
    g2 = rt[:, RT_G2:RT_G2 + 1]
    x = x_ref[...] + g1 * _tiles_to_rows(gbuf.at[slot, 0], tf) + g2 * _tiles_to_rows(gbuf.at[slot, 1], tf)
    r = _rms(x, gple_ref[...]).astype(BF16)
    ple_gate = _sigmoid(_dot(r, wpg_ref[...]))
    proj = _dot(p_ref[...].astype(BF16), wpp_ref[...])
    o_ref[...] = _rms(x + ple_gate * proj, gfin_ref[...])


def _final(pos12, x1, rt, p2d, g_ple, wpg, wpp, g_final, ys, tf):
    t, d = x1.shape
    nsub = d // LANES
    n = t // tf
    row = lambda i: (i, 0)
    const = lambda i: (0, 0)
    smem_blk = lambda imap: pl.BlockSpec((1, 1, 2 * tf), imap, memory_space=pltpu.SMEM)
    return pl.pallas_call(
        functools.partial(_final_body, tf=tf, nsub=nsub),
        grid=(n,),
        in_specs=[smem_blk(lambda i: (i, 0, 0)), smem_blk(lambda i: (jnp.minimum(i + 1, n - 1), 0, 0)),
                  pl.BlockSpec((tf, d), row), pl.BlockSpec((tf, LANES), row),
                  pl.BlockSpec((tf, p2d.shape[1]), row),
                  pl.BlockSpec((1, d), const), pl.BlockSpec(wpg.shape, const),
                  pl.BlockSpec(wpp.shape, const), pl.BlockSpec((1, d), const),
                  pl.BlockSpec(memory_space=pl.ANY)],
        out_specs=pl.BlockSpec((tf, d), row),
        out_shape=jax.ShapeDtypeStruct((t, d), F32),
        scratch_shapes=[pltpu.VMEM((2, 2, tf * nsub, LANES), F32), pltpu.SemaphoreType.DMA((2,))],
        compiler_params=_cparams(("arbitrary",)),
        name="final",
    )(pos12, pos12, x1, rt, p2d, g_ple, wpg, wpp, g_final, ys)


def _pad_lanes(v, lane0):
    return jnp.zeros((1, LANES), F32).at[0, lane0:lane0 + v.shape[0]].set(v.astype(F32))


def _tile(n, pref):
    return pref if n % pref == 0 else n


def _layer(x, p_l, g_mix, w_in, b_forget, conv_w, a_log, dt_bias, g_onorm, w_o_fox, w_o_delta, w_out,
           g_ffn, w_group, b_group, w_router, b_router, w_gate, w_up, w_down, g_ple, w_ple_gate, w_ple_proj,
           g_post, apply_post):
    b, s, d = x.shape
    t = b * s
    x2d = x.reshape(t, d)

    o_ff = 3 * FOX_WIDTH
    o_qkv = o_ff + FOX_HEADS
    o_da = o_qkv + 3 * GDN_WIDTH
    o_db = o_da + GDN_HEADS
    o_dz = o_db + GDN_HEADS
    o_gf = o_dz + GDN_WIDTH
    o_gd = o_gf + d
    w_small = jnp.concatenate(
        [w_in[:, o_ff:o_qkv], w_in[:, o_da:o_db], w_in[:, o_db:o_dz],
         jnp.zeros((d, LANES - FOX_HEADS - 2 * GDN_HEADS), w_in.dtype)], axis=1)
    wq = w_in[:, :o_ff].astype(BF16)
    wg = w_in[:, o_qkv:o_da].astype(BF16)
    wzs = jnp.concatenate([w_in[:, o_dz:o_gf], w_small], axis=1).astype(BF16)
    wgf = w_in[:, o_gf:o_gd].astype(BF16)
    wgd = w_in[:, o_gd:].astype(BF16)

    zq, zg, dz, small, gf, gd = _inproj(x2d, g_mix.reshape(1, d), wq, wg, wzs, wgf, wgd, _tile(t, 512))

    par = jnp.concatenate(
        [_pad_lanes(b_forget, 0) + _pad_lanes(dt_bias, FOX_HEADS), _pad_lanes(a_log, FOX_HEADS),
         jnp.zeros((6, LANES), F32)], axis=0)
    qn, kn, vv, col = _gdnprep(zg.reshape(b, s, 3 * GDN_WIDTH), conv_w.astype(F32),
                               small.reshape(b, s, LANES), par, _tile(s, 512))

    nc = s // GDN_CHUNK
    gc = col[:, :, FOX_HEADS:FOX_HEADS + GDN_HEADS].reshape(b, nc, GDN_CHUNK, GDN_HEADS)
    gcrow = jnp.transpose(gc, (0, 1, 3, 2)).reshape(b, nc, 1, GDN_HEADS * GDN_CHUNK)

    y_fox = _fox(zq.reshape(b, s, 3 * FOX_WIDTH), col, _tile(s, 1024))
    o_gdn = _gdn(qn, kn, vv, col, gcrow, _tile(s, 512))

    w_r = jnp.concatenate([w_group, w_router,
                           jnp.zeros((d, LANES - N_GROUPS - N_EXPERTS), w_group.dtype)], axis=1).astype(F32)
    wrh = w_r.astype(BF16)
    wrl = (w_r - wrh.astype(F32)).astype(BF16)
    br = _pad_lanes(b_group, 0) + _pad_lanes(b_router, EXPERT_LANE0)
    x1, tpk, rt, cnt = _combine(
        y_fox.reshape(t, FOX_WIDTH), o_gdn.reshape(t, GDN_WIDTH), dz, gf, gd, x2d,
        w_o_fox.astype(BF16), w_o_delta.astype(BF16), w_out.astype(BF16),
        g_onorm.reshape(1, GDN_HEAD_DIM).astype(F32), g_ffn.reshape(1, d).astype(F32), wrh, wrl, br,
        _tile(t, 512))

    tmx = MOE_SLOT_TILE
    n_slots = 2 * t + N_EXPERTS * tmx
    counts = cnt[0, EXPERT_LANE0:EXPERT_LANE0 + N_EXPERTS].astype(jnp.int32)
    padded = (counts + tmx - 1) // tmx * tmx
    ends = jnp.cumsum(padded)
    pos = _slots(rt, cnt, tmx, _tile(t, 2048))
    pos1, pos2 = pos[0], pos[1]
    tile_start = jnp.arange(n_slots // tmx, dtype=jnp.int32) * tmx
    tile_expert = jnp.minimum(jnp.sum(tile_start[:, None] >= ends[None, :], axis=1), N_EXPERTS - 1).astype(jnp.int32)
    n_used = (ends[-1:] // tmx).astype(jnp.int32)
    last_of_expert = jnp.any(jnp.logical_and(tile_start[:, None] + tmx == ends[None, :], padded[None, :] > 0), axis=1)
    zflag = jnp.logical_or(last_of_expert, tile_start >= ends[-1]).astype(jnp.int32)

    def tiled_pos(tile):
        n = t // tile
        return jnp.concatenate([pos1.reshape(n, 1, tile), pos2.reshape(n, 1, tile)], axis=2)

    td = _tile(t, 2048)
    nsub = d // LANES
    xs = _dispatch(zflag, tpk, tiled_pos(td), n_slots, td, tmx, nsub)
    wgu = jnp.concatenate([w_gate, w_up], axis=2).astype(BF16)
    ys = _experts(tile_expert, n_used, xs, wgu, w_down.astype(BF16), tmx)

    tf = _tile(t, 256)
    out = _final(tiled_pos(tf), x1, rt, p_l.reshape(t, -1), g_ple.reshape(1, d).astype(F32),
                 w_ple_gate.astype(BF16), w_ple_proj.astype(BF16), g_post.reshape(1, d).astype(F32), ys, tf)
    return out.reshape(b, s, d)


def kernel(x, p, g_mix, w_in, b_forget, conv_w, a_log, dt_bias, g_onorm, w_o_fox, w_o_delta, w_out,
           g_ffn, w_group, b_group, w_router, b_router, w_gate, w_up, w_down, g_ple, w_ple_gate, w_ple_proj,
           g_final):
    depth = p.shape[0]
    assert depth == 1, "the final rmsnorm is fused into the last layer's epilogue; depth 1 only"
    i = 0
    return _layer(x, p[i], g_mix[i], w_in[i], b_forget[i], conv_w[i], a_log[i], dt_bias[i], g_onorm[i],
                  w_o_fox[i], w_o_delta[i], w_out[i], g_ffn[i], w_group[i], b_group[i], w_router[i],
                  b_router[i], w_gate[i], w_up[i], w_down[i], g_ple[i], w_ple_gate[i], w_ple_proj[i],
                  g_final, True)
```
